```python
import jax, jax.numpy as jnp
from jax import lax
import numpy as np


D_MODEL = 1024
BATCH = 8
SEQ = 4096
DEPTH = 4

BRANCH_WIDTH = D_MODEL // 2
HEAD_DIM = 64
N_BRANCH = 4
CONV_WIDTH_A = BRANCH_WIDTH
CONV_K = 3
NSA_HEADS = BRANCH_WIDTH // HEAD_DIM
NSA_KV_GROUPS = 2
NSA_REP = NSA_HEADS // NSA_KV_GROUPS
NSA_WIDTH = NSA_HEADS * HEAD_DIM
KV_WIDTH = NSA_KV_GROUPS * HEAD_DIM
CMP_BLOCK = 32
CMP_STRIDE = 16
SLC_BLOCK = 64
SLC_TOPN = 16
WIN = 512
Q_BLOCK = 128
NEG = -1e30
FORCE = 1e6
POOL_WINDOWS = (2, 4, 8, 16)
POOL_GROUPS = len(POOL_WINDOWS)
POOL_GROUP_DIM = BRANCH_WIDTH // POOL_GROUPS
POOL_WIDTH = BRANCH_WIDTH
SGU_CHUNK = 128
SGU_GROUPS = 4
SGU_WIDTH = BRANCH_WIDTH
SGU_GROUP_DIM = SGU_WIDTH // SGU_GROUPS
D_FF = 2816
FF_CONV_K = 3
RMS_EPS = 1e-6
IN_SPLITS = (CONV_WIDTH_A, CONV_WIDTH_A, CONV_WIDTH_A, NSA_WIDTH, 6 * KV_WIDTH, 3 * NSA_HEADS,
             POOL_WIDTH, 2 * SGU_WIDTH, N_BRANCH * D_MODEL)
N_IN = sum(IN_SPLITS)

kernel_name = 'hybrid_gated_parallel_mixer'


def rms_norm(x, g, eps=RMS_EPS):
    xf = x.astype(jnp.float32)
    y = xf * lax.rsqrt(jnp.mean(xf * xf, axis=-1, keepdims=True) + eps)
    return (y * g.astype(jnp.float32)).astype(x.dtype)


def causal_dwconv(x, w):
    k, s = w.shape[0], x.shape[1]
    xp = jnp.pad(x, ((0, 0), (k - 1, 0), (0, 0)))
    return sum(w[j] * xp[:, j:j + s] for j in range(k))


def split_columns(z):
    offs, acc = [], 0
    for n in IN_SPLITS[:-1]:
        acc += n
        offs.append(acc)
    return jnp.split(z, offs, axis=-1)


def overlap_matrix(n_cmp, n_slc):
    cs = np.arange(n_cmp) * CMP_STRIDE
    ce = cs + CMP_BLOCK
    ss = np.arange(n_slc) * SLC_BLOCK
    se = ss + SLC_BLOCK
    ov = np.minimum(ce[:, None], se[None]) - np.maximum(cs[:, None], ss[None])
    return np.clip(ov, 0, None).astype(np.float32) / CMP_BLOCK


def nsa_mixer(q, kv, gates, qk_g, cmp_pe, cmp_w1, cmp_w2):
    b, s, _ = q.shape
    g_, r_, dk = NSA_KV_GROUPS, NSA_REP, HEAD_DIM
    q = rms_norm(q.reshape(b, s, g_, r_, dk), qk_g[0])
    kv = kv.reshape(b, s, 6, g_, dk)
    k_c, v_c, k_s, v_s, k_w, v_w = [kv[:, :, i] for i in range(6)]
    n_cmp = (s - CMP_BLOCK) // CMP_STRIDE + 1
    cidx = np.arange(n_cmp)[:, None] * CMP_STRIDE + np.arange(CMP_BLOCK)[None]

    def compress(t, j):
        blk = t[:, cidx] + cmp_pe[j][None, None, :, None, :]
        flat = blk.transpose(0, 1, 3, 2, 4).reshape(b, n_cmp, g_, CMP_BLOCK * dk)
        return jax.nn.silu(flat @ cmp_w1[j]) @ cmp_w2[j]

    kc = rms_norm(compress(k_c, 0), qk_g[1])
    vc = compress(v_c, 1)
    c_end = jnp.asarray(cidx[:, -1])
    n_slc = s // SLC_BLOCK
    top_n = min(SLC_TOPN, n_slc)
    ks = rms_norm(k_s, qk_g[2]).reshape(b, n_slc, SLC_BLOCK, g_, dk).transpose(0, 3, 1, 2, 4)
    vs = v_s.reshape(b, n_slc, SLC_BLOCK, g_, dk).transpose(0, 3, 1, 2, 4)
    ov = jnp.asarray(overlap_matrix(n_cmp, n_slc))
    pad = ((0, 0), (WIN, 0), (0, 0), (0, 0))
    kw = jnp.pad(rms_norm(k_w, qk_g[3]), pad)
    vw = jnp.pad(v_w, pad)
    scale = HEAD_DIM ** -0.5
    bi = jnp.arange(b)[:, None, None, None]
    gi = jnp.arange(g_)[None, None, :, None]
    jblk = jnp.arange(n_slc)

    def block(i):
        start = i * Q_BLOCK
        qb = lax.dynamic_slice_in_dim(q, start, Q_BLOCK, axis=1)
        t = start + jnp.arange(Q_BLOCK)
        sc = jnp.einsum('bqgrd,bngd->bqgrn', qb, kc).astype(jnp.float32) * scale
        cval = c_end[None, :] <= t[:, None]
        pc = jax.nn.softmax(jnp.where(cval[None, :, None, None, :], sc, NEG), axis=-1)
        pc = pc * jnp.any(cval, axis=-1)[None, :, None, None, None]
        o_c = jnp.einsum('bqgrn,bngd->bqgrd', pc.astype(vc.dtype), vc)
        imp = jnp.einsum('bqgrn,nj->bqgj', pc, ov)
        jt = (t // SLC_BLOCK)[:, None]
        future = jblk[None, :] > jt
        forced = (jblk[None, :] == 0) | (jblk[None, :] == jt) | (jblk[None, :] == jt - 1)
        imp = jnp.where(forced[None, :, None, :], FORCE, jnp.where(future[None, :, None, :], NEG, imp))
        _, idx = lax.top_k(imp, top_n)
        ksel = ks[bi, gi, idx].reshape(b, Q_BLOCK, g_, top_n * SLC_BLOCK, dk)
        vsel = vs[bi, gi, idx].reshape(b, Q_BLOCK, g_, top_n * SLC_BLOCK, dk)
        kpos = (idx[..., None] * SLC_BLOCK + jnp.arange(SLC_BLOCK)).reshape(b, Q_BLOCK, g_, top_n * SLC_BLOCK)
        ss_ = jnp.einsum('bqgrd,bqgkd->bqgrk', qb, ksel).astype(jnp.float32) * scale
        smask = (kpos <= t[None, :, None, None])[:, :, :, None, :]
        ps = jax.nn.softmax(jnp.where(smask, ss_, NEG), axis=-1)
        o_s = jnp.einsum('bqgrk,bqgkd->bqgrd', ps.astype(vsel.dtype), vsel)
        kwb = lax.dynamic_slice_in_dim(kw, start, Q_BLOCK + WIN, axis=1)
        vwb = lax.dynamic_slice_in_dim(vw, start, Q_BLOCK + WIN, axis=1)
        kp = start - WIN + jnp.arange(Q_BLOCK + WIN)
        wmask = (kp[None, :] <= t[:, None]) & (kp[None, :] > t[:, None] - WIN) & (kp[None, :] >= 0)
        sw = jnp.einsum('bqgrd,bkgd->bqgrk', qb, kwb).astype(jnp.float32) * scale
        pw = jax.nn.softmax(jnp.where(wmask[None, :, None, None, :], sw, NEG), axis=-1)
        o_w = jnp.einsum('bqgrk,bkgd->bqgrd', pw.astype(vwb.dtype), vwb)
        return o_c, o_s, o_w

    o_c, o_s, o_w = lax.map(block, jnp.arange(s // Q_BLOCK))

    def unblock(o):
        return o.transpose(1, 0, 2, 3, 4, 5).reshape(b, s, NSA_HEADS, dk)

    gt = jax.nn.sigmoid(gates.reshape(b, s, NSA_HEADS, 3))
    o = gt[..., 0:1] * unblock(o_c) + gt[..., 1:2] * unblock(o_s) + gt[..., 2:3] * unblock(o_w)
    return o.reshape(b, s, NSA_WIDTH)


def pool_mixer(p, w_pool, scale):
    b, s, c = p.shape
    maxw = max(POOL_WINDOWS)
    cs = jnp.pad(jnp.cumsum(p.astype(jnp.float32), axis=1), ((0, 0), (maxw, 0), (0, 0)))
    cnt = jnp.arange(1, s + 1).astype(jnp.float32)
    groups = []
    for gi, w in enumerate(POOL_WINDOWS):
        c0, c1 = gi * POOL_GROUP_DIM, (gi + 1) * POOL_GROUP_DIM
        wsum = cs[:, maxw:, c0:c1] - cs[:, maxw - w:maxw - w + s, c0:c1]
        groups.append(wsum / jnp.minimum(cnt, float(w))[None, :, None])
    pooled = jnp.concatenate(groups, axis=-1).astype(p.dtype) - p
    y = jnp.einsum('bsgc,gcd->bsgd', pooled.reshape(b, s, POOL_GROUPS, POOL_GROUP_DIM), w_pool)
    return y.reshape(b, s, c) * scale


def spatial_gating(z, norm_g, w_s, b_s):
    b, s, _ = z.shape
    z = jax.nn.gelu(z, approximate=False)
    u, v = jnp.split(z, 2, axis=-1)
    v = rms_norm(v, norm_g).reshape(b, s // SGU_CHUNK, SGU_CHUNK, SGU_GROUPS, SGU_GROUP_DIM)
    w = w_s * jnp.tril(jnp.ones((SGU_CHUNK, SGU_CHUNK), w_s.dtype))
    mixed = jnp.einsum('gts,bnsgc->bntgc', w, v) + b_s.T[None, None, :, :, None]
    return u * mixed.reshape(b, s, SGU_WIDTH)


def setup_inputs(seed: int = 0) -> dict:
    key = jax.random.key(seed)
    k = jax.random.split(key, 20)
    f32 = jnp.float32
    nrm = lambda kk, shape: jax.random.normal(kk, shape, f32)
    L = DEPTH
    return {
        'x': nrm(k[0], (BATCH, SEQ, D_MODEL)),
        'norm1_g': 1.0 + 0.1 * nrm(k[1], (L, D_MODEL)),
        'w_in': nrm(k[2], (L, D_MODEL, N_IN)) * D_MODEL ** -0.5,
        'conv_a_w': nrm(k[3], (L, CONV_K, CONV_WIDTH_A)) * CONV_K ** -0.5,
        'qk_norm_g': 1.0 + 0.1 * nrm(k[4], (L, 4, HEAD_DIM)),
        'cmp_pe': 0.5 * nrm(k[5], (L, 2, CMP_BLOCK, HEAD_DIM)),
        'cmp_w1': nrm(k[6], (L, 2, CMP_BLOCK * HEAD_DIM, HEAD_DIM)) * (CMP_BLOCK * HEAD_DIM) ** -0.5,
        'cmp_w2': nrm(k[7], (L, 2, HEAD_DIM, HEAD_DIM)) * HEAD_DIM ** -0.5,
        'pool_w': nrm(k[8], (L, POOL_GROUPS, POOL_GROUP_DIM, POOL_GROUP_DIM)) * POOL_GROUP_DIM ** -0.5,
        'pool_scale': 1.0 + 0.1 * nrm(k[9], (L, POOL_WIDTH)),
        'sgu_norm_g': 1.0 + 0.1 * nrm(k[10], (L, SGU_WIDTH)),
        'sgu_w': nrm(k[11], (L, SGU_GROUPS, SGU_CHUNK, SGU_CHUNK)) * SGU_CHUNK ** -0.5,
        'sgu_b': 1.0 + 0.1 * nrm(k[12], (L, SGU_GROUPS, SGU_CHUNK)),
        'w_branch': nrm(k[13], (L, N_BRANCH, BRANCH_WIDTH, D_MODEL)) * BRANCH_WIDTH ** -0.5,
        'w_o': nrm(k[14], (L, D_MODEL, D_MODEL)) * D_MODEL ** -0.5,
        'norm2_g': 1.0 + 0.1 * nrm(k[15], (L, D_MODEL)),
        'w_up': nrm(k[16], (L, D_MODEL, 2 * D_FF)) * D_MODEL ** -0.5,
        'conv_ff_w': nrm(k[17], (L, FF_CONV_K, D_FF)) * FF_CONV_K ** -0.5,
        'w_down': nrm(k[18], (L, D_FF, D_MODEL)) * D_FF ** -0.5,
    }


def reference(x, norm1_g, w_in, conv_a_w, qk_norm_g, cmp_pe, cmp_w1, cmp_w2, pool_w, pool_scale,
              sgu_norm_g, sgu_w, sgu_b, w_branch, w_o, norm2_g, w_up, conv_ff_w, w_down):
    b, s, d = x.shape
    for l in range(DEPTH):
        h = rms_norm(x, norm1_g[l])
        a_b, a_c, a_x, q, kv, nsa_g, pool_in, sgu_in, merge_g = split_columns(h @ w_in[l])
        out_a = a_b * causal_dwconv(a_c * a_x, conv_a_w[l])
        out_b = nsa_mixer(q, kv, nsa_g, qk_norm_g[l], cmp_pe[l], cmp_w1[l], cmp_w2[l])
        out_c = pool_mixer(pool_in, pool_w[l], pool_scale[l])
        out_d = spatial_gating(sgu_in, sgu_norm_g[l], sgu_w[l], sgu_b[l])
        gm = jax.nn.sigmoid(merge_g.reshape(b, s, N_BRANCH, d))
        merged = (gm[:, :, 0] * (out_a @ w_branch[l, 0]) + gm[:, :, 1] * (out_b @ w_branch[l, 1])
                  + gm[:, :, 2] * (out_c @ w_branch[l, 2]) + gm[:, :, 3] * (out_d @ w_branch[l, 3]))
        x = x + merged @ w_o[l]
        h2 = rms_norm(x, norm2_g[l])
        gate, up = jnp.split(h2 @ w_up[l], 2, axis=-1)
        x = x + (jax.nn.silu(causal_dwconv(gate, conv_ff_w[l])) * up) @ w_down[l]
    return x
```

```python
import functools

import jax
import jax.numpy as jnp
import numpy as np
from jax import lax
from jax.experimental import pallas as pl
from jax.experimental.pallas import tpu as pltpu

F32 = jnp.float32
BF16 = jnp.bfloat16

HEAD_DIM = 64
NSA_HEADS = 8
NSA_GROUPS = 2
NSA_REP = NSA_HEADS // NSA_GROUPS
CMP_BLOCK = 32
CMP_STRIDE = 16
SLC_BLOCK = 64
SLC_TOPN = 16
WIN = 512
NEG = -1e30
FORCE = 1e6
RMS_EPS = 1e-6
POOL_WINDOWS = (2, 4, 8, 16)
SGU_CHUNK = 128
SGU_GROUPS = 4
BRANCH_WIDTH = 512
N_BRANCH = 4

LANES = 128
HALO = 16
QT = 128
SEL_CHUNK = 256
WIN_CHUNK = 128
VMEM_LIMIT = 56 * 1024 * 1024


def _params(n_axes):
    return pltpu.CompilerParams(dimension_semantics=("arbitrary",) * n_axes, vmem_limit_bytes=VMEM_LIMIT)


def _dot(a, b):
    return jnp.dot(a, b, preferred_element_type=F32)


def _norm_matmul_kernel(x_ref, g_ref, w_ref, o_ref, h_ref):
    @pl.when(pl.program_id(1) == 0)
    def _():
        x = x_ref[...]
        ms = jnp.mean(x * x, axis=-1, keepdims=True)
        h_ref[...] = ((x * lax.rsqrt(ms + RMS_EPS)) * g_ref[...]).astype(BF16)

    o_ref[...] = _dot(h_ref[...], w_ref[...]).astype(o_ref.dtype)


def _norm_matmul(x, g, w, out_dtype, tm, tn, name):
    t, d = x.shape
    n = w.shape[1]
    return pl.pallas_call(
        _norm_matmul_kernel,
        out_shape=jax.ShapeDtypeStruct((t, n), out_dtype),
        grid=(t // tm, n // tn),
        in_specs=[
            pl.BlockSpec((tm, d), lambda i, j: (i, 0)),
            pl.BlockSpec((1, d), lambda i, j: (0, 0)),
            pl.BlockSpec((d, tn), lambda i, j: (0, j)),
        ],
        out_specs=pl.BlockSpec((tm, tn), lambda i, j: (i, j)),
        scratch_shapes=[pltpu.VMEM((tm, d), BF16)],
        compiler_params=_params(2),
        name=name,
    )(x, g.reshape(1, d), w)


def _group_rms(x, g):
    lane = lax.broadcasted_iota(jnp.int32, x.shape, 1)
    lo = lane < HEAD_DIM
    x2 = x * x
    s_lo = jnp.sum(jnp.where(lo, x2, 0.0), axis=-1, keepdims=True)
    s_hi = jnp.sum(jnp.where(lo, 0.0, x2), axis=-1, keepdims=True)
    ms = jnp.where(lo, s_lo, s_hi) * (1.0 / HEAD_DIM)
    return (x * lax.rsqrt(ms + RMS_EPS)) * g


def _kv_prep_kernel(ks_ref, vs_ref, kw_ref, vw_ref, gs_ref, gw_ref, ksa_ref, vst_ref, kwn_ref, vwt_ref, *, ts):
    j = pl.program_id(1)
    ksn = _group_rms(ks_ref[...], gs_ref[...]).astype(BF16)
    row = lax.broadcasted_iota(jnp.int32, (ts, LANES), 0) + j * ts
    col = lax.broadcasted_iota(jnp.int32, (ts, LANES), 1)
    onehot = jnp.where(row // SLC_BLOCK == col, 1.0, 0.0).astype(BF16)
    ksa_ref[0, :, 0:LANES] = ksn
    ksa_ref[0, :, LANES:2 * LANES] = onehot
    kwn_ref[0] = _group_rms(kw_ref[...], gw_ref[...]).astype(BF16)
    vs = vs_ref[...]
    for c in range(ts // SEL_CHUNK):
        vst_ref[0, c] = vs[c * SEL_CHUNK:(c + 1) * SEL_CHUNK, :].T.astype(BF16)
    vw = vw_ref[...]
    for c in range(ts // WIN_CHUNK):
        vwt_ref[0, c] = vw[c * WIN_CHUNK:(c + 1) * WIN_CHUNK, :].T.astype(BF16)


def _kv_prep(znsa, gs, gw, b, s, ts=512):
    nt = s // ts
    col = lambda c: pl.BlockSpec((ts, LANES), lambda bi, j, c=c: (bi * nt + j, c))
    vec = pl.BlockSpec((1, LANES), lambda bi, j: (0, 0))
    return pl.pallas_call(
        functools.partial(_kv_prep_kernel, ts=ts),
        out_shape=(
            jax.ShapeDtypeStruct((b, s, 2 * LANES), BF16),
            jax.ShapeDtypeStruct((b, s // SEL_CHUNK, LANES, SEL_CHUNK), BF16),
            jax.ShapeDtypeStruct((b, s, LANES), BF16),
            jax.ShapeDtypeStruct((b, s // WIN_CHUNK, LANES, WIN_CHUNK), BF16),
        ),
        grid=(b, nt),
        in_specs=[col(6), col(7), col(8), col(9), vec, vec],
        out_specs=(
            pl.BlockSpec((1, ts, 2 * LANES), lambda bi, j: (bi, j, 0)),
            pl.BlockSpec((1, ts // SEL_CHUNK, LANES, SEL_CHUNK), lambda bi, j: (bi, j, 0, 0)),
            pl.BlockSpec((1, ts, LANES), lambda bi, j: (bi, j, 0)),
            pl.BlockSpec((1, ts // WIN_CHUNK, LANES, WIN_CHUNK), lambda bi, j: (bi, j, 0, 0)),
        ),
        compiler_params=_params(2),
        name="kv_prep",
    )(znsa, znsa, znsa, znsa, gs, gw)


def _compress_kernel(kc_ref, vc_ref, pe_ref, w1_ref, w2_ref, g_ref, kco_ref, vct_ref, u_ref, *, nc):
    half = CMP_BLOCK // 2
    for j, src in enumerate((kc_ref, vc_ref)):
        for l in range(half):
            u_ref[:, l * LANES:(l + 1) * LANES] = src[pl.ds(l, nc, stride=CMP_STRIDE), :]
        u = u_ref[...]
        top = _dot((u + pe_ref[j, 0:1, :]).astype(BF16), w1_ref[j, 0])
        bot = _dot((u + pe_ref[j, 1:2, :]).astype(BF16), w1_ref[j, 1])
        pre = top + pltpu.roll(bot, nc - 1, 0)
        hid = pre * jax.nn.sigmoid(pre)
        out = _dot(hid.astype(BF16), w2_ref[j])
        if j == 0:
            kco_ref[0] = _group_rms(out, g_ref[...]).astype(BF16)
        else:
            vct_ref[0] = out.T.astype(BF16)


def _compress(znsa, pe2, w1bd, w2bd, g1, b, s):
    nc = s // CMP_STRIDE
    kw = (CMP_BLOCK // 2) * LANES
    return pl.pallas_call(
        functools.partial(_compress_kernel, nc=nc),
        out_shape=(
            jax.ShapeDtypeStruct((b, nc, LANES), BF16),
            jax.ShapeDtypeStruct((b, LANES, nc), BF16),
        ),
        grid=(b,),
        in_specs=[
            pl.BlockSpec((s, LANES), lambda bi: (bi, 4)),
            pl.BlockSpec((s, LANES), lambda bi: (bi, 5)),
            pl.BlockSpec((2, 2, kw), lambda bi: (0, 0, 0)),
            pl.BlockSpec((2, 2, kw, LANES), lambda bi: (0, 0, 0, 0)),
            pl.BlockSpec((2, LANES, LANES), lambda bi: (0, 0, 0)),
            pl.BlockSpec((1, LANES), lambda bi: (0, 0)),
        ],
        out_specs=(
            pl.BlockSpec((1, nc, LANES), lambda bi: (bi, 0, 0)),
            pl.BlockSpec((1, LANES, nc), lambda bi: (bi, 0, 0)),
        ),
        scratch_shapes=[pltpu.VMEM((nc, kw), F32)],
        compiler_params=_params(1),
        name="compress",
    )(znsa, znsa, pe2, w1bd, w2bd, g1)


def _rank_desc(imp, ns):
    nq = imp.shape[1]
    sub = lax.broadcasted_iota(jnp.int32, (8, nq), 0)
    blocks = [imp[8 * a:8 * a + 8, :] for a in range(ns // 8)]
    rank = [jnp.zeros((8, nq), F32) for _ in blocks]
    for k in range(ns):
        ka, kr = divmod(k, 8)
        rk = jnp.broadcast_to(imp[k:k + 1, :], (8, nq))
        for a, blk in enumerate(blocks):
            if a > ka:
                ahead = rk >= blk
            elif a < ka:
                ahead = rk > blk
            else:
                rank[a] = rank[a] + jnp.where(sub > kr, jnp.where(rk >= blk, 1.0, 0.0), jnp.where(rk > blk, 1.0, 0.0))
                continue
            rank[a] = rank[a] + jnp.where(ahead, 1.0, 0.0)
    return jnp.concatenate(rank, axis=0)


def _nsa_attn_kernel(q_ref, gate_ref, gq_ref, kc_ref, vct_ref, ovt_ref, ksa_ref, vst_ref, kw_ref, vwt_ref,
                     o_ref, qa_ref, m_ref, l_ref, acc_ref, *, ns, nc, topn):
    i = pl.program_id(1)
    start = i * QT
    nl = NSA_HEADS * QT
    gl = NSA_REP * QT

    qt = q_ref[...].T.reshape(NSA_HEADS, HEAD_DIM, QT)
    ms = jnp.mean(qt * qt, axis=1, keepdims=True)
    qn = ((qt * lax.rsqrt(ms + RMS_EPS)) * gq_ref[...][None]) * (HEAD_DIM ** -0.5)
    qa_ref[...] = jnp.zeros(qa_ref.shape, BF16)
    for h in range(NSA_HEADS):
        g = h // NSA_REP
        qa_ref[g * HEAD_DIM:(g + 1) * HEAD_DIM, h * QT:(h + 1) * QT] = qn[h].astype(BF16)
    qk = qa_ref[0:LANES, :]

    def tq(shape):
        return start + (lax.broadcasted_iota(jnp.int32, shape, 1) & (QT - 1))

    def krow(shape, base):
        return base + lax.broadcasted_iota(jnp.int32, shape, 0)

    sc = _dot(kc_ref[0], qk)
    cval = krow((nc, nl), 0) * CMP_STRIDE + (CMP_BLOCK - 1) <= tq((nc, nl))
    sc = jnp.where(cval, sc, NEG)
    e = jnp.exp(sc - jnp.max(sc, axis=0, keepdims=True))
    pc = e / jnp.sum(e, axis=0, keepdims=True)
    pc = jnp.where(tq((1, nl)) >= CMP_BLOCK - 1, pc, 0.0)
    pcb = pc.astype(BF16)
    o_cmp = jnp.concatenate(
        [_dot(vct_ref[0, g * HEAD_DIM:(g + 1) * HEAD_DIM, :], pcb[:, g * gl:(g + 1) * gl]) for g in range(NSA_GROUPS)],
        axis=1)

    jb = krow((ns, QT), 0)
    jt = (start + lax.broadcasted_iota(jnp.int32, (ns, QT), 1)) // SLC_BLOCK
    forced = (jb == 0) | (jb == jt) | (jb == jt - 1)
    future = jb > jt
    ovt = ovt_ref[...]
    for g in range(NSA_GROUPS):
        psum = pc[:, g * gl:g * gl + QT]
        for r in range(1, NSA_REP):
            psum = psum + pc[:, g * gl + r * QT:g * gl + (r + 1) * QT]
        p_hi = psum.astype(BF16)
        rem = psum - p_hi.astype(F32)
        p_mid = rem.astype(BF16)
        p_lo = (rem - p_mid.astype(F32)).astype(BF16)
        imp = (_dot(ovt, p_hi) + _dot(ovt, p_mid)) + _dot(ovt, p_lo)
        imp = jnp.where(forced, FORCE, jnp.where(future, NEG, imp))
        sel = (_rank_desc(imp, ns) < topn) & jnp.logical_not(future)
        bias = jnp.where(sel, 0.0, NEG).astype(BF16)
        for r in range(NSA_REP):
            h = g * NSA_REP + r
            qa_ref[LANES:LANES + ns, h * QT:(h + 1) * QT] = bias

    def update(s, vt, first):
        cm = jnp.max(s, axis=0, keepdims=True)
        if first:
            m_new = cm
        else:
            m_old = m_ref[...]
            m_new = jnp.maximum(m_old, cm)
        p = jnp.exp(s - m_new)
        pb = p.astype(BF16)
        pv = jnp.concatenate(
            [_dot(vt[g * HEAD_DIM:(g + 1) * HEAD_DIM, :], pb[:, g * gl:(g + 1) * gl]) for g in range(NSA_GROUPS)],
            axis=1)
        ps = jnp.sum(p, axis=0, keepdims=True)
        if first:
            l_ref[...] = ps
            acc_ref[...] = pv
        else:
            alpha = jnp.exp(m_old - m_new)
            l_ref[...] = alpha * l_ref[...] + ps
            acc_ref[...] = alpha * acc_ref[...] + pv
        m_ref[...] = m_new

    def result():
        return acc_ref[...] / l_ref[...]

    def sel_chunk(c, diag):
        k = ksa_ref[0, pl.ds(pl.multiple_of(c * SEL_CHUNK, SEL_CHUNK), SEL_CHUNK), :]
        s = _dot(k, qa_ref[...])
        if diag:
            shape = (SEL_CHUNK, nl)
            s = jnp.where(krow(shape, c * SEL_CHUNK) <= tq(shape), s, NEG)
        update(s, vst_ref[0, c], diag)

    cd = start // SEL_CHUNK
    sel_chunk(cd, True)

    def sel_body(c, carry):
        sel_chunk(c, False)
        return carry

    lax.fori_loop(0, cd, sel_body, 0)
    o_sel = result()

    def win_chunk(c, kind):
        k = kw_ref[0, pl.ds(pl.multiple_of(c * WIN_CHUNK, WIN_CHUNK), WIN_CHUNK), :]
        s = _dot(k, qk)
        shape = (WIN_CHUNK, nl)
        if kind == "causal":
            s = jnp.where(krow(shape, c * WIN_CHUNK) <= tq(shape), s, NEG)
        elif kind == "tail":
            s = jnp.where(krow(shape, c * WIN_CHUNK) > tq(shape) - WIN, s, NEG)
        update(s, vwt_ref[0, c], kind == "causal")

    n_back = WIN // WIN_CHUNK
    win_chunk(i, "causal")

    def win_body(c, carry):
        win_chunk(c, "full")
        return carry

    lax.fori_loop(jnp.maximum(i - (n_back - 1), 0), i, win_body, 0)

    @pl.when(i >= n_back)
    def _():
        win_chunk(i - n_back, "tail")

    o_win = result()

    sg = jax.nn.sigmoid(gate_ref[...].T[0:3 * NSA_HEADS, :])
    outs = []
    for h in range(NSA_HEADS):
        lanes = slice(h * QT, (h + 1) * QT)
        outs.append(sg[3 * h:3 * h + 1, :] * o_cmp[:, lanes] + sg[3 * h + 1:3 * h + 2, :] * o_sel[:, lanes]
                    + sg[3 * h + 2:3 * h + 3, :] * o_win[:, lanes])
    o_ref[...] = jnp.concatenate(outs, axis=0).T.astype(o_ref.dtype)


def _nsa_attn(znsa, gq, kc, vct, ovt, ksa, vst, kwn, vwt, b, s):
    ns = s // SLC_BLOCK
    nc = s // CMP_STRIDE
    nq = s // QT
    nl = NSA_HEADS * QT
    width = NSA_HEADS * HEAD_DIM
    kern = functools.partial(_nsa_attn_kernel, ns=ns, nc=nc, topn=min(SLC_TOPN, ns))
    return pl.pallas_call(
        kern,
        out_shape=jax.ShapeDtypeStruct((b * s, width), BF16),
        grid=(b, nq),
        in_specs=[
            pl.BlockSpec((QT, width), lambda bi, i: (bi * nq + i, 0)),
            pl.BlockSpec((QT, LANES), lambda bi, i: (bi * nq + i, 10)),
            pl.BlockSpec((HEAD_DIM, QT), lambda bi, i: (0, 0)),
            pl.BlockSpec((1, nc, LANES), lambda bi, i: (bi, 0, 0)),
            pl.BlockSpec((1, LANES, nc), lambda bi, i: (bi, 0, 0)),
            pl.BlockSpec((ns, nc), lambda bi, i: (0, 0)),
            pl.BlockSpec((1, s, 2 * LANES), lambda bi, i: (bi, 0, 0)),
            pl.BlockSpec((1, s // SEL_CHUNK, LANES, SEL_CHUNK), lambda bi, i: (bi, 0, 0, 0)),
            pl.BlockSpec((1, s, LANES), lambda bi, i: (bi, 0, 0)),
            pl.BlockSpec((1, s // WIN_CHUNK, LANES, WIN_CHUNK), lambda bi, i: (bi, 0, 0, 0)),
        ],
        out_specs=pl.BlockSpec((QT, width), lambda bi, i: (bi * nq + i, 0)),
        scratch_shapes=[
            pltpu.VMEM((2 * LANES, nl), BF16),
            pltpu.VMEM((1, nl), F32),
            pltpu.VMEM((1, nl), F32),
            pltpu.VMEM((HEAD_DIM, nl), F32),
        ],
        compiler_params=_params(2),
        name="nsa_attn",
    )(znsa, znsa, gq, kc, vct, ovt, ksa, vst, kwn, vwt)


def _erf_gelu(x):
    return 0.5 * x * (1.0 + lax.erf(x * (2.0 ** -0.5)))


def _mix_kernel(x_ref, ab_ref, ac_ref, ax_ref, ach_ref, axh_ref, p_ref, ph_ref, sgu_ref, mg0_ref, mg1_ref, mg2_ref,
                mg3_ref, ob_ref, cw_ref, pw_ref, psc_ref, sng_ref, sw_ref, sb_ref, wbr_ref, wo_ref, o_ref, ext_ref,
                *, tm, tiles_per_seq):
    ti = pl.program_id(0) % tiles_per_seq
    keep = jnp.where(ti == 0, 0.0, 1.0)

    ext_ref[0:HALO, :] = (ach_ref[...].astype(F32) * axh_ref[...].astype(F32)) * keep
    ext_ref[HALO:, :] = ac_ref[...].astype(F32) * ax_ref[...].astype(F32)
    e = ext_ref[...]
    cw = cw_ref[...]
    conv = (cw[0:1, :] * pltpu.roll(e, 2, 0) + cw[1:2, :] * pltpu.roll(e, 1, 0)) + cw[2:3, :] * e
    out_a = ab_ref[...].astype(F32) * conv[HALO:, :]

    p = p_ref[...].astype(F32)
    ext_ref[0:HALO, :] = ph_ref[...].astype(F32) * keep
    ext_ref[HALO:, :] = p
    e = ext_ref[...]
    gw = BRANCH_WIDTH // len(POOL_WINDOWS)
    cnt = (ti * tm + 1 + lax.broadcasted_iota(jnp.int32, (tm, gw), 0)).astype(F32)
    groups = []
    for gi, w in enumerate(POOL_WINDOWS):
        acc = e[:, gi * gw:(gi + 1) * gw]
        span = 1
        while span < w:
            acc = acc + pltpu.roll(acc, span, 0)
            span *= 2
        groups.append(acc[HALO:, :] / jnp.minimum(cnt, float(w)))
    pooled = jnp.concatenate(groups, axis=1) - p
    out_c = _dot(pooled.astype(BF16), pw_ref[...]) * psc_ref[...]

    z = _erf_gelu(sgu_ref[...].astype(F32))
    u = z[:, :BRANCH_WIDTH]
    v = z[:, BRANCH_WIDTH:]
    v = (v * lax.rsqrt(jnp.mean(v * v, axis=-1, keepdims=True) + RMS_EPS)) * sng_ref[...]
    vb = v.astype(BF16)
    nchunk = tm // SGU_CHUNK
    gd = BRANCH_WIDTH // SGU_GROUPS
    tri = (lax.broadcasted_iota(jnp.int32, (SGU_CHUNK, SGU_CHUNK), 0)
           >= lax.broadcasted_iota(jnp.int32, (SGU_CHUNK, SGU_CHUNK), 1))
    mixed_g = []
    for g in range(SGU_GROUPS):
        wg = jnp.where(tri, sw_ref[g], 0.0).astype(BF16)
        rhs = jnp.concatenate(
            [vb[c * SGU_CHUNK:(c + 1) * SGU_CHUNK, g * gd:(g + 1) * gd] for c in range(nchunk)], axis=1)
        res = _dot(wg, rhs)
        mixed_g.append(jnp.concatenate([res[:, c * gd:(c + 1) * gd] for c in range(nchunk)], axis=0))
    bias = jnp.concatenate([sb_ref[...]] * nchunk, axis=0)
    out_d = u * (jnp.concatenate(mixed_g, axis=1) + bias)

    merged = jax.nn.sigmoid(mg0_ref[...].astype(F32)) * _dot(out_a.astype(BF16), wbr_ref[0])
    merged = merged + jax.nn.sigmoid(mg1_ref[...].astype(F32)) * _dot(ob_ref[...], wbr_ref[1])
    merged = merged + jax.nn.sigmoid(mg2_ref[...].astype(F32)) * _dot(out_c.astype(BF16), wbr_ref[2])
    merged = merged + jax.nn.sigmoid(mg3_ref[...].astype(F32)) * _dot(out_d.astype(BF16), wbr_ref[3])
    o_ref[...] = x_ref[...] + _dot(merged.astype(BF16), wo_ref[...])


def _mix_merge(x, za, out_b, cw, pw, psc, sng, sw, sb, wbr, wo, s, tm=256):
    t, d = x.shape
    bw = BRANCH_WIDTH
    tps = s // tm
    hb = tm // HALO
    tile = lambda c: pl.BlockSpec((tm, bw), lambda i, c=c: (i, c))
    halo = lambda c: pl.BlockSpec((HALO, bw), lambda i, c=c: (jnp.maximum(i * hb - 1, 0), c))
    wide = lambda c: pl.BlockSpec((tm, d), lambda i, c=c: (i, c))
    full = lambda a: pl.BlockSpec(a.shape, lambda i, n=a.ndim: (0,) * n)
    consts = (cw, pw, psc, sng, sw, sb, wbr, wo)
    return pl.pallas_call(
        functools.partial(_mix_kernel, tm=tm, tiles_per_seq=tps),
        out_shape=jax.ShapeDtypeStruct((t, d), F32),
        grid=(t // tm,),
        in_specs=[wide(0), tile(0), tile(1), tile(2), halo(1), halo(2), tile(3), halo(3), wide(2),
                  wide(3), wide(4), wide(5), wide(6), tile(0)] + [full(a) for a in consts],
        out_specs=wide(0),
        scratch_shapes=[pltpu.VMEM((tm + HALO, bw), F32)],
        compiler_params=_params(1),
        name="mix_merge",
    )(x, za, za, za, za, za, za, za, za, za, za, za, za, out_b, *consts)


def _ffn_tail_kernel(x_ref, g_ref, gh_ref, up_ref, cw_ref, wd_ref, o_ref, ext_ref, *, tiles_per_seq):
    keep = jnp.where(pl.program_id(0) % tiles_per_seq == 0, 0.0, 1.0)
    ext_ref[0:HALO, :] = gh_ref[...].astype(F32) * keep
    ext_ref[HALO:, :] = g_ref[...].astype(F32)
    e = ext_ref[...]
    cw = cw_ref[...]
    conv = ((cw[0:1, :] * pltpu.roll(e, 2, 0) + cw[1:2, :] * pltpu.roll(e, 1, 0)) + cw[2:3, :] * e)[HALO:, :]
    act = (conv * jax.nn.sigmoid(conv)) * up_ref[...].astype(F32)
    o_ref[...] = x_ref[...] + _dot(act.astype(BF16), wd_ref[...])


def _ffn_tail(x, gu, cw, wd, s, tm=256):
    t, d = x.shape
    dff = wd.shape[0]
    tps = s // tm
    hb = tm // HALO
    return pl.pallas_call(
        functools.partial(_ffn_tail_kernel, tiles_per_seq=tps),
        out_shape=jax.ShapeDtypeStruct((t, d), F32),
        grid=(t // tm,),
        in_specs=[
            pl.BlockSpec((tm, d), lambda i: (i, 0)),
            pl.BlockSpec((tm, dff), lambda i: (i, 0)),
            pl.BlockSpec((HALO, dff), lambda i: (jnp.maximum(i * hb - 1, 0), 0)),
            pl.BlockSpec((tm, dff), lambda i: (i, 1)),
            pl.BlockSpec(cw.shape, lambda i: (0, 0)),
            pl.BlockSpec(wd.shape, lambda i: (0, 0)),
        ],
        out_specs=pl.BlockSpec((tm, d), lambda i: (i, 0)),
        scratch_shapes=[pltpu.VMEM((tm + HALO, dff), F32)],
        compiler_params=_params(1),
        name="ffn_tail",
    )(x, gu, gu, gu, cw, wd)


def _block_diag(blocks):
    n = len(blocks)
    r, c = blocks[0].shape
    out = jnp.zeros((n * r, n * c), blocks[0].dtype)
    for k, blk in enumerate(blocks):
        out = out.at[k * r:(k + 1) * r, k * c:(k + 1) * c].set(blk)
    return out


def _overlap_t(nc, ns):
    n_cmp = nc - 1
    cs = np.arange(n_cmp) * CMP_STRIDE
    ce = cs + CMP_BLOCK
    ss = np.arange(ns) * SLC_BLOCK
    se = ss + SLC_BLOCK
    ov = np.clip(np.minimum(ce[:, None], se[None]) - np.maximum(cs[:, None], ss[None]), 0, None)
    out = np.zeros((ns, nc), np.float32)
    out[:, :n_cmp] = (ov.astype(np.float32) / CMP_BLOCK).T
    return out


def _compress_params(pe, w1, w2):
    half = CMP_BLOCK // 2
    pe2 = jnp.tile(pe.reshape(2, 2, half, 1, HEAD_DIM), (1, 1, 1, NSA_GROUPS, 1)).reshape(2, 2, half * LANES)
    w1r = w1.reshape(2, 2, half, HEAD_DIM, HEAD_DIM)
    eye = jnp.eye(NSA_GROUPS, dtype=w1.dtype)
    w1bd = jnp.einsum("jpldo,gh->jplgdho", w1r, eye).reshape(2, 2, half * LANES, LANES)
    w2bd = jnp.einsum("jdo,gh->jgdho", w2, eye).reshape(2, LANES, LANES)
    return pe2, w1bd.astype(BF16), w2bd.astype(BF16)


def kernel(x, norm1_g, w_in, conv_a_w, qk_norm_g, cmp_pe, cmp_w1, cmp_w2, pool_w, pool_scale, sgu_norm_g, sgu_w,
           sgu_b, w_branch, w_o, norm2_g, w_up, conv_ff_w, w_down):
    b, s, d = x.shape
    depth = w_in.shape[0]
    t = b * s
    bw = BRANCH_WIDTH
    kvw = 6 * NSA_GROUPS * HEAD_DIM
    ngate = 3 * NSA_HEADS
    o_q = 3 * bw
    o_kv = o_q + bw
    o_gate = o_kv + kvw
    o_pool = o_gate + ngate
    dff = w_down.shape[1]
    ns = s // SLC_BLOCK
    nc = s // CMP_STRIDE
    ovt = jnp.asarray(_overlap_t(nc, ns), BF16)
    xf = x.reshape(t, d)
    tm = min(1024, t)
    for l in range(depth):
        w = w_in[l]
        w_a = jnp.concatenate([w[:, :o_q], w[:, o_pool:]], axis=1).astype(BF16)
        w_n = jnp.concatenate([w[:, o_q:o_pool], jnp.zeros((d, LANES - ngate), w.dtype)], axis=1).astype(BF16)
        za = _norm_matmul(xf, norm1_g[l], w_a, BF16, tm, 1024, "in_proj_a")
        znsa = _norm_matmul(xf, norm1_g[l], w_n, F32, tm, w_n.shape[1], "in_proj_nsa")

        g2 = lambda v: jnp.tile(v, NSA_GROUPS).reshape(1, LANES)
        ksa, vst, kwn, vwt = _kv_prep(znsa, g2(qk_norm_g[l, 2]), g2(qk_norm_g[l, 3]), b, s)
        pe2, w1bd, w2bd = _compress_params(cmp_pe[l], cmp_w1[l], cmp_w2[l])
        kc, vct = _compress(znsa, pe2, w1bd, w2bd, g2(qk_norm_g[l, 1]), b, s)
        gq = jnp.broadcast_to(qk_norm_g[l, 0][:, None], (HEAD_DIM, QT))
        out_b = _nsa_attn(znsa, gq, kc, vct, ovt, ksa, vst, kwn, vwt, b, s)

        pw = _block_diag([pool_w[l, g] for g in range(len(POOL_WINDOWS))]).astype(BF16)
        sb = jnp.repeat(sgu_b[l].T, bw // SGU_GROUPS, axis=1)
        xf = _mix_merge(xf, za, out_b, conv_a_w[l], pw, pool_scale[l].reshape(1, bw), sgu_norm_g[l].reshape(1, bw),
                        sgu_w[l], sb, w_branch[l].astype(BF16), w_o[l].astype(BF16), s)

        gu = _norm_matmul(xf, norm2_g[l], w_up[l].astype(BF16), BF16, tm, dff // 2, "ffn_up")
        xf = _ffn_tail(xf, gu, conv_ff_w[l], w_down[l].astype(BF16), s)
    return xf.reshape(b, s, d)
```

```python
import functools

import jax
import jax.numpy as jnp
import numpy as np
from jax import lax
from jax.experimental import pallas as pl
from jax.experimental.pallas import tpu as pltpu

F32 = jnp.float32
BF16 = jnp.bfloat16

HEAD_DIM = 64
NSA_HEADS = 8
NSA_GROUPS = 2
NSA_REP = NSA_HEADS // NSA_GROUPS
CMP_BLOCK = 32
CMP_STRIDE = 16
SLC_BLOCK = 64
SLC_TOPN = 16
WIN = 512
NEG = -1e30
FORCE = 1e6
RMS_EPS = 1e-6
LOG2E = 1.4426950408889634
POOL_WINDOWS = (2, 4, 8, 16)
SGU_CHUNK = 128
SGU_GROUPS = 4
BRANCH_WIDTH = 512
N_BRANCH = 4

LANES = 128
HALO = 16
QT = 128
SEL_CHUNK = 256
WIN_CHUNK = 128
VMEM_LIMIT = 56 * 1024 * 1024


def _params(n_axes):
    return pltpu.CompilerParams(dimension_semantics=("arbitrary",) * n_axes, vmem_limit_bytes=VMEM_LIMIT)


def _dot(a, b):
    return jnp.dot(a, b, preferred_element_type=F32)


def _norm_matmul_kernel(x_ref, g_ref, w_ref, o_ref, h_ref):
    @pl.when(pl.program_id(1) == 0)
    def _():
        x = x_ref[...]
        ms = jnp.mean(x * x, axis=-1, keepdims=True)
        h_ref[...] = ((x * lax.rsqrt(ms + RMS_EPS)) * g_ref[...]).astype(BF16)

    o_ref[...] = _dot(h_ref[...], w_ref[...]).astype(o_ref.dtype)


def _norm_matmul(x, g, w, out_dtype, tm, tn, name):
    t, d = x.shape
    n = w.shape[1]
    return pl.pallas_call(
        _norm_matmul_kernel,
        out_shape=jax.ShapeDtypeStruct((t, n), out_dtype),
        grid=(t // tm, n // tn),
        in_specs=[
            pl.BlockSpec((tm, d), lambda i, j: (i, 0)),
            pl.BlockSpec((1, d), lambda i, j: (0, 0)),
            pl.BlockSpec((d, tn), lambda i, j: (0, j)),
        ],
        out_specs=pl.BlockSpec((tm, tn), lambda i, j: (i, j)),
        scratch_shapes=[pltpu.VMEM((tm, d), BF16)],
        compiler_params=_params(2),
        name=name,
    )(x, g.reshape(1, d), w)


def _group_rms(x, g):
    lane = lax.broadcasted_iota(jnp.int32, x.shape, 1)
    lo = lane < HEAD_DIM
    x2 = x * x
    s_lo = jnp.sum(jnp.where(lo, x2, 0.0), axis=-1, keepdims=True)
    s_hi = jnp.sum(jnp.where(lo, 0.0, x2), axis=-1, keepdims=True)
    ms = jnp.where(lo, s_lo, s_hi) * (1.0 / HEAD_DIM)
    return (x * lax.rsqrt(ms + RMS_EPS)) * g


def _kv_prep_kernel(ks_ref, vs_ref, kw_ref, vw_ref, gs_ref, gw_ref, ksa_ref, vst_ref, kwn_ref, vwt_ref, *, ts):
    j = pl.program_id(1)
    ksn = _group_rms(ks_ref[...], gs_ref[...]).astype(BF16)
    row = lax.broadcasted_iota(jnp.int32, (ts, LANES), 0) + j * ts
    col = lax.broadcasted_iota(jnp.int32, (ts, LANES), 1)
    onehot = jnp.where(row // SLC_BLOCK == col, 1.0, 0.0).astype(BF16)
    ksa_ref[0, :, 0:LANES] = ksn
    ksa_ref[0, :, LANES:2 * LANES] = onehot
    kwn_ref[0] = _group_rms(kw_ref[...], gw_ref[...]).astype(BF16)
    vs = vs_ref[...]
    for c in range(ts // SEL_CHUNK):
        vst_ref[0, c] = vs[c * SEL_CHUNK:(c + 1) * SEL_CHUNK, :].T.astype(BF16)
    vw = vw_ref[...]
    for c in range(ts // WIN_CHUNK):
        vwt_ref[0, c] = vw[c * WIN_CHUNK:(c + 1) * WIN_CHUNK, :].T.astype(BF16)


def _kv_prep(znsa, gs, gw, b, s, ts=512):
    nt = s // ts
    col = lambda c: pl.BlockSpec((ts, LANES), lambda bi, j, c=c: (bi * nt + j, c))
    vec = pl.BlockSpec((1, LANES), lambda bi, j: (0, 0))
    return pl.pallas_call(
        functools.partial(_kv_prep_kernel, ts=ts),
        out_shape=(
            jax.ShapeDtypeStruct((b, s, 2 * LANES), BF16),
            jax.ShapeDtypeStruct((b, s // SEL_CHUNK, LANES, SEL_CHUNK), BF16),
            jax.ShapeDtypeStruct((b, s, LANES), BF16),
            jax.ShapeDtypeStruct((b, s // WIN_CHUNK, LANES, WIN_CHUNK), BF16),
        ),
        grid=(b, nt),
        in_specs=[col(6), col(7), col(8), col(9), vec, vec],
        out_specs=(
            pl.BlockSpec((1, ts, 2 * LANES), lambda bi, j: (bi, j, 0)),
            pl.BlockSpec((1, ts // SEL_CHUNK, LANES, SEL_CHUNK), lambda bi, j: (bi, j, 0, 0)),
            pl.BlockSpec((1, ts, LANES), lambda bi, j: (bi, j, 0)),
            pl.BlockSpec((1, ts // WIN_CHUNK, LANES, WIN_CHUNK), lambda bi, j: (bi, j, 0, 0)),
        ),
        compiler_params=_params(2),
        name="kv_prep",
    )(znsa, znsa, znsa, znsa, gs, gw)


def _compress_kernel(kc_ref, vc_ref, pe_ref, w1_ref, w2_ref, g_ref, kco_ref, vct_ref, u_ref, *, nc):
    half = CMP_BLOCK // 2
    for j, src in enumerate((kc_ref, vc_ref)):
        for l in range(half):
            u_ref[:, l * LANES:(l + 1) * LANES] = src[pl.ds(l, nc, stride=CMP_STRIDE), :]
        u = u_ref[...]
        top = _dot((u + pe_ref[j, 0:1, :]).astype(BF16), w1_ref[j, 0])
        bot = _dot((u + pe_ref[j, 1:2, :]).astype(BF16), w1_ref[j, 1])
        pre = top + pltpu.roll(bot, nc - 1, 0)
        hid = pre * jax.nn.sigmoid(pre)
        out = _dot(hid.astype(BF16), w2_ref[j])
        if j == 0:
            kco_ref[0] = _group_rms(out, g_ref[...]).astype(BF16)
        else:
            vct_ref[0] = out.T.astype(BF16)


def _compress(znsa, pe2, w1bd, w2bd, g1, b, s):
    nc = s // CMP_STRIDE
    kw = (CMP_BLOCK // 2) * LANES
    return pl.pallas_call(
        functools.partial(_compress_kernel, nc=nc),
        out_shape=(
            jax.ShapeDtypeStruct((b, nc, LANES), BF16),
            jax.ShapeDtypeStruct((b, LANES, nc), BF16),
        ),
        grid=(b,),
        in_specs=[
            pl.BlockSpec((s, LANES), lambda bi: (bi, 4)),
            pl.BlockSpec((s, LANES), lambda bi: (bi, 5)),
            pl.BlockSpec((2, 2, kw), lambda bi: (0, 0, 0)),
            pl.BlockSpec((2, 2, kw, LANES), lambda bi: (0, 0, 0, 0)),
            pl.BlockSpec((2, LANES, LANES), lambda bi: (0, 0, 0)),
            pl.BlockSpec((1, LANES), lambda bi: (0, 0)),
        ],
        out_specs=(
            pl.BlockSpec((1, nc, LANES), lambda bi: (bi, 0, 0)),
            pl.BlockSpec((1, LANES, nc), lambda bi: (bi, 0, 0)),
        ),
        scratch_shapes=[pltpu.VMEM((nc, kw), F32)],
        compiler_params=_params(1),
        name="compress",
    )(znsa, znsa, pe2, w1bd, w2bd, g1)


def _rank_desc(imp, ns):
    nq = imp.shape[1]
    sub = lax.broadcasted_iota(jnp.int32, (8, nq), 0)
    blocks = [imp[8 * a:8 * a + 8, :] for a in range(ns // 8)]
    rank = [jnp.zeros((8, nq), F32) for _ in blocks]
    for k in range(ns):
        ka, kr = divmod(k, 8)
        rk = jnp.broadcast_to(imp[k:k + 1, :], (8, nq))
        for a, blk in enumerate(blocks):
            if a > ka:
                ahead = rk >= blk
            elif a < ka:
                ahead = rk > blk
            else:
                rank[a] = rank[a] + jnp.where(sub > kr, jnp.where(rk >= blk, 1.0, 0.0), jnp.where(rk > blk, 1.0, 0.0))
                continue
            rank[a] = rank[a] + jnp.where(ahead, 1.0, 0.0)
    return jnp.concatenate(rank, axis=0)


def _nsa_attn_kernel(q_ref, gate_ref, gq_ref, kc_ref, vct_ref, ovt_ref, ksa_ref, vst_ref, kw_ref, vwt_ref,
                     o_ref, qa_ref, sa_ref, sb_ref, m_ref, l_ref, acc_ref, ocmp_ref, owin_ref, *, ns, nc, topn):
    i = pl.program_id(1)
    start = i * QT
    nl = NSA_HEADS * QT
    gl = NSA_REP * QT

    qt = q_ref[...].T.reshape(NSA_HEADS, HEAD_DIM, QT)
    ms = jnp.mean(qt * qt, axis=1, keepdims=True)
    qn = ((qt * lax.rsqrt(ms + RMS_EPS)) * gq_ref[...][None]) * (HEAD_DIM ** -0.5 * LOG2E)
    qa_ref[...] = jnp.zeros(qa_ref.shape, BF16)
    for h in range(NSA_HEADS):
        g = h // NSA_REP
        qa_ref[g * HEAD_DIM:(g + 1) * HEAD_DIM, h * QT:(h + 1) * QT] = qn[h].astype(BF16)

    def tq(shape):
        return start + (lax.broadcasted_iota(jnp.int32, shape, 1) & (QT - 1))

    def krow(shape, base):
        return base + lax.broadcasted_iota(jnp.int32, shape, 0)

    def group_pv(vt, pb):
        return jnp.concatenate(
            [_dot(vt[g * HEAD_DIM:(g + 1) * HEAD_DIM, :], pb[:, g * gl:(g + 1) * gl]) for g in range(NSA_GROUPS)],
            axis=1)

    def compressed_and_select():
        qk = qa_ref[0:LANES, :]
        sc = _dot(kc_ref[0], qk)
        cval = krow((nc, nl), 0) * CMP_STRIDE + (CMP_BLOCK - 1) <= tq((nc, nl))
        sc = jnp.where(cval, sc, NEG)
        e = jnp.exp2(sc - jnp.max(sc, axis=0, keepdims=True))
        pc = e / jnp.sum(e, axis=0, keepdims=True)
        pc = jnp.where(tq((1, nl)) >= CMP_BLOCK - 1, pc, 0.0)
        ocmp_ref[...] = group_pv(vct_ref[0], pc.astype(BF16))

        jb = krow((ns, QT), 0)
        jt = (start + lax.broadcasted_iota(jnp.int32, (ns, QT), 1)) // SLC_BLOCK
        forced = (jb == 0) | (jb == jt) | (jb == jt - 1)
        future = jb > jt
        ovt = ovt_ref[...]
        for g in range(NSA_GROUPS):
            psum = pc[:, g * gl:g * gl + QT]
            for r in range(1, NSA_REP):
                psum = psum + pc[:, g * gl + r * QT:g * gl + (r + 1) * QT]
            p_hi = psum.astype(BF16)
            rem = psum - p_hi.astype(F32)
            p_mid = rem.astype(BF16)
            p_lo = (rem - p_mid.astype(F32)).astype(BF16)
            imp = (_dot(ovt, p_hi) + _dot(ovt, p_mid)) + _dot(ovt, p_lo)
            imp = jnp.where(forced, FORCE, jnp.where(future, NEG, imp))
            sel = (_rank_desc(imp, ns) < topn) & jnp.logical_not(future)
            bias = jnp.where(sel, 0.0, NEG).astype(BF16)
            for r in range(NSA_REP):
                h = g * NSA_REP + r
                qa_ref[LANES:LANES + ns, h * QT:(h + 1) * QT] = bias

    def update(s, vt, first):
        cm = jnp.max(s, axis=0, keepdims=True)
        if first:
            m_new = cm
        else:
            m_old = m_ref[...]
            m_new = jnp.maximum(m_old, cm)
        p = jnp.exp2(s - m_new)
        pv = group_pv(vt, p.astype(BF16))
        ps = jnp.sum(p, axis=0, keepdims=True)
        if first:
            l_ref[...] = ps
            acc_ref[...] = pv
        else:
            alpha = jnp.exp2(m_old - m_new)
            l_ref[...] = alpha * l_ref[...] + ps
            acc_ref[...] = alpha * acc_ref[...] + pv
        m_ref[...] = m_new

    def result():
        return acc_ref[...] / l_ref[...]

    n_back = WIN // WIN_CHUNK
    wshape = (WIN_CHUNK, nl)
    local_q = lax.broadcasted_iota(jnp.int32, wshape, 1) & (QT - 1)
    local_k = lax.broadcasted_iota(jnp.int32, wshape, 0)

    def win_scores(c):
        k = kw_ref[0, pl.ds(pl.multiple_of(c * WIN_CHUNK, WIN_CHUNK), WIN_CHUNK), :]
        return _dot(k, qa_ref[0:LANES, :])

    def window_full():
        ss = [win_scores(i - n_back + c) for c in range(n_back + 1)]
        ss[0] = jnp.where(local_k > local_q, ss[0], NEG)
        ss[n_back] = jnp.where(local_k <= local_q, ss[n_back], NEG)
        m = jnp.max(ss[0], axis=0, keepdims=True)
        for s in ss[1:]:
            m = jnp.maximum(m, jnp.max(s, axis=0, keepdims=True))
        ps = [jnp.exp2(s - m) for s in ss]
        den = jnp.sum(ps[0], axis=0, keepdims=True)
        for p in ps[1:]:
            den = den + jnp.sum(p, axis=0, keepdims=True)
        pb = jnp.concatenate([p.astype(BF16) for p in ps], axis=0)
        vt = jnp.concatenate([vwt_ref[0, i - n_back + c] for c in range(n_back + 1)], axis=1)
        owin_ref[...] = group_pv(vt, pb) / den

    def window_head():
        s = jnp.where(local_k <= local_q, win_scores(i), NEG)
        update(s, vwt_ref[0, i], True)

        def body(c, carry):
            update(win_scores(c), vwt_ref[0, c], False)
            return carry

        lax.fori_loop(0, i, body, 0)
        owin_ref[...] = result()

    @pl.when(i >= n_back)
    def _():
        compressed_and_select()
        window_full()

    @pl.when(i < n_back)
    def _():
        compressed_and_select()
        window_head()

    def sel_scores(buf, c):
        k = ksa_ref[0, pl.ds(pl.multiple_of(c * SEL_CHUNK, SEL_CHUNK), SEL_CHUNK), :]
        buf[...] = _dot(k, qa_ref[...])

    def sel_update(buf, c, diag):
        s = buf[...]
        if diag:
            shape = (SEL_CHUNK, nl)
            s = jnp.where(krow(shape, c * SEL_CHUNK) <= tq(shape), s, NEG)
        update(s, vst_ref[0, c], diag)

    cd = start // SEL_CHUNK
    sel_scores(sa_ref, cd)
    sel_scores(sb_ref, 0)
    sel_update(sa_ref, cd, True)

    def sel_pair(p, carry):
        c0 = 2 * p
        sel_scores(sa_ref, c0 + 1)
        sel_update(sb_ref, c0, False)
        sel_scores(sb_ref, jnp.minimum(c0 + 2, cd - 1))
        sel_update(sa_ref, c0 + 1, False)
        return carry

    lax.fori_loop(0, cd // 2, sel_pair, 0)

    @pl.when(cd % 2 == 1)
    def _():
        sel_update(sb_ref, cd - 1, False)

    o_sel = result()
    o_cmp = ocmp_ref[...]
    o_win = owin_ref[...]

    sg = jax.nn.sigmoid(gate_ref[...].T[0:3 * NSA_HEADS, :])
    outs = []
    for h in range(NSA_HEADS):
        lanes = slice(h * QT, (h + 1) * QT)
        outs.append(sg[3 * h:3 * h + 1, :] * o_cmp[:, lanes] + sg[3 * h + 1:3 * h + 2, :] * o_sel[:, lanes]
                    + sg[3 * h + 2:3 * h + 3, :] * o_win[:, lanes])
    o_ref[...] = jnp.concatenate(outs, axis=0).T.astype(o_ref.dtype)


def _nsa_attn(znsa, gq, kc, vct, ovt, ksa, vst, kwn, vwt, b, s):
    ns = s // SLC_BLOCK
    nc = s // CMP_STRIDE
    nq = s // QT
    nl = NSA_HEADS * QT
    width = NSA_HEADS * HEAD_DIM
    kern = functools.partial(_nsa_attn_kernel, ns=ns, nc=nc, topn=min(SLC_TOPN, ns))
    return pl.pallas_call(
        kern,
        out_shape=jax.ShapeDtypeStruct((b * s, width), BF16),
        grid=(b, nq),
        in_specs=[
            pl.BlockSpec((QT, width), lambda bi, i: (bi * nq + i, 0)),
            pl.BlockSpec((QT, LANES), lambda bi, i: (bi * nq + i, 10)),
            pl.BlockSpec((HEAD_DIM, QT), lambda bi, i: (0, 0)),
            pl.BlockSpec((1, nc, LANES), lambda bi, i: (bi, 0, 0)),
            pl.BlockSpec((1, LANES, nc), lambda bi, i: (bi, 0, 0)),
            pl.BlockSpec((ns, nc), lambda bi, i: (0, 0)),
            pl.BlockSpec((1, s, 2 * LANES), lambda bi, i: (bi, 0, 0)),
            pl.BlockSpec((1, s // SEL_CHUNK, LANES, SEL_CHUNK), lambda bi, i: (bi, 0, 0, 0)),
            pl.BlockSpec((1, s, LANES), lambda bi, i: (bi, 0, 0)),
            pl.BlockSpec((1, s // WIN_CHUNK, LANES, WIN_CHUNK), lambda bi, i: (bi, 0, 0, 0)),
        ],
        out_specs=pl.BlockSpec((QT, width), lambda bi, i: (bi * nq + i, 0)),
        scratch_shapes=[
            pltpu.VMEM((2 * LANES, nl), BF16),
            pltpu.VMEM((SEL_CHUNK, nl), F32),
            pltpu.VMEM((SEL_CHUNK, nl), F32),
            pltpu.VMEM((1, nl), F32),
            pltpu.VMEM((1, nl), F32),
            pltpu.VMEM((HEAD_DIM, nl), F32),
            pltpu.VMEM((HEAD_DIM, nl), F32),
            pltpu.VMEM((HEAD_DIM, nl), F32),
        ],
        compiler_params=_params(2),
        name="nsa_attn",
    )(znsa, znsa, gq, kc, vct, ovt, ksa, vst, kwn, vwt)


def _erf_gelu(x):
    return 0.5 * x * (1.0 + lax.erf(x * (2.0 ** -0.5)))


def _mix_kernel(x_ref, ab_ref, ac_ref, ax_ref, ach_ref, axh_ref, p_ref, ph_ref, sgu_ref, mg0_ref, mg1_ref, mg2_ref,
                mg3_ref, ob_ref, cw_ref, pw_ref, psc_ref, sng_ref, sw_ref, sb_ref, wbr_ref, wo_ref, o_ref, ext_ref,
                *, tm, tiles_per_seq):
    ti = pl.program_id(0) % tiles_per_seq
    keep = jnp.where(ti == 0, 0.0, 1.0)

    ext_ref[0:HALO, :] = (ach_ref[...].astype(F32) * axh_ref[...].astype(F32)) * keep
    ext_ref[HALO:, :] = ac_ref[...].astype(F32) * ax_ref[...].astype(F32)
    e = ext_ref[...]
    cw = cw_ref[...]
    conv = (cw[0:1, :] * pltpu.roll(e, 2, 0) + cw[1:2, :] * pltpu.roll(e, 1, 0)) + cw[2:3, :] * e
    out_a = ab_ref[...].astype(F32) * conv[HALO:, :]

    p = p_ref[...].astype(F32)
    ext_ref[0:HALO, :] = ph_ref[...].astype(F32) * keep
    ext_ref[HALO:, :] = p
    e = ext_ref[...]
    gw = BRANCH_WIDTH // len(POOL_WINDOWS)
    cnt = (ti * tm + 1 + lax.broadcasted_iota(jnp.int32, (tm, gw), 0)).astype(F32)
    groups = []
    for gi, w in enumerate(POOL_WINDOWS):
        acc = e[:, gi * gw:(gi + 1) * gw]
        span = 1
        while span < w:
            acc = acc + pltpu.roll(acc, span, 0)
            span *= 2
        groups.append(acc[HALO:, :] / jnp.minimum(cnt, float(w)))
    pooled = jnp.concatenate(groups, axis=1) - p
    out_c = _dot(pooled.astype(BF16), pw_ref[...]) * psc_ref[...]

    z = _erf_gelu(sgu_ref[...].astype(F32))
    u = z[:, :BRANCH_WIDTH]
    v = z[:, BRANCH_WIDTH:]
    v = (v * lax.rsqrt(jnp.mean(v * v, axis=-1, keepdims=True) + RMS_EPS)) * sng_ref[...]
    vb = v.astype(BF16)
    nchunk = tm // SGU_CHUNK
    gd = BRANCH_WIDTH // SGU_GROUPS
    tri = (lax.broadcasted_iota(jnp.int32, (SGU_CHUNK, SGU_CHUNK), 0)
           >= lax.broadcasted_iota(jnp.int32, (SGU_CHUNK, SGU_CHUNK), 1))
    mixed_g = []
    for g in range(SGU_GROUPS):
        wg = jnp.where(tri, sw_ref[g], 0.0).astype(BF16)
        rhs = jnp.concatenate(
            [vb[c * SGU_CHUNK:(c + 1) * SGU_CHUNK, g * gd:(g + 1) * gd] for c in range(nchunk)], axis=1)
        res = _dot(wg, rhs)
        mixed_g.append(jnp.concatenate([res[:, c * gd:(c + 1) * gd] for c in range(nchunk)], axis=0))
    bias = jnp.concatenate([sb_ref[...]] * nchunk, axis=0)
    out_d = u * (jnp.concatenate(mixed_g, axis=1) + bias)

    merged = jax.nn.sigmoid(mg0_ref[...].astype(F32)) * _dot(out_a.astype(BF16), wbr_ref[0])
    merged = merged + jax.nn.sigmoid(mg1_ref[...].astype(F32)) * _dot(ob_ref[...], wbr_ref[1])
    merged = merged + jax.nn.sigmoid(mg2_ref[...].astype(F32)) * _dot(out_c.astype(BF16), wbr_ref[2])
    merged = merged + jax.nn.sigmoid(mg3_ref[...].astype(F32)) * _dot(out_d.astype(BF16), wbr_ref[3])
    o_ref[...] = x_ref[...] + _dot(merged.astype(BF16), wo_ref[...])


def _mix_merge(x, za, out_b, cw, pw, psc, sng, sw, sb, wbr, wo, s, tm=256):
    t, d = x.shape
    bw = BRANCH_WIDTH
    tps = s // tm
    hb = tm // HALO
    tile = lambda c: pl.BlockSpec((tm, bw), lambda i, c=c: (i, c))
    halo = lambda c: pl.BlockSpec((HALO, bw), lambda i, c=c: (jnp.maximum(i * hb - 1, 0), c))
    wide = lambda c: pl.BlockSpec((tm, d), lambda i, c=c: (i, c))
    full = lambda a: pl.BlockSpec(a.shape, lambda i, n=a.ndim: (0,) * n)
    consts = (cw, pw, psc, sng, sw, sb, wbr, wo)
    return pl.pallas_call(
        functools.partial(_mix_kernel, tm=tm, tiles_per_seq=tps),
        out_shape=jax.ShapeDtypeStruct((t, d), F32),
        grid=(t // tm,),
        in_specs=[wide(0), tile(0), tile(1), tile(2), halo(1), halo(2), tile(3), halo(3), wide(2),
                  wide(3), wide(4), wide(5), wide(6), tile(0)] + [full(a) for a in consts],
        out_specs=wide(0),
        scratch_shapes=[pltpu.VMEM((tm + HALO, bw), F32)],
        compiler_params=_params(1),
        name="mix_merge",
    )(x, za, za, za, za, za, za, za, za, za, za, za, za, out_b, *consts)


def _ffn_tail_kernel(x_ref, g_ref, gh_ref, up_ref, cw_ref, wd_ref, o_ref, ext_ref, *, tiles_per_seq):
    keep = jnp.where(pl.program_id(0) % tiles_per_seq == 0, 0.0, 1.0)
    ext_ref[0:HALO, :] = gh_ref[...].astype(F32) * keep
    ext_ref[HALO:, :] = g_ref[...].astype(F32)
    e = ext_ref[...]
    cw = cw_ref[...]
    conv = ((cw[0:1, :] * pltpu.roll(e, 2, 0) + cw[1:2, :] * pltpu.roll(e, 1, 0)) + cw[2:3, :] * e)[HALO:, :]
    act = (conv * jax.nn.sigmoid(conv)) * up_ref[...].astype(F32)
    o_ref[...] = x_ref[...] + _dot(act.astype(BF16), wd_ref[...])


def _ffn_tail(x, gu, cw, wd, s, tm=256):
    t, d = x.shape
    dff = wd.shape[0]
    tps = s // tm
    hb = tm // HALO
    return pl.pallas_call(
        functools.partial(_ffn_tail_kernel, tiles_per_seq=tps),
        out_shape=jax.ShapeDtypeStruct((t, d), F32),
        grid=(t // tm,),
        in_specs=[
            pl.BlockSpec((tm, d), lambda i: (i, 0)),
            pl.BlockSpec((tm, dff), lambda i: (i, 0)),
            pl.BlockSpec((HALO, dff), lambda i: (jnp.maximum(i * hb - 1, 0), 0)),
            pl.BlockSpec((tm, dff), lambda i: (i, 1)),
            pl.BlockSpec(cw.shape, lambda i: (0, 0)),
            pl.BlockSpec(wd.shape, lambda i: (0, 0)),
        ],
        out_specs=pl.BlockSpec((tm, d), lambda i: (i, 0)),
        scratch_shapes=[pltpu.VMEM((tm + HALO, dff), F32)],
        compiler_params=_params(1),
        name="ffn_tail",
    )(x, gu, gu, gu, cw, wd)


def _block_diag(blocks):
    n = len(blocks)
    r, c = blocks[0].shape
    out = jnp.zeros((n * r, n * c), blocks[0].dtype)
    for k, blk in enumerate(blocks):
        out = out.at[k * r:(k + 1) * r, k * c:(k + 1) * c].set(blk)
    return out


def _overlap_t(nc, ns):
    n_cmp = nc - 1
    cs = np.arange(n_cmp) * CMP_STRIDE
    ce = cs + CMP_BLOCK
    ss = np.arange(ns) * SLC_BLOCK
    se = ss + SLC_BLOCK
    ov = np.clip(np.minimum(ce[:, None], se[None]) - np.maximum(cs[:, None], ss[None]), 0, None)
    out = np.zeros((ns, nc), np.float32)
    out[:, :n_cmp] = (ov.astype(np.float32) / CMP_BLOCK).T
    return out


def _compress_params(pe, w1, w2):
    half = CMP_BLOCK // 2
    pe2 = jnp.tile(pe.reshape(2, 2, half, 1, HEAD_DIM), (1, 1, 1, NSA_GROUPS, 1)).reshape(2, 2, half * LANES)
    w1r = w1.reshape(2, 2, half, HEAD_DIM, HEAD_DIM)
    eye = jnp.eye(NSA_GROUPS, dtype=w1.dtype)
    w1bd = jnp.einsum("jpldo,gh->jplgdho", w1r, eye).reshape(2, 2, half * LANES, LANES)
    w2bd = jnp.einsum("jdo,gh->jgdho", w2, eye).reshape(2, LANES, LANES)
    return pe2, w1bd.astype(BF16), w2bd.astype(BF16)


def kernel(x, norm1_g, w_in, conv_a_w, qk_norm_g, cmp_pe, cmp_w1, cmp_w2, pool_w, pool_scale, sgu_norm_g, sgu_w,
           sgu_b, w_branch, w_o, norm2_g, w_up, conv_ff_w, w_down):
    b, s, d = x.shape
    depth = w_in.shape[0]
    t = b * s
    bw = BRANCH_WIDTH
    kvw = 6 * NSA_GROUPS * HEAD_DIM
    ngate = 3 * NSA_HEADS
    o_q = 3 * bw
    o_kv = o_q + bw
    o_gate = o_kv + kvw
    o_pool = o_gate + ngate
    dff = w_down.shape[1]
    ns = s // SLC_BLOCK
    nc = s // CMP_STRIDE
    ovt = jnp.asarray(_overlap_t(nc, ns), BF16)
    xf = x.reshape(t, d)
    tm = min(1024, t)
    tm_wide = min(2048, t)
    for l in range(depth):
        w = w_in[l]
        w_a = jnp.concatenate([w[:, :o_q], w[:, o_pool:]], axis=1).astype(BF16)
        w_n = jnp.concatenate([w[:, o_q:o_pool], jnp.zeros((d, LANES - ngate), w.dtype)], axis=1).astype(BF16)
        za = _norm_matmul(xf, norm1_g[l], w_a, BF16, tm_wide, 1024, "in_proj_a")
        znsa = _norm_matmul(xf, norm1_g[l], w_n, F32, tm, w_n.shape[1], "in_proj_nsa")

        g2 = lambda v: jnp.tile(v, NSA_GROUPS).reshape(1, LANES)
        ksa, vst, kwn, vwt = _kv_prep(znsa, g2(qk_norm_g[l, 2]), g2(qk_norm_g[l, 3]), b, s)
        pe2, w1bd, w2bd = _compress_params(cmp_pe[l], cmp_w1[l], cmp_w2[l])
        kc, vct = _compress(znsa, pe2, w1bd, w2bd, g2(qk_norm_g[l, 1]), b, s)
        gq = jnp.broadcast_to(qk_norm_g[l, 0][:, None], (HEAD_DIM, QT))
        out_b = _nsa_attn(znsa, gq, kc, vct, ovt, ksa, vst, kwn, vwt, b, s)

        pw = _block_diag([pool_w[l, g] for g in range(len(POOL_WINDOWS))]).astype(BF16)
        sb = jnp.repeat(sgu_b[l].T, bw // SGU_GROUPS, axis=1)
        xf = _mix_merge(xf, za, out_b, conv_a_w[l], pw, pool_scale[l].reshape(1, bw), sgu_norm_g[l].reshape(1, bw),
                        sgu_w[l], sb, w_branch[l].astype(BF16), w_o[l].astype(BF16), s)

        gu = _norm_matmul(xf, norm2_g[l], w_up[l].astype(BF16), BF16, tm_wide, dff // 2, "ffn_up")
        xf = _ffn_tail(xf, gu, conv_ff_w[l], w_down[l].astype(BF16), s)
    return xf.reshape(b, s, d)
```

```python
import functools

import jax
import jax.numpy as jnp
import numpy as np
from jax import lax
from jax.experimental import pallas as pl
from jax.experimental.pallas import tpu as pltpu

F32 = jnp.float32
BF16 = jnp.bfloat16

HEAD_DIM = 64
NSA_HEADS = 8
NSA_GROUPS = 2
NSA_REP = NSA_HEADS // NSA_GROUPS
CMP_BLOCK = 32
CMP_STRIDE = 16
SLC_BLOCK = 64
SLC_TOPN = 16
WIN = 512
NEG = -1e30
FORCE = 1e6
RMS_EPS = 1e-6
LOG2E = 1.4426950408889634
POOL_WINDOWS = (2, 4, 8, 16)
SGU_CHUNK = 128
SGU_GROUPS = 4
BRANCH_WIDTH = 512
N_BRANCH = 4

LANES = 128
HALO = 16
QT = 128
SEL_CHUNK = 256
WIN_CHUNK = 128
FF_CHUNK = 256
RANK_STEP = 16
VMEM_LIMIT = 56 * 1024 * 1024


def _params(n_axes):
    return pltpu.CompilerParams(dimension_semantics=("arbitrary",) * n_axes, vmem_limit_bytes=VMEM_LIMIT)


def _dot(a, b):
    return jnp.dot(a, b, preferred_element_type=F32)


def _rms_bf16(x, g):
    ms = jnp.mean(x * x, axis=-1, keepdims=True)
    return ((x * lax.rsqrt(ms + RMS_EPS)) * g).astype(BF16)


def _erf_gelu(x):
    return 0.5 * x * (1.0 + lax.erf(x * (2.0 ** -0.5)))


_EPILOGUES = {"plain": lambda z: z, "gelu": _erf_gelu, "sigmoid": jax.nn.sigmoid}


def _norm_matmul_kernel(x_ref, g_ref, w_ref, o_ref, h_ref, *, epilogues, rb):
    j = pl.program_id(1)

    @pl.when(j == 0)
    def _():
        h_ref[...] = _rms_bf16(x_ref[...], g_ref[...])

    for lo, hi, kind in epilogues:
        @pl.when((j >= lo) & (j < hi))
        def _(kind=kind):
            for r0 in range(0, h_ref.shape[0], rb):
                z = _dot(h_ref[r0:r0 + rb, :], w_ref[...])
                o_ref[r0:r0 + rb, :] = _EPILOGUES[kind](z).astype(o_ref.dtype)


def _norm_matmul(x, g, w, out_dtype, tm, tn, name, epilogues=None):
    t, d = x.shape
    n = w.shape[1]
    epilogues = epilogues or ((0, n // tn, "plain"),)
    rb = tm if all(kind == "plain" for _, _, kind in epilogues) else min(tm, 512)
    return pl.pallas_call(
        functools.partial(_norm_matmul_kernel, epilogues=epilogues, rb=rb),
        out_shape=jax.ShapeDtypeStruct((t, n), out_dtype),
        grid=(t // tm, n // tn),
        in_specs=[
            pl.BlockSpec((tm, d), lambda i, j: (i, 0)),
            pl.BlockSpec((1, d), lambda i, j: (0, 0)),
            pl.BlockSpec((d, tn), lambda i, j: (0, j)),
        ],
        out_specs=pl.BlockSpec((tm, tn), lambda i, j: (i, j)),
        scratch_shapes=[pltpu.VMEM((tm, d), BF16)],
        compiler_params=_params(2),
        name=name,
    )(x, g.reshape(1, d), w)


def _group_rms(x, g):
    lane = lax.broadcasted_iota(jnp.int32, x.shape, 1)
    lo = lane < HEAD_DIM
    x2 = x * x
    s_lo = jnp.sum(jnp.where(lo, x2, 0.0), axis=-1, keepdims=True)
    s_hi = jnp.sum(jnp.where(lo, 0.0, x2), axis=-1, keepdims=True)
    ms = jnp.where(lo, s_lo, s_hi) * (1.0 / HEAD_DIM)
    return (x * lax.rsqrt(ms + RMS_EPS)) * g


def _kv_prep_kernel(ks_ref, vs_ref, kw_ref, vw_ref, gs_ref, gw_ref, ksa_ref, vst_ref, kwn_ref, vwt_ref, *, ts):
    j = pl.program_id(1)
    ksn = _group_rms(ks_ref[...], gs_ref[...]).astype(BF16)
    row = lax.broadcasted_iota(jnp.int32, (ts, LANES), 0) + j * ts
    col = lax.broadcasted_iota(jnp.int32, (ts, LANES), 1)
    onehot = jnp.where(row // SLC_BLOCK == col, 1.0, 0.0).astype(BF16)
    ksa_ref[0, :, 0:LANES] = ksn
    ksa_ref[0, :, LANES:2 * LANES] = onehot
    kwn_ref[0] = _group_rms(kw_ref[...], gw_ref[...]).astype(BF16)
    vs = vs_ref[...]
    for c in range(ts // SEL_CHUNK):
        vst_ref[0, c] = vs[c * SEL_CHUNK:(c + 1) * SEL_CHUNK, :].T.astype(BF16)
    vw = vw_ref[...]
    for c in range(ts // WIN_CHUNK):
        vwt_ref[0, c] = vw[c * WIN_CHUNK:(c + 1) * WIN_CHUNK, :].T.astype(BF16)


def _kv_prep(znsa, gs, gw, b, s, ts=512):
    nt = s // ts
    col = lambda c: pl.BlockSpec((ts, LANES), lambda bi, j, c=c: (bi * nt + j, c))
    vec = pl.BlockSpec((1, LANES), lambda bi, j: (0, 0))
    return pl.pallas_call(
        functools.partial(_kv_prep_kernel, ts=ts),
        out_shape=(
            jax.ShapeDtypeStruct((b, s, 2 * LANES), BF16),
            jax.ShapeDtypeStruct((b, s // SEL_CHUNK, LANES, SEL_CHUNK), BF16),
            jax.ShapeDtypeStruct((b, s, LANES), BF16),
            jax.ShapeDtypeStruct((b, s // WIN_CHUNK, LANES, WIN_CHUNK), BF16),
        ),
        grid=(b, nt),
        in_specs=[col(6), col(7), col(8), col(9), vec, vec],
        out_specs=(
            pl.BlockSpec((1, ts, 2 * LANES), lambda bi, j: (bi, j, 0)),
            pl.BlockSpec((1, ts // SEL_CHUNK, LANES, SEL_CHUNK), lambda bi, j: (bi, j, 0, 0)),
            pl.BlockSpec((1, ts, LANES), lambda bi, j: (bi, j, 0)),
            pl.BlockSpec((1, ts // WIN_CHUNK, LANES, WIN_CHUNK), lambda bi, j: (bi, j, 0, 0)),
        ),
        compiler_params=_params(2),
        name="kv_prep",
    )(znsa, znsa, znsa, znsa, gs, gw)


def _compress_kernel(kc_ref, vc_ref, pe_ref, w1_ref, w2_ref, g_ref, kco_ref, vct_ref, u_ref, *, nc):
    half = CMP_BLOCK // 2
    for j, src in enumerate((kc_ref, vc_ref)):
        for l in range(half):
            u_ref[:, l * LANES:(l + 1) * LANES] = src[pl.ds(l, nc, stride=CMP_STRIDE), :]
        u = u_ref[...]
        top = _dot((u + pe_ref[j, 0:1, :]).astype(BF16), w1_ref[j, 0])
        bot = _dot((u + pe_ref[j, 1:2, :]).astype(BF16), w1_ref[j, 1])
        pre = top + pltpu.roll(bot, nc - 1, 0)
        hid = pre * jax.nn.sigmoid(pre)
        out = _dot(hid.astype(BF16), w2_ref[j])
        if j == 0:
            kco_ref[0] = _group_rms(out, g_ref[...]).astype(BF16)
        else:
            vct_ref[0] = out.T.astype(BF16)


def _compress(znsa, pe2, w1bd, w2bd, g1, b, s):
    nc = s // CMP_STRIDE
    kw = (CMP_BLOCK // 2) * LANES
    return pl.pallas_call(
        functools.partial(_compress_kernel, nc=nc),
        out_shape=(
            jax.ShapeDtypeStruct((b, nc, LANES), BF16),
            jax.ShapeDtypeStruct((b, LANES, nc), BF16),
        ),
        grid=(b,),
        in_specs=[
            pl.BlockSpec((s, LANES), lambda bi: (bi, 4)),
            pl.BlockSpec((s, LANES), lambda bi: (bi, 5)),
            pl.BlockSpec((2, 2, kw), lambda bi: (0, 0, 0)),
            pl.BlockSpec((2, 2, kw, LANES), lambda bi: (0, 0, 0, 0)),
            pl.BlockSpec((2, LANES, LANES), lambda bi: (0, 0, 0)),
            pl.BlockSpec((1, LANES), lambda bi: (0, 0)),
        ],
        out_specs=(
            pl.BlockSpec((1, nc, LANES), lambda bi: (bi, 0, 0)),
            pl.BlockSpec((1, LANES, nc), lambda bi: (bi, 0, 0)),
        ),
        scratch_shapes=[pltpu.VMEM((nc, kw), F32)],
        compiler_params=_params(1),
        name="compress",
    )(znsa, znsa, pe2, w1bd, w2bd, g1)


def _rank_desc(imp, ns):
    nq = imp.shape[1]
    sub = lax.broadcasted_iota(jnp.int32, (8, nq), 0)
    blocks = [imp[8 * a:8 * a + 8, :] for a in range(ns // 8)]
    rank = [jnp.zeros((8, nq), F32) for _ in blocks]
    for k in range(ns):
        ka, kr = divmod(k, 8)
        rk = jnp.broadcast_to(imp[k:k + 1, :], (8, nq))
        for a, blk in enumerate(blocks):
            if a > ka:
                ahead = rk >= blk
            elif a < ka:
                ahead = rk > blk
            else:
                rank[a] = rank[a] + jnp.where(sub > kr, jnp.where(rk >= blk, 1.0, 0.0), jnp.where(rk > blk, 1.0, 0.0))
                continue
            rank[a] = rank[a] + jnp.where(ahead, 1.0, 0.0)
    return jnp.concatenate(rank, axis=0)


def _nsa_attn_kernel(q_ref, gate_ref, gq_ref, kc_ref, vct_ref, ovt_ref, ksa_ref, vst_ref, kw_ref, vwt_ref,
                     o_ref, qa_ref, sa_ref, sb_ref, m_ref, l_ref, acc_ref, ocmp_ref, owin_ref, *, ns, nc, topn):
    i = pl.program_id(1)
    start = i * QT
    nl = NSA_HEADS * QT
    gl = NSA_REP * QT

    qt = q_ref[...].T.reshape(NSA_HEADS, HEAD_DIM, QT)
    ms = jnp.mean(qt * qt, axis=1, keepdims=True)
    qn = ((qt * lax.rsqrt(ms + RMS_EPS)) * gq_ref[...][None]) * (HEAD_DIM ** -0.5 * LOG2E)
    qa_ref[...] = jnp.zeros(qa_ref.shape, BF16)
    for h in range(NSA_HEADS):
        g = h // NSA_REP
        qa_ref[g * HEAD_DIM:(g + 1) * HEAD_DIM, h * QT:(h + 1) * QT] = qn[h].astype(BF16)

    def tq(shape):
        return start + (lax.broadcasted_iota(jnp.int32, shape, 1) & (QT - 1))

    def krow(shape, base):
        return base + lax.broadcasted_iota(jnp.int32, shape, 0)

    def group_pv(vt, pb):
        rows = vt.shape[0] // NSA_GROUPS
        return jnp.concatenate(
            [_dot(vt[g * rows:(g + 1) * rows, :], pb[:, g * gl:(g + 1) * gl]) for g in range(NSA_GROUPS)],
            axis=1)

    def compressed_and_select():
        qk = qa_ref[0:LANES, :]
        sc = _dot(kc_ref[0], qk)
        cval = krow((nc, nl), 0) * CMP_STRIDE + (CMP_BLOCK - 1) <= tq((nc, nl))
        sc = jnp.where(cval, sc, NEG)
        e = jnp.exp2(sc - jnp.max(sc, axis=0, keepdims=True))
        pc = e / jnp.sum(e, axis=0, keepdims=True)
        pc = jnp.where(tq((1, nl)) >= CMP_BLOCK - 1, pc, 0.0)
        ocmp_ref[...] = group_pv(vct_ref[0], pc.astype(BF16))

        jb = krow((ns, QT), 0)
        jt = (start + lax.broadcasted_iota(jnp.int32, (ns, QT), 1)) // SLC_BLOCK
        forced = (jb == 0) | (jb == jt) | (jb == jt - 1)
        future = jb > jt
        last_block = (start + QT - 1) // SLC_BLOCK
        n_var = max(ns // RANK_STEP, 1)
        ovt = ovt_ref[...]
        for g in range(NSA_GROUPS):
            psum = pc[:, g * gl:g * gl + QT]
            for r in range(1, NSA_REP):
                psum = psum + pc[:, g * gl + r * QT:g * gl + (r + 1) * QT]
            p_hi = psum.astype(BF16)
            rem = psum - p_hi.astype(F32)
            p_mid = rem.astype(BF16)
            p_lo = (rem - p_mid.astype(F32)).astype(BF16)
            imp = (_dot(ovt, p_hi) + _dot(ovt, p_mid)) + _dot(ovt, p_lo)
            imp = jnp.where(forced, FORCE, jnp.where(future, NEG, imp))
            for v in range(n_var):
                rows = RANK_STEP * (v + 1)

                @pl.when(jnp.minimum(last_block // RANK_STEP, n_var - 1) == v)
                def _(g=g, imp=imp, rows=rows):
                    sel = (_rank_desc(imp[0:rows, :], rows) < topn) & jnp.logical_not(future[0:rows, :])
                    bias = jnp.where(sel, 0.0, NEG)
                    if rows < ns:
                        bias = jnp.concatenate([bias, jnp.full((ns - rows, QT), NEG, F32)], axis=0)
                    for r in range(NSA_REP):
                        h = g * NSA_REP + r
                        qa_ref[LANES:LANES + ns, h * QT:(h + 1) * QT] = bias.astype(BF16)

    def update(s, vt, first):
        cm = jnp.max(s, axis=0, keepdims=True)
        if first:
            m_new = cm
        else:
            m_old = m_ref[...]
            m_new = jnp.maximum(m_old, cm)
        p = jnp.exp2(s - m_new)
        pv = group_pv(vt, p.astype(BF16))
        ps = jnp.sum(p, axis=0, keepdims=True)
        if first:
            l_ref[...] = ps
            acc_ref[...] = pv
        else:
            alpha = jnp.exp2(m_old - m_new)
            l_ref[...] = alpha * l_ref[...] + ps
            acc_ref[...] = alpha * acc_ref[...] + pv
        m_ref[...] = m_new

    def result():
        return acc_ref[...] / l_ref[...]

    n_back = WIN // WIN_CHUNK
    wshape = (WIN_CHUNK, nl)
    local_q = lax.broadcasted_iota(jnp.int32, wshape, 1) & (QT - 1)
    local_k = lax.broadcasted_iota(jnp.int32, wshape, 0)

    def win_scores(c):
        k = kw_ref[0, pl.ds(pl.multiple_of(c * WIN_CHUNK, WIN_CHUNK), WIN_CHUNK), :]
        return _dot(k, qa_ref[0:LANES, :])

    def window_full():
        ss = [win_scores(i - n_back + c) for c in range(n_back + 1)]
        ss[0] = jnp.where(local_k > local_q, ss[0], NEG)
        ss[n_back] = jnp.where(local_k <= local_q, ss[n_back], NEG)
        m = jnp.max(ss[0], axis=0, keepdims=True)
        for s in ss[1:]:
            m = jnp.maximum(m, jnp.max(s, axis=0, keepdims=True))
        ps = [jnp.exp2(s - m) for s in ss]
        den = jnp.sum(ps[0], axis=0, keepdims=True)
        for p in ps[1:]:
            den = den + jnp.sum(p, axis=0, keepdims=True)
        pb = jnp.concatenate([p.astype(BF16) for p in ps], axis=0)
        vt = jnp.concatenate([vwt_ref[0, i - n_back + c] for c in range(n_back + 1)], axis=1)
        owin_ref[...] = group_pv(vt, pb) / den

    def window_head():
        s = jnp.where(local_k <= local_q, win_scores(i), NEG)
        update(s, vwt_ref[0, i], True)

        def body(c, carry):
            update(win_scores(c), vwt_ref[0, c], False)
            return carry

        lax.fori_loop(0, i, body, 0)
        owin_ref[...] = result()

    @pl.when(i >= n_back)
    def _():
        compressed_and_select()
        window_full()

    @pl.when(i < n_back)
    def _():
        compressed_and_select()
        window_head()

    def sel_scores(buf, c):
        k = ksa_ref[0, pl.ds(pl.multiple_of(c * SEL_CHUNK, SEL_CHUNK), SEL_CHUNK), :]
        buf[...] = _dot(k, qa_ref[...])

    def sel_update(buf, c, diag):
        s = buf[...]
        if diag:
            shape = (SEL_CHUNK, nl)
            s = jnp.where(krow(shape, c * SEL_CHUNK) <= tq(shape), s, NEG)
        update(s, vst_ref[0, c], diag)

    cd = start // SEL_CHUNK
    sel_scores(sa_ref, cd)
    sel_scores(sb_ref, 0)
    sel_update(sa_ref, cd, True)

    def sel_pair(p, carry):
        c0 = 2 * p
        sel_scores(sa_ref, c0 + 1)
        sel_update(sb_ref, c0, False)
        sel_scores(sb_ref, jnp.minimum(c0 + 2, cd - 1))
        sel_update(sa_ref, c0 + 1, False)
        return carry

    lax.fori_loop(0, cd // 2, sel_pair, 0)

    @pl.when(cd % 2 == 1)
    def _():
        sel_update(sb_ref, cd - 1, False)

    o_sel = result()
    o_cmp = ocmp_ref[...]
    o_win = owin_ref[...]

    sg = jax.nn.sigmoid(gate_ref[...].T[0:3 * NSA_HEADS, :])
    outs = []
    for h in range(NSA_HEADS):
        lanes = slice(h * QT, (h + 1) * QT)
        outs.append(sg[3 * h:3 * h + 1, :] * o_cmp[:, lanes] + sg[3 * h + 1:3 * h + 2, :] * o_sel[:, lanes]
                    + sg[3 * h + 2:3 * h + 3, :] * o_win[:, lanes])
    o_ref[...] = jnp.concatenate(outs, axis=0).T.astype(o_ref.dtype)


def _nsa_attn(znsa, gq, kc, vct, ovt, ksa, vst, kwn, vwt, b, s):
    ns = s // SLC_BLOCK
    nc = s // CMP_STRIDE
    nq = s // QT
    nl = NSA_HEADS * QT
    width = NSA_HEADS * HEAD_DIM
    kern = functools.partial(_nsa_attn_kernel, ns=ns, nc=nc, topn=min(SLC_TOPN, ns))
    return pl.pallas_call(
        kern,
        out_shape=jax.ShapeDtypeStruct((b * s, width), BF16),
        grid=(b, nq),
        in_specs=[
            pl.BlockSpec((QT, width), lambda bi, i: (bi * nq + i, 0)),
            pl.BlockSpec((QT, LANES), lambda bi, i: (bi * nq + i, 10)),
            pl.BlockSpec((HEAD_DIM, QT), lambda bi, i: (0, 0)),
            pl.BlockSpec((1, nc, LANES), lambda bi, i: (bi, 0, 0)),
            pl.BlockSpec((1, LANES, nc), lambda bi, i: (bi, 0, 0)),
            pl.BlockSpec((ns, nc), lambda bi, i: (0, 0)),
            pl.BlockSpec((1, s, 2 * LANES), lambda bi, i: (bi, 0, 0)),
            pl.BlockSpec((1, s // SEL_CHUNK, LANES, SEL_CHUNK), lambda bi, i: (bi, 0, 0, 0)),
            pl.BlockSpec((1, s, LANES), lambda bi, i: (bi, 0, 0)),
            pl.BlockSpec((1, s // WIN_CHUNK, LANES, WIN_CHUNK), lambda bi, i: (bi, 0, 0, 0)),
        ],
        out_specs=pl.BlockSpec((QT, width), lambda bi, i: (bi * nq + i, 0)),
        scratch_shapes=[
            pltpu.VMEM((2 * LANES, nl), BF16),
            pltpu.VMEM((SEL_CHUNK, nl), F32),
            pltpu.VMEM((SEL_CHUNK, nl), F32),
            pltpu.VMEM((1, nl), F32),
            pltpu.VMEM((1, nl), F32),
            pltpu.VMEM((HEAD_DIM, nl), F32),
            pltpu.VMEM((HEAD_DIM, nl), F32),
            pltpu.VMEM((HEAD_DIM, nl), F32),
        ],
        compiler_params=_params(2),
        name="nsa_attn",
    )(znsa, znsa, gq, kc, vct, ovt, ksa, vst, kwn, vwt)


def _mix_kernel(x_ref, ab_ref, ac_ref, ax_ref, ach_ref, axh_ref, p_ref, ph_ref, sgu_ref, mg0_ref, mg1_ref, mg2_ref,
                mg3_ref, ob_ref, cw_ref, pw_ref, psc_ref, sng_ref, sw_ref, sb_ref, wbr_ref, wo_ref, o_ref, ext_ref,
                ext2_ref, *, tm, tiles_per_seq):
    ti = pl.program_id(0) % tiles_per_seq
    keep = jnp.where(ti == 0, 0.0, 1.0)

    ext_ref[0:HALO, :] = (ach_ref[...].astype(F32) * axh_ref[...].astype(F32)) * keep
    ext_ref[HALO:, :] = ac_ref[...].astype(F32) * ax_ref[...].astype(F32)
    e = ext_ref[...]
    cw = cw_ref[...]
    conv = (cw[0:1, :] * pltpu.roll(e, 2, 0) + cw[1:2, :] * pltpu.roll(e, 1, 0)) + cw[2:3, :] * e
    out_a = ab_ref[...].astype(F32) * conv[HALO:, :]

    p = p_ref[...].astype(F32)
    ext2_ref[0:HALO, :] = ph_ref[...].astype(F32) * keep
    ext2_ref[HALO:, :] = p
    e = ext2_ref[...]
    gw = BRANCH_WIDTH // len(POOL_WINDOWS)
    cnt = (ti * tm + 1 + lax.broadcasted_iota(jnp.int32, (tm, gw), 0)).astype(F32)
    groups = []
    for gi, w in enumerate(POOL_WINDOWS):
        acc = e[:, gi * gw:(gi + 1) * gw]
        span = 1
        while span < w:
            acc = acc + pltpu.roll(acc, span, 0)
            span *= 2
        groups.append(acc[HALO:, :] / jnp.minimum(cnt, float(w)))
    pooled = jnp.concatenate(groups, axis=1) - p
    out_c = _dot(pooled.astype(BF16), pw_ref[...]) * psc_ref[...]

    z = sgu_ref[...].astype(F32)
    u = z[:, :BRANCH_WIDTH]
    v = z[:, BRANCH_WIDTH:]
    v = (v * lax.rsqrt(jnp.mean(v * v, axis=-1, keepdims=True) + RMS_EPS)) * sng_ref[...]
    vb = v.astype(BF16)
    nchunk = tm // SGU_CHUNK
    gd = BRANCH_WIDTH // SGU_GROUPS
    tri = (lax.broadcasted_iota(jnp.int32, (SGU_CHUNK, SGU_CHUNK), 0)
           >= lax.broadcasted_iota(jnp.int32, (SGU_CHUNK, SGU_CHUNK), 1))
    mixed_g = []
    for g in range(SGU_GROUPS):
        wg = jnp.where(tri, sw_ref[g], 0.0).astype(BF16)
        rhs = jnp.concatenate(
            [vb[c * SGU_CHUNK:(c + 1) * SGU_CHUNK, g * gd:(g + 1) * gd] for c in range(nchunk)], axis=1)
        res = _dot(wg, rhs)
        mixed_g.append(jnp.concatenate([res[:, c * gd:(c + 1) * gd] for c in range(nchunk)], axis=0))
    bias = jnp.concatenate([sb_ref[...]] * nchunk, axis=0)
    out_d = u * (jnp.concatenate(mixed_g, axis=1) + bias)

    merged = mg0_ref[...].astype(F32) * _dot(out_a.astype(BF16), wbr_ref[0])
    merged = merged + mg1_ref[...].astype(F32) * _dot(ob_ref[...], wbr_ref[1])
    merged = merged + mg2_ref[...].astype(F32) * _dot(out_c.astype(BF16), wbr_ref[2])
    merged = merged + mg3_ref[...].astype(F32) * _dot(out_d.astype(BF16), wbr_ref[3])
    o_ref[...] = x_ref[...] + _dot(merged.astype(BF16), wo_ref[...])


def _mix_merge(x, za, out_b, cw, pw, psc, sng, sw, sb, wbr, wo, s, tm=256):
    t, d = x.shape
    bw = BRANCH_WIDTH
    tps = s // tm
    hb = tm // HALO
    tile = lambda c: pl.BlockSpec((tm, bw), lambda i, c=c: (i, c))
    halo = lambda c: pl.BlockSpec((HALO, bw), lambda i, c=c: (jnp.maximum(i * hb - 1, 0), c))
    wide = lambda c: pl.BlockSpec((tm, d), lambda i, c=c: (i, c))
    full = lambda a: pl.BlockSpec(a.shape, lambda i, n=a.ndim: (0,) * n)
    consts = (cw, pw, psc, sng, sw, sb, wbr, wo)
    return pl.pallas_call(
        functools.partial(_mix_kernel, tm=tm, tiles_per_seq=tps),
        out_shape=jax.ShapeDtypeStruct((t, d), F32),
        grid=(t // tm,),
        in_specs=[wide(0), tile(0), tile(1), tile(2), halo(1), halo(2), tile(3), halo(3), wide(2),
                  wide(3), wide(4), wide(5), wide(6), tile(0)] + [full(a) for a in consts],
        out_specs=wide(0),
        scratch_shapes=[pltpu.VMEM((tm + HALO, bw), F32), pltpu.VMEM((tm + HALO, bw), F32)],
        compiler_params=_params(1),
        name="mix_merge",
    )(x, za, za, za, za, za, za, za, za, za, za, za, za, out_b, *consts)


def _ffn_act_kernel(x_ref, xh_ref, g_ref, w_ref, cw_ref, o_ref, h_ref, hh_ref, ext_ref, *, tiles_per_seq, rb):
    @pl.when(pl.program_id(1) == 0)
    def _():
        h_ref[...] = _rms_bf16(x_ref[...], g_ref[...])
        hh_ref[...] = _rms_bf16(xh_ref[...], g_ref[...])

    keep = jnp.where(pl.program_id(0) % tiles_per_seq == 0, 0.0, 1.0)
    w = w_ref[...]
    cw = cw_ref[...]
    ext_ref[0:HALO, :] = _dot(hh_ref[...], w[:, :FF_CHUNK]) * keep
    for r0 in range(0, h_ref.shape[0], rb):
        gu = _dot(h_ref[r0:r0 + rb, :], w)
        ext_ref[HALO + r0:HALO + r0 + rb, :] = gu[:, :FF_CHUNK]
        e = ext_ref[r0:r0 + rb + HALO, :]
        conv = ((cw[0:1, :] * pltpu.roll(e, 2, 0) + cw[1:2, :] * pltpu.roll(e, 1, 0)) + cw[2:3, :] * e)[HALO:, :]
        o_ref[r0:r0 + rb, :] = ((conv * jax.nn.sigmoid(conv)) * gu[:, FF_CHUNK:]).astype(o_ref.dtype)


def _ffn_act(x, g, w_gu, cw, s, tm):
    t, d = x.shape
    dff = cw.shape[1]
    hb = tm // HALO
    return pl.pallas_call(
        functools.partial(_ffn_act_kernel, tiles_per_seq=s // tm, rb=min(tm, 512)),
        out_shape=jax.ShapeDtypeStruct((t, dff), BF16),
        grid=(t // tm, dff // FF_CHUNK),
        in_specs=[
            pl.BlockSpec((tm, d), lambda i, j: (i, 0)),
            pl.BlockSpec((HALO, d), lambda i, j: (jnp.maximum(i * hb - 1, 0), 0)),
            pl.BlockSpec((1, d), lambda i, j: (0, 0)),
            pl.BlockSpec((d, 2 * FF_CHUNK), lambda i, j: (0, j)),
            pl.BlockSpec((cw.shape[0], FF_CHUNK), lambda i, j: (0, j)),
        ],
        out_specs=pl.BlockSpec((tm, FF_CHUNK), lambda i, j: (i, j)),
        scratch_shapes=[pltpu.VMEM((tm, d), BF16), pltpu.VMEM((HALO, d), BF16), pltpu.VMEM((tm + HALO, FF_CHUNK), F32)],
        compiler_params=_params(2),
        name="ffn_act",
    )(x, x, g.reshape(1, d), w_gu, cw)


def _ffn_down_kernel(x_ref, a_ref, wd_ref, o_ref):
    o_ref[...] = x_ref[...] + _dot(a_ref[...], wd_ref[...])


def _ffn_down(x, act, wd, tm):
    t, d = x.shape
    dff = wd.shape[0]
    return pl.pallas_call(
        _ffn_down_kernel,
        out_shape=jax.ShapeDtypeStruct((t, d), F32),
        grid=(t // tm,),
        in_specs=[
            pl.BlockSpec((tm, d), lambda i: (i, 0)),
            pl.BlockSpec((tm, dff), lambda i: (i, 0)),
            pl.BlockSpec((dff, d), lambda i: (0, 0)),
        ],
        out_specs=pl.BlockSpec((tm, d), lambda i: (i, 0)),
        compiler_params=_params(1),
        name="ffn_down",
    )(x, act, wd)


def _block_diag(blocks):
    n = len(blocks)
    r, c = blocks[0].shape
    out = jnp.zeros((n * r, n * c), blocks[0].dtype)
    for k, blk in enumerate(blocks):
        out = out.at[k * r:(k + 1) * r, k * c:(k + 1) * c].set(blk)
    return out


def _overlap_t(nc, ns):
    n_cmp = nc - 1
    cs = np.arange(n_cmp) * CMP_STRIDE
    ce = cs + CMP_BLOCK
    ss = np.arange(ns) * SLC_BLOCK
    se = ss + SLC_BLOCK
    ov = np.clip(np.minimum(ce[:, None], se[None]) - np.maximum(cs[:, None], ss[None]), 0, None)
    out = np.zeros((ns, nc), np.float32)
    out[:, :n_cmp] = (ov.astype(np.float32) / CMP_BLOCK).T
    return out


def _compress_params(pe, w1, w2):
    half = CMP_BLOCK // 2
    pe2 = jnp.tile(pe.reshape(2, 2, half, 1, HEAD_DIM), (1, 1, 1, NSA_GROUPS, 1)).reshape(2, 2, half * LANES)
    w1r = w1.reshape(2, 2, half, HEAD_DIM, HEAD_DIM)
    eye = jnp.eye(NSA_GROUPS, dtype=w1.dtype)
    w1bd = jnp.einsum("jpldo,gh->jplgdho", w1r, eye).reshape(2, 2, half * LANES, LANES)
    w2bd = jnp.einsum("jdo,gh->jgdho", w2, eye).reshape(2, LANES, LANES)
    return pe2, w1bd.astype(BF16), w2bd.astype(BF16)


def kernel(x, norm1_g, w_in, conv_a_w, qk_norm_g, cmp_pe, cmp_w1, cmp_w2, pool_w, pool_scale, sgu_norm_g, sgu_w,
           sgu_b, w_branch, w_o, norm2_g, w_up, conv_ff_w, w_down):
    b, s, d = x.shape
    depth = w_in.shape[0]
    t = b * s
    bw = BRANCH_WIDTH
    kvw = 6 * NSA_GROUPS * HEAD_DIM
    ngate = 3 * NSA_HEADS
    o_q = 3 * bw
    o_kv = o_q + bw
    o_gate = o_kv + kvw
    o_pool = o_gate + ngate
    dff = w_down.shape[1]
    ns = s // SLC_BLOCK
    nc = s // CMP_STRIDE
    ovt = jnp.asarray(_overlap_t(nc, ns), BF16)
    xf = x.reshape(t, d)
    tm = min(1024, t)
    tm_wide = min(2048, t)
    for l in range(depth):
        w = w_in[l]
        w_a = jnp.concatenate([w[:, :o_q], w[:, o_pool:]], axis=1).astype(BF16)
        w_n = jnp.concatenate([w[:, o_q:o_pool], jnp.zeros((d, LANES - ngate), w.dtype)], axis=1).astype(BF16)
        za = _norm_matmul(xf, norm1_g[l], w_a, BF16, tm_wide, 2 * bw, "in_proj_a",
                          epilogues=((0, 2, "plain"), (2, 3, "gelu"), (3, 3 + N_BRANCH, "sigmoid")))
        znsa = _norm_matmul(xf, norm1_g[l], w_n, F32, tm, w_n.shape[1], "in_proj_nsa")

        g2 = lambda v: jnp.tile(v, NSA_GROUPS).reshape(1, LANES)
        ksa, vst, kwn, vwt = _kv_prep(znsa, g2(qk_norm_g[l, 2]), g2(qk_norm_g[l, 3]), b, s)
        pe2, w1bd, w2bd = _compress_params(cmp_pe[l], cmp_w1[l], cmp_w2[l])
        kc, vct = _compress(znsa, pe2, w1bd, w2bd, g2(qk_norm_g[l, 1]), b, s)
        gq = jnp.broadcast_to(qk_norm_g[l, 0][:, None], (HEAD_DIM, QT))
        out_b = _nsa_attn(znsa, gq, kc, vct, ovt, ksa, vst, kwn, vwt, b, s)

        pw = _block_diag([pool_w[l, g] for g in range(len(POOL_WINDOWS))]).astype(BF16)
        sb = jnp.repeat(sgu_b[l].T, bw // SGU_GROUPS, axis=1)
        xf = _mix_merge(xf, za, out_b, conv_a_w[l], pw, pool_scale[l].reshape(1, bw), sgu_norm_g[l].reshape(1, bw),
                        sgu_w[l], sb, w_branch[l].astype(BF16), w_o[l].astype(BF16), s)

        nj = dff // FF_CHUNK
        w_gu = jnp.concatenate([w_up[l][:, :dff].reshape(d, nj, FF_CHUNK), w_up[l][:, dff:].reshape(d, nj, FF_CHUNK)],
                               axis=2).reshape(d, 2 * dff).astype(BF16)
        act = _ffn_act(xf, norm2_g[l], w_gu, conv_ff_w[l], s, tm_wide)
        xf = _ffn_down(xf, act, w_down[l].astype(BF16), tm)
    return xf.reshape(b, s, d)
```

```python
import functools

import jax
import jax.numpy as jnp
import numpy as np
from jax import lax
from jax.experimental import pallas as pl
from jax.experimental.pallas import tpu as pltpu

F32 = jnp.float32
BF16 = jnp.bfloat16

HEAD_DIM = 64
NSA_HEADS = 8
NSA_GROUPS = 2
NSA_REP = NSA_HEADS // NSA_GROUPS
CMP_BLOCK = 32
CMP_STRIDE = 16
SLC_BLOCK = 64
SLC_TOPN = 16
WIN = 512
NEG = -1e30
FORCE = 1e6
RMS_EPS = 1e-6
LOG2E = 1.4426950408889634
POOL_WINDOWS = (2, 4, 8, 16)
SGU_CHUNK = 128
SGU_GROUPS = 4
BRANCH_WIDTH = 512
N_BRANCH = 4

LANES = 128
HALO = 16
QT = 128
SEL_CHUNK = 256
WIN_CHUNK = 128
FF_CHUNK = 256
RANK_STEP = 16
VMEM_LIMIT = 56 * 1024 * 1024


def _params(n_axes):
    return pltpu.CompilerParams(dimension_semantics=("arbitrary",) * n_axes, vmem_limit_bytes=VMEM_LIMIT)


def _dot(a, b):
    return jnp.dot(a, b, preferred_element_type=F32)


def _rms_bf16(x, g):
    ms = jnp.mean(x * x, axis=-1, keepdims=True)
    return ((x * lax.rsqrt(ms + RMS_EPS)) * g).astype(BF16)


def _erf_gelu(x):
    return 0.5 * x * (1.0 + lax.erf(x * (2.0 ** -0.5)))


def _norm_matmul_kernel(x_ref, g_ref, w_ref, o_ref, h_ref):
    @pl.when(pl.program_id(1) == 0)
    def _():
        h_ref[...] = _rms_bf16(x_ref[...], g_ref[...])

    o_ref[...] = _dot(h_ref[...], w_ref[...]).astype(o_ref.dtype)


def _norm_matmul(x, g, w, out_dtype, tm, tn, name):
    t, d = x.shape
    n = w.shape[1]
    return pl.pallas_call(
        _norm_matmul_kernel,
        out_shape=jax.ShapeDtypeStruct((t, n), out_dtype),
        grid=(t // tm, n // tn),
        in_specs=[
            pl.BlockSpec((tm, d), lambda i, j: (i, 0)),
            pl.BlockSpec((1, d), lambda i, j: (0, 0)),
            pl.BlockSpec((d, tn), lambda i, j: (0, j)),
        ],
        out_specs=pl.BlockSpec((tm, tn), lambda i, j: (i, j)),
        scratch_shapes=[pltpu.VMEM((tm, d), BF16)],
        compiler_params=_params(2),
        name=name,
    )(x, g.reshape(1, d), w)


def _group_rms(x, g):
    lane = lax.broadcasted_iota(jnp.int32, x.shape, 1)
    lo = lane < HEAD_DIM
    x2 = x * x
    s_lo = jnp.sum(jnp.where(lo, x2, 0.0), axis=-1, keepdims=True)
    s_hi = jnp.sum(jnp.where(lo, 0.0, x2), axis=-1, keepdims=True)
    ms = jnp.where(lo, s_lo, s_hi) * (1.0 / HEAD_DIM)
    return (x * lax.rsqrt(ms + RMS_EPS)) * g


def _kv_prep_kernel(ks_ref, vs_ref, kw_ref, vw_ref, gs_ref, gw_ref, ksa_ref, vst_ref, kwn_ref, vwt_ref, *, ts):
    j = pl.program_id(1)
    ksn = _group_rms(ks_ref[...], gs_ref[...]).astype(BF16)
    row = lax.broadcasted_iota(jnp.int32, (ts, LANES), 0) + j * ts
    col = lax.broadcasted_iota(jnp.int32, (ts, LANES), 1)
    onehot = jnp.where(row // SLC_BLOCK == col, 1.0, 0.0).astype(BF16)
    ksa_ref[0, :, 0:LANES] = ksn
    ksa_ref[0, :, LANES:2 * LANES] = onehot
    kwn_ref[0] = _group_rms(kw_ref[...], gw_ref[...]).astype(BF16)
    vs = vs_ref[...]
    for c in range(ts // SEL_CHUNK):
        vst_ref[0, c] = vs[c * SEL_CHUNK:(c + 1) * SEL_CHUNK, :].T.astype(BF16)
    vw = vw_ref[...]
    for c in range(ts // WIN_CHUNK):
        vwt_ref[0, c] = vw[c * WIN_CHUNK:(c + 1) * WIN_CHUNK, :].T.astype(BF16)


def _kv_prep(znsa, gs, gw, b, s, ts=512):
    nt = s // ts
    col = lambda c: pl.BlockSpec((ts, LANES), lambda bi, j, c=c: (bi * nt + j, c))
    vec = pl.BlockSpec((1, LANES), lambda bi, j: (0, 0))
    return pl.pallas_call(
        functools.partial(_kv_prep_kernel, ts=ts),
        out_shape=(
            jax.ShapeDtypeStruct((b, s, 2 * LANES), BF16),
            jax.ShapeDtypeStruct((b, s // SEL_CHUNK, LANES, SEL_CHUNK), BF16),
            jax.ShapeDtypeStruct((b, s, LANES), BF16),
            jax.ShapeDtypeStruct((b, s // WIN_CHUNK, LANES, WIN_CHUNK), BF16),
        ),
        grid=(b, nt),
        in_specs=[col(6), col(7), col(8), col(9), vec, vec],
        out_specs=(
            pl.BlockSpec((1, ts, 2 * LANES), lambda bi, j: (bi, j, 0)),
            pl.BlockSpec((1, ts // SEL_CHUNK, LANES, SEL_CHUNK), lambda bi, j: (bi, j, 0, 0)),
            pl.BlockSpec((1, ts, LANES), lambda bi, j: (bi, j, 0)),
            pl.BlockSpec((1, ts // WIN_CHUNK, LANES, WIN_CHUNK), lambda bi, j: (bi, j, 0, 0)),
        ),
        compiler_params=_params(2),
        name="kv_prep",
    )(znsa, znsa, znsa, znsa, gs, gw)


def _compress_kernel(kc_ref, vc_ref, pe_ref, w1_ref, w2_ref, g_ref, kco_ref, vct_ref, u_ref, *, nc):
    half = CMP_BLOCK // 2
    for j, src in enumerate((kc_ref, vc_ref)):
        for l in range(half):
            u_ref[:, l * LANES:(l + 1) * LANES] = src[pl.ds(l, nc, stride=CMP_STRIDE), :]
        u = u_ref[...]
        top = _dot((u + pe_ref[j, 0:1, :]).astype(BF16), w1_ref[j, 0])
        bot = _dot((u + pe_ref[j, 1:2, :]).astype(BF16), w1_ref[j, 1])
        pre = top + pltpu.roll(bot, nc - 1, 0)
        hid = pre * jax.nn.sigmoid(pre)
        out = _dot(hid.astype(BF16), w2_ref[j])
        if j == 0:
            kco_ref[0] = _group_rms(out, g_ref[...]).astype(BF16)
        else:
            vct_ref[0] = out.T.astype(BF16)


def _compress(znsa, pe2, w1bd, w2bd, g1, b, s):
    nc = s // CMP_STRIDE
    kw = (CMP_BLOCK // 2) * LANES
    return pl.pallas_call(
        functools.partial(_compress_kernel, nc=nc),
        out_shape=(
            jax.ShapeDtypeStruct((b, nc, LANES), BF16),
            jax.ShapeDtypeStruct((b, LANES, nc), BF16),
        ),
        grid=(b,),
        in_specs=[
            pl.BlockSpec((s, LANES), lambda bi: (bi, 4)),
            pl.BlockSpec((s, LANES), lambda bi: (bi, 5)),
            pl.BlockSpec((2, 2, kw), lambda bi: (0, 0, 0)),
            pl.BlockSpec((2, 2, kw, LANES), lambda bi: (0, 0, 0, 0)),
            pl.BlockSpec((2, LANES, LANES), lambda bi: (0, 0, 0)),
            pl.BlockSpec((1, LANES), lambda bi: (0, 0)),
        ],
        out_specs=(
            pl.BlockSpec((1, nc, LANES), lambda bi: (bi, 0, 0)),
            pl.BlockSpec((1, LANES, nc), lambda bi: (bi, 0, 0)),
        ),
        scratch_shapes=[pltpu.VMEM((nc, kw), F32)],
        compiler_params=_params(1),
        name="compress",
    )(znsa, znsa, pe2, w1bd, w2bd, g1)


def _rank_desc(imp, ns):
    nq = imp.shape[1]
    sub = lax.broadcasted_iota(jnp.int32, (8, nq), 0)
    blocks = [imp[8 * a:8 * a + 8, :] for a in range(ns // 8)]
    rank = [jnp.zeros((8, nq), F32) for _ in blocks]
    for k in range(ns):
        ka, kr = divmod(k, 8)
        rk = jnp.broadcast_to(imp[k:k + 1, :], (8, nq))
        for a, blk in enumerate(blocks):
            if a > ka:
                ahead = rk >= blk
            elif a < ka:
                ahead = rk > blk
            else:
                rank[a] = rank[a] + jnp.where(sub > kr, jnp.where(rk >= blk, 1.0, 0.0), jnp.where(rk > blk, 1.0, 0.0))
                continue
            rank[a] = rank[a] + jnp.where(ahead, 1.0, 0.0)
    return jnp.concatenate(rank, axis=0)


def _nsa_attn_kernel(q_ref, gate_ref, gq_ref, kc_ref, vct_ref, ovt_ref, ksa_ref, vst_ref, kw_ref, vwt_ref,
                     o_ref, qa_ref, sa_ref, sb_ref, m_ref, l_ref, acc_ref, ocmp_ref, owin_ref, *, ns, nc, topn):
    i = pl.program_id(1)
    start = i * QT
    nl = NSA_HEADS * QT
    gl = NSA_REP * QT

    qt = q_ref[...].T.reshape(NSA_HEADS, HEAD_DIM, QT)
    ms = jnp.mean(qt * qt, axis=1, keepdims=True)
    qn = ((qt * lax.rsqrt(ms + RMS_EPS)) * gq_ref[...][None]) * (HEAD_DIM ** -0.5 * LOG2E)
    qa_ref[...] = jnp.zeros(qa_ref.shape, BF16)
    for h in range(NSA_HEADS):
        g = h // NSA_REP
        qa_ref[g * HEAD_DIM:(g + 1) * HEAD_DIM, h * QT:(h + 1) * QT] = qn[h].astype(BF16)

    def tq(shape):
        return start + (lax.broadcasted_iota(jnp.int32, shape, 1) & (QT - 1))

    def krow(shape, base):
        return base + lax.broadcasted_iota(jnp.int32, shape, 0)

    def group_pv(vt, pb):
        rows = vt.shape[0] // NSA_GROUPS
        return jnp.concatenate(
            [_dot(vt[g * rows:(g + 1) * rows, :], pb[:, g * gl:(g + 1) * gl]) for g in range(NSA_GROUPS)],
            axis=1)

    def compressed_and_select():
        qk = qa_ref[0:LANES, :]
        sc = _dot(kc_ref[0], qk)
        cval = krow((nc, nl), 0) * CMP_STRIDE + (CMP_BLOCK - 1) <= tq((nc, nl))
        sc = jnp.where(cval, sc, NEG)
        e = jnp.exp2(sc - jnp.max(sc, axis=0, keepdims=True))
        pc = e / jnp.sum(e, axis=0, keepdims=True)
        pc = jnp.where(tq((1, nl)) >= CMP_BLOCK - 1, pc, 0.0)
        ocmp_ref[...] = group_pv(vct_ref[0], pc.astype(BF16))

        jb = krow((ns, QT), 0)
        jt = (start + lax.broadcasted_iota(jnp.int32, (ns, QT), 1)) // SLC_BLOCK
        forced = (jb == 0) | (jb == jt) | (jb == jt - 1)
        future = jb > jt
        last_block = (start + QT - 1) // SLC_BLOCK
        n_var = max(ns // RANK_STEP, 1)
        ovt = ovt_ref[...]
        for g in range(NSA_GROUPS):
            psum = pc[:, g * gl:g * gl + QT]
            for r in range(1, NSA_REP):
                psum = psum + pc[:, g * gl + r * QT:g * gl + (r + 1) * QT]
            p_hi = psum.astype(BF16)
            rem = psum - p_hi.astype(F32)
            p_mid = rem.astype(BF16)
            p_lo = (rem - p_mid.astype(F32)).astype(BF16)
            imp = (_dot(ovt, p_hi) + _dot(ovt, p_mid)) + _dot(ovt, p_lo)
            imp = jnp.where(forced, FORCE, jnp.where(future, NEG, imp))
            for v in range(n_var):
                rows = RANK_STEP * (v + 1)

                @pl.when(jnp.minimum(last_block // RANK_STEP, n_var - 1) == v)
                def _(g=g, imp=imp, rows=rows):
                    sel = (_rank_desc(imp[0:rows, :], rows) < topn) & jnp.logical_not(future[0:rows, :])
                    bias = jnp.where(sel, 0.0, NEG)
                    if rows < ns:
                        bias = jnp.concatenate([bias, jnp.full((ns - rows, QT), NEG, F32)], axis=0)
                    for r in range(NSA_REP):
                        h = g * NSA_REP + r
                        qa_ref[LANES:LANES + ns, h * QT:(h + 1) * QT] = bias.astype(BF16)

    def update(s, vt, first):
        cm = jnp.max(s, axis=0, keepdims=True)
        if first:
            m_new = cm
        else:
            m_old = m_ref[...]
            m_new = jnp.maximum(m_old, cm)
        p = jnp.exp2(s - m_new)
        pv = group_pv(vt, p.astype(BF16))
        ps = jnp.sum(p, axis=0, keepdims=True)
        if first:
            l_ref[...] = ps
            acc_ref[...] = pv
        else:
            alpha = jnp.exp2(m_old - m_new)
            l_ref[...] = alpha * l_ref[...] + ps
            acc_ref[...] = alpha * acc_ref[...] + pv
        m_ref[...] = m_new

    def result():
        return acc_ref[...] / l_ref[...]

    n_back = WIN // WIN_CHUNK
    wshape = (WIN_CHUNK, nl)
    local_q = lax.broadcasted_iota(jnp.int32, wshape, 1) & (QT - 1)
    local_k = lax.broadcasted_iota(jnp.int32, wshape, 0)

    def window():
        ss, vts = [], []
        for c in range(n_back + 1):
            ci = i - n_back + c
            cl = jnp.maximum(ci, 0)
            k = kw_ref[0, pl.ds(pl.multiple_of(cl * WIN_CHUNK, WIN_CHUNK), WIN_CHUNK), :]
            s = _dot(k, qa_ref[0:LANES, :])
            if c == 0:
                s = jnp.where(local_k > local_q + jnp.where(ci >= 0, 0, WIN_CHUNK), s, NEG)
            elif c == n_back:
                s = jnp.where(local_k <= local_q, s, NEG)
            else:
                s = s + jnp.where(ci >= 0, 0.0, NEG)
            ss.append(s)
            vts.append(vwt_ref[0, cl])
        m = jnp.max(ss[0], axis=0, keepdims=True)
        for s in ss[1:]:
            m = jnp.maximum(m, jnp.max(s, axis=0, keepdims=True))
        ps = [jnp.exp2(s - m) for s in ss]
        den = jnp.sum(ps[0], axis=0, keepdims=True)
        for p in ps[1:]:
            den = den + jnp.sum(p, axis=0, keepdims=True)
        pb = jnp.concatenate([p.astype(BF16) for p in ps], axis=0)
        owin_ref[...] = group_pv(jnp.concatenate(vts, axis=1), pb) / den

    compressed_and_select()
    window()

    def sel_scores(buf, c):
        k = ksa_ref[0, pl.ds(pl.multiple_of(c * SEL_CHUNK, SEL_CHUNK), SEL_CHUNK), :]
        buf[...] = _dot(k, qa_ref[...])

    def sel_update(buf, c, diag):
        s = buf[...]
        if diag:
            shape = (SEL_CHUNK, nl)
            s = jnp.where(krow(shape, c * SEL_CHUNK) <= tq(shape), s, NEG)
        update(s, vst_ref[0, c], diag)

    cd = start // SEL_CHUNK
    sel_scores(sa_ref, cd)
    sel_scores(sb_ref, 0)
    sel_update(sa_ref, cd, True)

    def sel_pair(p, carry):
        c0 = 2 * p
        sel_scores(sa_ref, c0 + 1)
        sel_update(sb_ref, c0, False)
        sel_scores(sb_ref, jnp.minimum(c0 + 2, cd - 1))
        sel_update(sa_ref, c0 + 1, False)
        return carry

    lax.fori_loop(0, cd // 2, sel_pair, 0)

    @pl.when(cd % 2 == 1)
    def _():
        sel_update(sb_ref, cd - 1, False)

    o_sel = result()
    o_cmp = ocmp_ref[...]
    o_win = owin_ref[...]

    sg = jax.nn.sigmoid(gate_ref[...].T[0:3 * NSA_HEADS, :])
    outs = []
    for h in range(NSA_HEADS):
        lanes = slice(h * QT, (h + 1) * QT)
        outs.append(sg[3 * h:3 * h + 1, :] * o_cmp[:, lanes] + sg[3 * h + 1:3 * h + 2, :] * o_sel[:, lanes]
                    + sg[3 * h + 2:3 * h + 3, :] * o_win[:, lanes])
    o_ref[...] = jnp.concatenate(outs, axis=0).T.astype(o_ref.dtype)


def _nsa_attn(znsa, gq, kc, vct, ovt, ksa, vst, kwn, vwt, b, s):
    ns = s // SLC_BLOCK
    nc = s // CMP_STRIDE
    nq = s // QT
    nl = NSA_HEADS * QT
    width = NSA_HEADS * HEAD_DIM
    kern = functools.partial(_nsa_attn_kernel, ns=ns, nc=nc, topn=min(SLC_TOPN, ns))
    return pl.pallas_call(
        kern,
        out_shape=jax.ShapeDtypeStruct((b * s, width), BF16),
        grid=(b, nq),
        in_specs=[
            pl.BlockSpec((QT, width), lambda bi, i: (bi * nq + i, 0)),
            pl.BlockSpec((QT, LANES), lambda bi, i: (bi * nq + i, 10)),
            pl.BlockSpec((HEAD_DIM, QT), lambda bi, i: (0, 0)),
            pl.BlockSpec((1, nc, LANES), lambda bi, i: (bi, 0, 0)),
            pl.BlockSpec((1, LANES, nc), lambda bi, i: (bi, 0, 0)),
            pl.BlockSpec((ns, nc), lambda bi, i: (0, 0)),
            pl.BlockSpec((1, s, 2 * LANES), lambda bi, i: (bi, 0, 0)),
            pl.BlockSpec((1, s // SEL_CHUNK, LANES, SEL_CHUNK), lambda bi, i: (bi, 0, 0, 0)),
            pl.BlockSpec((1, s, LANES), lambda bi, i: (bi, 0, 0)),
            pl.BlockSpec((1, s // WIN_CHUNK, LANES, WIN_CHUNK), lambda bi, i: (bi, 0, 0, 0)),
        ],
        out_specs=pl.BlockSpec((QT, width), lambda bi, i: (bi * nq + i, 0)),
        scratch_shapes=[
            pltpu.VMEM((2 * LANES, nl), BF16),
            pltpu.VMEM((SEL_CHUNK, nl), F32),
            pltpu.VMEM((SEL_CHUNK, nl), F32),
            pltpu.VMEM((1, nl), F32),
            pltpu.VMEM((1, nl), F32),
            pltpu.VMEM((HEAD_DIM, nl), F32),
            pltpu.VMEM((HEAD_DIM, nl), F32),
            pltpu.VMEM((HEAD_DIM, nl), F32),
        ],
        compiler_params=_params(2),
        name="nsa_attn",
    )(znsa, znsa, gq, kc, vct, ovt, ksa, vst, kwn, vwt)


def _mix_kernel(x_ref, ab_ref, ac_ref, ax_ref, ach_ref, axh_ref, p_ref, ph_ref, sgu_ref, mg0_ref, mg1_ref, mg2_ref,
                mg3_ref, ob_ref, cw_ref, pw_ref, psc_ref, sng_ref, sw_ref, sb_ref, wbr_ref, wo_ref, o_ref, ext_ref,
                ext2_ref, *, tm, tiles_per_seq):
    ti = pl.program_id(0) % tiles_per_seq
    keep = jnp.where(ti == 0, 0.0, 1.0)

    ext_ref[0:HALO, :] = (ach_ref[...].astype(F32) * axh_ref[...].astype(F32)) * keep
    ext_ref[HALO:, :] = ac_ref[...].astype(F32) * ax_ref[...].astype(F32)
    e = ext_ref[...]
    cw = cw_ref[...]
    conv = (cw[0:1, :] * pltpu.roll(e, 2, 0) + cw[1:2, :] * pltpu.roll(e, 1, 0)) + cw[2:3, :] * e
    out_a = ab_ref[...].astype(F32) * conv[HALO:, :]

    p = p_ref[...].astype(F32)
    ext2_ref[0:HALO, :] = ph_ref[...].astype(F32) * keep
    ext2_ref[HALO:, :] = p
    e = ext2_ref[...]
    gw = BRANCH_WIDTH // len(POOL_WINDOWS)
    cnt = (ti * tm + 1 + lax.broadcasted_iota(jnp.int32, (tm, gw), 0)).astype(F32)
    groups = []
    for gi, w in enumerate(POOL_WINDOWS):
        acc = e[:, gi * gw:(gi + 1) * gw]
        span = 1
        while span < w:
            acc = acc + pltpu.roll(acc, span, 0)
            span *= 2
        groups.append(acc[HALO:, :] / jnp.minimum(cnt, float(w)))
    pooled = jnp.concatenate(groups, axis=1) - p
    out_c = _dot(pooled.astype(BF16), pw_ref[...]) * psc_ref[...]

    z = _erf_gelu(sgu_ref[...].astype(F32))
    u = z[:, :BRANCH_WIDTH]
    v = z[:, BRANCH_WIDTH:]
    v = (v * lax.rsqrt(jnp.mean(v * v, axis=-1, keepdims=True) + RMS_EPS)) * sng_ref[...]
    vb = v.astype(BF16)
    nchunk = tm // SGU_CHUNK
    gd = BRANCH_WIDTH // SGU_GROUPS
    tri = (lax.broadcasted_iota(jnp.int32, (SGU_CHUNK, SGU_CHUNK), 0)
           >= lax.broadcasted_iota(jnp.int32, (SGU_CHUNK, SGU_CHUNK), 1))
    mixed_g = []
    for g in range(SGU_GROUPS):
        wg = jnp.where(tri, sw_ref[g], 0.0).astype(BF16)
        rhs = jnp.concatenate(
            [vb[c * SGU_CHUNK:(c + 1) * SGU_CHUNK, g * gd:(g + 1) * gd] for c in range(nchunk)], axis=1)
        res = _dot(wg, rhs)
        mixed_g.append(jnp.concatenate([res[:, c * gd:(c + 1) * gd] for c in range(nchunk)], axis=0))
    bias = jnp.concatenate([sb_ref[...]] * nchunk, axis=0)
    out_d = u * (jnp.concatenate(mixed_g, axis=1) + bias)

    merged = jax.nn.sigmoid(mg0_ref[...].astype(F32)) * _dot(out_a.astype(BF16), wbr_ref[0])
    merged = merged + jax.nn.sigmoid(mg1_ref[...].astype(F32)) * _dot(ob_ref[...], wbr_ref[1])
    merged = merged + jax.nn.sigmoid(mg2_ref[...].astype(F32)) * _dot(out_c.astype(BF16), wbr_ref[2])
    merged = merged + jax.nn.sigmoid(mg3_ref[...].astype(F32)) * _dot(out_d.astype(BF16), wbr_ref[3])
    o_ref[...] = x_ref[...] + _dot(merged.astype(BF16), wo_ref[...])


def _mix_merge(x, za, out_b, cw, pw, psc, sng, sw, sb, wbr, wo, s, tm=256):
    t, d = x.shape
    bw = BRANCH_WIDTH
    tps = s // tm
    hb = tm // HALO
    tile = lambda c: pl.BlockSpec((tm, bw), lambda i, c=c: (i, c))
    halo = lambda c: pl.BlockSpec((HALO, bw), lambda i, c=c: (jnp.maximum(i * hb - 1, 0), c))
    wide = lambda c: pl.BlockSpec((tm, d), lambda i, c=c: (i, c))
    full = lambda a: pl.BlockSpec(a.shape, lambda i, n=a.ndim: (0,) * n)
    consts = (cw, pw, psc, sng, sw, sb, wbr, wo)
    return pl.pallas_call(
        functools.partial(_mix_kernel, tm=tm, tiles_per_seq=tps),
        out_shape=jax.ShapeDtypeStruct((t, d), F32),
        grid=(t // tm,),
        in_specs=[wide(0), tile(0), tile(1), tile(2), halo(1), halo(2), tile(3), halo(3), wide(2),
                  wide(3), wide(4), wide(5), wide(6), tile(0)] + [full(a) for a in consts],
        out_specs=wide(0),
        scratch_shapes=[pltpu.VMEM((tm + HALO, bw), F32), pltpu.VMEM((tm + HALO, bw), F32)],
        compiler_params=_params(1),
        name="mix_merge",
    )(x, za, za, za, za, za, za, za, za, za, za, za, za, out_b, *consts)


def _ffn_act_kernel(x_ref, xh_ref, g_ref, w_ref, cw_ref, o_ref, h_ref, hh_ref, ext_ref, *, tiles_per_seq, rb):
    @pl.when(pl.program_id(1) == 0)
    def _():
        h_ref[...] = _rms_bf16(x_ref[...], g_ref[...])
        hh_ref[...] = _rms_bf16(xh_ref[...], g_ref[...])

    keep = jnp.where(pl.program_id(0) % tiles_per_seq == 0, 0.0, 1.0)
    w = w_ref[...]
    cw = cw_ref[...]
    ext_ref[0:HALO, :] = _dot(hh_ref[...], w[:, :FF_CHUNK]) * keep
    for r0 in range(0, h_ref.shape[0], rb):
        gu = _dot(h_ref[r0:r0 + rb, :], w)
        ext_ref[HALO + r0:HALO + r0 + rb, :] = gu[:, :FF_CHUNK]
        e = ext_ref[r0:r0 + rb + HALO, :]
        conv = ((cw[0:1, :] * pltpu.roll(e, 2, 0) + cw[1:2, :] * pltpu.roll(e, 1, 0)) + cw[2:3, :] * e)[HALO:, :]
        o_ref[r0:r0 + rb, :] = ((conv * jax.nn.sigmoid(conv)) * gu[:, FF_CHUNK:]).astype(o_ref.dtype)


def _ffn_act(x, g, w_gu, cw, s, tm):
    t, d = x.shape
    dff = cw.shape[1]
    hb = tm // HALO
    return pl.pallas_call(
        functools.partial(_ffn_act_kernel, tiles_per_seq=s // tm, rb=min(tm, 512)),
        out_shape=jax.ShapeDtypeStruct((t, dff), BF16),
        grid=(t // tm, dff // FF_CHUNK),
        in_specs=[
            pl.BlockSpec((tm, d), lambda i, j: (i, 0)),
            pl.BlockSpec((HALO, d), lambda i, j: (jnp.maximum(i * hb - 1, 0), 0)),
            pl.BlockSpec((1, d), lambda i, j: (0, 0)),
            pl.BlockSpec((d, 2 * FF_CHUNK), lambda i, j: (0, j)),
            pl.BlockSpec((cw.shape[0], FF_CHUNK), lambda i, j: (0, j)),
        ],
        out_specs=pl.BlockSpec((tm, FF_CHUNK), lambda i, j: (i, j)),
        scratch_shapes=[pltpu.VMEM((tm, d), BF16), pltpu.VMEM((HALO, d), BF16), pltpu.VMEM((tm + HALO, FF_CHUNK), F32)],
        compiler_params=_params(2),
        name="ffn_act",
    )(x, x, g.reshape(1, d), w_gu, cw)


def _ffn_down_kernel(x_ref, a_ref, wd_ref, o_ref):
    o_ref[...] = x_ref[...] + _dot(a_ref[...], wd_ref[...])


def _ffn_down(x, act, wd, tm):
    t, d = x.shape
    dff = wd.shape[0]
    return pl.pallas_call(
        _ffn_down_kernel,
        out_shape=jax.ShapeDtypeStruct((t, d), F32),
        grid=(t // tm,),
        in_specs=[
            pl.BlockSpec((tm, d), lambda i: (i, 0)),
            pl.BlockSpec((tm, dff), lambda i: (i, 0)),
            pl.BlockSpec((dff, d), lambda i: (0, 0)),
        ],
        out_specs=pl.BlockSpec((tm, d), lambda i: (i, 0)),
        compiler_params=_params(1),
        name="ffn_down",
    )(x, act, wd)


def _block_diag(blocks):
    n = len(blocks)
    r, c = blocks[0].shape
    out = jnp.zeros((n * r, n * c), blocks[0].dtype)
    for k, blk in enumerate(blocks):
        out = out.at[k * r:(k + 1) * r, k * c:(k + 1) * c].set(blk)
    return out


def _overlap_t(nc, ns):
    n_cmp = nc - 1
    cs = np.arange(n_cmp) * CMP_STRIDE
    ce = cs + CMP_BLOCK
    ss = np.arange(ns) * SLC_BLOCK
    se = ss + SLC_BLOCK
    ov = np.clip(np.minimum(ce[:, None], se[None]) - np.maximum(cs[:, None], ss[None]), 0, None)
    out = np.zeros((ns, nc), np.float32)
    out[:, :n_cmp] = (ov.astype(np.float32) / CMP_BLOCK).T
    return out


def _compress_params(pe, w1, w2):
    half = CMP_BLOCK // 2
    pe2 = jnp.tile(pe.reshape(2, 2, half, 1, HEAD_DIM), (1, 1, 1, NSA_GROUPS, 1)).reshape(2, 2, half * LANES)
    w1r = w1.reshape(2, 2, half, HEAD_DIM, HEAD_DIM)
    eye = jnp.eye(NSA_GROUPS, dtype=w1.dtype)
    w1bd = jnp.einsum("jpldo,gh->jplgdho", w1r, eye).reshape(2, 2, half * LANES, LANES)
    w2bd = jnp.einsum("jdo,gh->jgdho", w2, eye).reshape(2, LANES, LANES)
    return pe2, w1bd.astype(BF16), w2bd.astype(BF16)


def kernel(x, norm1_g, w_in, conv_a_w, qk_norm_g, cmp_pe, cmp_w1, cmp_w2, pool_w, pool_scale, sgu_norm_g, sgu_w,
           sgu_b, w_branch, w_o, norm2_g, w_up, conv_ff_w, w_down):
    b, s, d = x.shape
    depth = w_in.shape[0]
    t = b * s
    bw = BRANCH_WIDTH
    kvw = 6 * NSA_GROUPS * HEAD_DIM
    ngate = 3 * NSA_HEADS
    o_q = 3 * bw
    o_kv = o_q + bw
    o_gate = o_kv + kvw
    o_pool = o_gate + ngate
    dff = w_down.shape[1]
    ns = s // SLC_BLOCK
    nc = s // CMP_STRIDE
    ovt = jnp.asarray(_overlap_t(nc, ns), BF16)
    xf = x.reshape(t, d)
    tm = min(1024, t)
    tm_wide = min(2048, t)
    for l in range(depth):
        w = w_in[l]
        w_a = jnp.concatenate([w[:, :o_q], w[:, o_pool:]], axis=1).astype(BF16)
        w_n = jnp.concatenate([w[:, o_q:o_pool], jnp.zeros((d, LANES - ngate), w.dtype)], axis=1).astype(BF16)
        za = _norm_matmul(xf, norm1_g[l], w_a, BF16, tm_wide, 2 * bw, "in_proj_a")
        znsa = _norm_matmul(xf, norm1_g[l], w_n, F32, tm, w_n.shape[1], "in_proj_nsa")

        g2 = lambda v: jnp.tile(v, NSA_GROUPS).reshape(1, LANES)
        ksa, vst, kwn, vwt = _kv_prep(znsa, g2(qk_norm_g[l, 2]), g2(qk_norm_g[l, 3]), b, s)
        pe2, w1bd, w2bd = _compress_params(cmp_pe[l], cmp_w1[l], cmp_w2[l])
        kc, vct = _compress(znsa, pe2, w1bd, w2bd, g2(qk_norm_g[l, 1]), b, s)
        gq = jnp.broadcast_to(qk_norm_g[l, 0][:, None], (HEAD_DIM, QT))
        out_b = _nsa_attn(znsa, gq, kc, vct, ovt, ksa, vst, kwn, vwt, b, s)

        pw = _block_diag([pool_w[l, g] for g in range(len(POOL_WINDOWS))]).astype(BF16)
        sb = jnp.repeat(sgu_b[l].T, bw // SGU_GROUPS, axis=1)
        xf = _mix_merge(xf, za, out_b, conv_a_w[l], pw, pool_scale[l].reshape(1, bw), sgu_norm_g[l].reshape(1, bw),
                        sgu_w[l], sb, w_branch[l].astype(BF16), w_o[l].astype(BF16), s)

        nj = dff // FF_CHUNK
        w_gu = jnp.concatenate([w_up[l][:, :dff].reshape(d, nj, FF_CHUNK), w_up[l][:, dff:].reshape(d, nj, FF_CHUNK)],
                               axis=2).reshape(d, 2 * dff).astype(BF16)
        act = _ffn_act(xf, norm2_g[l], w_gu, conv_ff_w[l], s, tm_wide)
        xf = _ffn_down(xf, act, w_down[l].astype(BF16), tm)
    return xf.reshape(b, s, d)
```

```python
import functools

import jax
import jax.numpy as jnp
import numpy as np
from jax import lax
from jax.experimental import pallas as pl
from jax.experimental.pallas import tpu as pltpu

F32 = jnp.float32
BF16 = jnp.bfloat16

HEAD_DIM = 64
NSA_HEADS = 8
NSA_GROUPS = 2
NSA_REP = NSA_HEADS // NSA_GROUPS
CMP_BLOCK = 32
CMP_STRIDE = 16
SLC_BLOCK = 64
SLC_TOPN = 16
WIN = 512
NEG = -1e30
FORCE = 1e6
RMS_EPS = 1e-6
LOG2E = 1.4426950408889634
POOL_WINDOWS = (2, 4, 8, 16)
SGU_CHUNK = 128
SGU_GROUPS = 4
BRANCH_WIDTH = 512
N_BRANCH = 4

LANES = 128
HALO = 16
QT = 128
SEL_CHUNK = 256
WIN_CHUNK = 128
FF_CHUNK = 256
RANK_STEP = 16
VROWS = HEAD_DIM + 16
BF16_ROWS = 16
VMEM_LIMIT = 56 * 1024 * 1024


def _params(n_axes):
    return pltpu.CompilerParams(dimension_semantics=("arbitrary",) * n_axes, vmem_limit_bytes=VMEM_LIMIT)


def _dot(a, b):
    return jnp.dot(a, b, preferred_element_type=F32)


def _rms_bf16(x, g):
    ms = jnp.mean(x * x, axis=-1, keepdims=True)
    return ((x * lax.rsqrt(ms + RMS_EPS)) * g).astype(BF16)


def _erf_gelu(x):
    return 0.5 * x * (1.0 + lax.erf(x * (2.0 ** -0.5)))


def _norm_matmul_kernel(x_ref, g_ref, w_ref, o_ref, h_ref):
    @pl.when(pl.program_id(1) == 0)
    def _():
        h_ref[...] = _rms_bf16(x_ref[...], g_ref[...])

    o_ref[...] = _dot(h_ref[...], w_ref[...]).astype(o_ref.dtype)


def _norm_matmul(x, g, w, out_dtype, tm, tn, name):
    t, d = x.shape
    n = w.shape[1]
    return pl.pallas_call(
        _norm_matmul_kernel,
        out_shape=jax.ShapeDtypeStruct((t, n), out_dtype),
        grid=(t // tm, n // tn),
        in_specs=[
            pl.BlockSpec((tm, d), lambda i, j: (i, 0)),
            pl.BlockSpec((1, d), lambda i, j: (0, 0)),
            pl.BlockSpec((d, tn), lambda i, j: (0, j)),
        ],
        out_specs=pl.BlockSpec((tm, tn), lambda i, j: (i, j)),
        scratch_shapes=[pltpu.VMEM((tm, d), BF16)],
        compiler_params=_params(2),
        name=name,
    )(x, g.reshape(1, d), w)


def _group_rms(x, g):
    lane = lax.broadcasted_iota(jnp.int32, x.shape, 1)
    lo = lane < HEAD_DIM
    x2 = x * x
    s_lo = jnp.sum(jnp.where(lo, x2, 0.0), axis=-1, keepdims=True)
    s_hi = jnp.sum(jnp.where(lo, 0.0, x2), axis=-1, keepdims=True)
    ms = jnp.where(lo, s_lo, s_hi) * (1.0 / HEAD_DIM)
    return (x * lax.rsqrt(ms + RMS_EPS)) * g


def _kv_prep_kernel(ks_ref, vs_ref, kw_ref, vw_ref, gs_ref, gw_ref, ksa_ref, vst_ref, kwn_ref, vwt_ref, *, ts):
    j = pl.program_id(1)
    ksn = _group_rms(ks_ref[...], gs_ref[...]).astype(BF16)
    row = lax.broadcasted_iota(jnp.int32, (ts, LANES), 0) + j * ts
    col = lax.broadcasted_iota(jnp.int32, (ts, LANES), 1)
    onehot = jnp.where(row // SLC_BLOCK == col, 1.0, 0.0).astype(BF16)
    ksa_ref[0, :, 0:LANES] = ksn
    ksa_ref[0, :, LANES:2 * LANES] = onehot
    kwn_ref[0] = _group_rms(kw_ref[...], gw_ref[...]).astype(BF16)
    vs = vs_ref[...]
    for c in range(ts // SEL_CHUNK):
        vst_ref[0, c] = _values_t(vs[c * SEL_CHUNK:(c + 1) * SEL_CHUNK, :])
    vw = vw_ref[...]
    for c in range(ts // WIN_CHUNK):
        vwt_ref[0, c] = _values_t(vw[c * WIN_CHUNK:(c + 1) * WIN_CHUNK, :])


def _values_t(v):
    vt = v.T
    keys = vt.shape[1]
    ones_row = jnp.where(lax.broadcasted_iota(jnp.int32, (VROWS - HEAD_DIM, keys), 0) == 0, 1.0, 0.0)
    parts = []
    for g in range(NSA_GROUPS):
        parts += [vt[g * HEAD_DIM:(g + 1) * HEAD_DIM, :], ones_row]
    return jnp.concatenate(parts, axis=0).astype(BF16)


def _kv_prep(znsa, gs, gw, b, s, ts=2048):
    ts = min(ts, s)
    nt = s // ts
    col = lambda c: pl.BlockSpec((ts, LANES), lambda bi, j, c=c: (bi * nt + j, c))
    vec = pl.BlockSpec((1, LANES), lambda bi, j: (0, 0))
    return pl.pallas_call(
        functools.partial(_kv_prep_kernel, ts=ts),
        out_shape=(
            jax.ShapeDtypeStruct((b, s, 2 * LANES), BF16),
            jax.ShapeDtypeStruct((b, s // SEL_CHUNK, NSA_GROUPS * VROWS, SEL_CHUNK), BF16),
            jax.ShapeDtypeStruct((b, s, LANES), BF16),
            jax.ShapeDtypeStruct((b, s // WIN_CHUNK, NSA_GROUPS * VROWS, WIN_CHUNK), BF16),
        ),
        grid=(b, nt),
        in_specs=[col(6), col(7), col(8), col(9), vec, vec],
        out_specs=(
            pl.BlockSpec((1, ts, 2 * LANES), lambda bi, j: (bi, j, 0)),
            pl.BlockSpec((1, ts // SEL_CHUNK, NSA_GROUPS * VROWS, SEL_CHUNK), lambda bi, j: (bi, j, 0, 0)),
            pl.BlockSpec((1, ts, LANES), lambda bi, j: (bi, j, 0)),
            pl.BlockSpec((1, ts // WIN_CHUNK, NSA_GROUPS * VROWS, WIN_CHUNK), lambda bi, j: (bi, j, 0, 0)),
        ),
        compiler_params=_params(2),
        name="kv_prep",
    )(znsa, znsa, znsa, znsa, gs, gw)


def _compress_kernel(kc_ref, vc_ref, pe_ref, w1_ref, w2_ref, g_ref, kco_ref, vct_ref, u_ref, *, nc):
    half = CMP_BLOCK // 2
    for j, src in enumerate((kc_ref, vc_ref)):
        for l in range(half):
            u_ref[:, l * LANES:(l + 1) * LANES] = src[pl.ds(l, nc, stride=CMP_STRIDE), :]
        u = u_ref[...]
        top = _dot((u + pe_ref[j, 0:1, :]).astype(BF16), w1_ref[j, 0])
        bot = _dot((u + pe_ref[j, 1:2, :]).astype(BF16), w1_ref[j, 1])
        pre = top + pltpu.roll(bot, nc - 1, 0)
        hid = pre * jax.nn.sigmoid(pre)
        out = _dot(hid.astype(BF16), w2_ref[j])
        if j == 0:
            kco_ref[0] = _group_rms(out, g_ref[...]).astype(BF16)
        else:
            vct_ref[0] = out.T.astype(BF16)


def _compress(znsa, pe2, w1bd, w2bd, g1, b, s):
    nc = s // CMP_STRIDE
    kw = (CMP_BLOCK // 2) * LANES
    return pl.pallas_call(
        functools.partial(_compress_kernel, nc=nc),
        out_shape=(
            jax.ShapeDtypeStruct((b, nc, LANES), BF16),
            jax.ShapeDtypeStruct((b, LANES, nc), BF16),
        ),
        grid=(b,),
        in_specs=[
            pl.BlockSpec((s, LANES), lambda bi: (bi, 4)),
            pl.BlockSpec((s, LANES), lambda bi: (bi, 5)),
            pl.BlockSpec((2, 2, kw), lambda bi: (0, 0, 0)),
            pl.BlockSpec((2, 2, kw, LANES), lambda bi: (0, 0, 0, 0)),
            pl.BlockSpec((2, LANES, LANES), lambda bi: (0, 0, 0)),
            pl.BlockSpec((1, LANES), lambda bi: (0, 0)),
        ],
        out_specs=(
            pl.BlockSpec((1, nc, LANES), lambda bi: (bi, 0, 0)),
            pl.BlockSpec((1, LANES, nc), lambda bi: (bi, 0, 0)),
        ),
        scratch_shapes=[pltpu.VMEM((nc, kw), F32)],
        compiler_params=_params(1),
        name="compress",
    )(znsa, znsa, pe2, w1bd, w2bd, g1)


def _rank_desc(imp, ns):
    nq = imp.shape[1]
    sub = lax.broadcasted_iota(jnp.int32, (8, nq), 0)
    blocks = [imp[8 * a:8 * a + 8, :] for a in range(ns // 8)]
    rank = [jnp.zeros((8, nq), F32) for _ in blocks]
    for k in range(ns):
        ka, kr = divmod(k, 8)
        rk = jnp.broadcast_to(imp[k:k + 1, :], (8, nq))
        for a, blk in enumerate(blocks):
            if a > ka:
                ahead = rk >= blk
            elif a < ka:
                ahead = rk > blk
            else:
                rank[a] = rank[a] + jnp.where(sub > kr, jnp.where(rk >= blk, 1.0, 0.0), jnp.where(rk > blk, 1.0, 0.0))
                continue
            rank[a] = rank[a] + jnp.where(ahead, 1.0, 0.0)
    return jnp.concatenate(rank, axis=0)


def _nsa_attn_kernel(q_ref, gate_ref, gq_ref, kc_ref, vct_ref, ovt_ref, ksa_ref, vst_ref, kw_ref, vwt_ref,
                     o_ref, qa_ref, sa_ref, sb_ref, m_ref, acc_ref, ocmp_ref, owin_ref, *, ns, nc, topn):
    i = pl.program_id(1)
    start = i * QT
    nl = NSA_HEADS * QT
    gl = NSA_REP * QT

    qt = q_ref[...].T.reshape(NSA_HEADS, HEAD_DIM, QT)
    ms = jnp.mean(qt * qt, axis=1, keepdims=True)
    qn = ((qt * lax.rsqrt(ms + RMS_EPS)) * gq_ref[...][None]) * (HEAD_DIM ** -0.5 * LOG2E)
    qa_ref[...] = jnp.zeros(qa_ref.shape, BF16)
    for h in range(NSA_HEADS):
        g = h // NSA_REP
        qa_ref[g * HEAD_DIM:(g + 1) * HEAD_DIM, h * QT:(h + 1) * QT] = qn[h].astype(BF16)

    def tq(shape):
        return start + (lax.broadcasted_iota(jnp.int32, shape, 1) & (QT - 1))

    def krow(shape, base):
        return base + lax.broadcasted_iota(jnp.int32, shape, 0)

    def group_pv(vt, pb):
        rows = vt.shape[0] // NSA_GROUPS
        return jnp.concatenate(
            [_dot(vt[g * rows:(g + 1) * rows, :], pb[:, g * gl:(g + 1) * gl]) for g in range(NSA_GROUPS)],
            axis=1)

    def compressed_and_select():
        qk = qa_ref[0:LANES, :]
        sc = _dot(kc_ref[0], qk)
        cval = krow((nc, nl), 0) * CMP_STRIDE + (CMP_BLOCK - 1) <= tq((nc, nl))
        sc = jnp.where(cval, sc, NEG)
        e = jnp.exp2(sc - jnp.max(sc, axis=0, keepdims=True))
        pc = e / jnp.sum(e, axis=0, keepdims=True)
        pc = jnp.where(tq((1, nl)) >= CMP_BLOCK - 1, pc, 0.0)
        ocmp_ref[...] = group_pv(vct_ref[0], pc.astype(BF16))

        jb = krow((ns, QT), 0)
        jt = (start + lax.broadcasted_iota(jnp.int32, (ns, QT), 1)) // SLC_BLOCK
        forced = (jb == 0) | (jb == jt) | (jb == jt - 1)
        future = jb > jt
        last_block = (start + QT - 1) // SLC_BLOCK
        n_var = max(ns // RANK_STEP, 1)
        ovt = ovt_ref[...]
        for g in range(NSA_GROUPS):
            psum = pc[:, g * gl:g * gl + QT]
            for r in range(1, NSA_REP):
                psum = psum + pc[:, g * gl + r * QT:g * gl + (r + 1) * QT]
            p_hi = psum.astype(BF16)
            rem = psum - p_hi.astype(F32)
            p_mid = rem.astype(BF16)
            p_lo = (rem - p_mid.astype(F32)).astype(BF16)
            imp = (_dot(ovt, p_hi) + _dot(ovt, p_mid)) + _dot(ovt, p_lo)
            imp = jnp.where(forced, FORCE, jnp.where(future, NEG, imp))
            for v in range(n_var):
                rows = RANK_STEP * (v + 1)

                @pl.when(jnp.minimum(last_block // RANK_STEP, n_var - 1) == v)
                def _(g=g, imp=imp, rows=rows):
                    sel = (_rank_desc(imp[0:rows, :], rows) < topn) & jnp.logical_not(future[0:rows, :])
                    bias = jnp.where(sel, 0.0, NEG)
                    if rows < ns:
                        bias = jnp.concatenate([bias, jnp.full((ns - rows, QT), NEG, F32)], axis=0)
                    for r in range(NSA_REP):
                        h = g * NSA_REP + r
                        qa_ref[LANES:LANES + ns, h * QT:(h + 1) * QT] = bias.astype(BF16)

    def col_max(s):
        part = jnp.max(s.reshape(s.shape[0] // BF16_ROWS, BF16_ROWS, nl), axis=0)
        return jnp.max(part.astype(F32), axis=0, keepdims=True)

    def probs(s, m):
        return jnp.exp2(s - m.astype(BF16))

    def normalised(acc):
        return acc[0:HEAD_DIM, :] / acc[HEAD_DIM:HEAD_DIM + 1, :]

    def update(s, vt, first):
        cm = col_max(s)
        if first:
            m_new = cm
        else:
            m_old = m_ref[...]
            m_new = jnp.maximum(m_old, cm)
        pv = group_pv(vt, probs(s, m_new))
        if first:
            acc_ref[...] = pv
        else:
            acc_ref[...] = jnp.exp2(m_old - m_new) * acc_ref[...] + pv
        m_ref[...] = m_new

    n_back = WIN // WIN_CHUNK
    wshape = (WIN_CHUNK, nl)
    local_q = lax.broadcasted_iota(jnp.int32, wshape, 1) & (QT - 1)
    local_k = lax.broadcasted_iota(jnp.int32, wshape, 0)

    def window():
        ss, vts = [], []
        for c in range(n_back + 1):
            ci = i - n_back + c
            cl = jnp.maximum(ci, 0)
            k = kw_ref[0, pl.ds(pl.multiple_of(cl * WIN_CHUNK, WIN_CHUNK), WIN_CHUNK), :]
            s = _dot(k, qa_ref[0:LANES, :])
            if c == 0:
                s = jnp.where(local_k > local_q + jnp.where(ci >= 0, 0, WIN_CHUNK), s, NEG)
            elif c == n_back:
                s = jnp.where(local_k <= local_q, s, NEG)
            else:
                s = s + jnp.where(ci >= 0, 0.0, NEG)
            ss.append(s.astype(BF16))
            vts.append(vwt_ref[0, cl])
        m = col_max(ss[0])
        for s in ss[1:]:
            m = jnp.maximum(m, col_max(s))
        pb = jnp.concatenate([probs(s, m) for s in ss], axis=0)
        owin_ref[...] = normalised(group_pv(jnp.concatenate(vts, axis=1), pb))

    compressed_and_select()
    window()

    def sel_scores(buf, c, diag=False):
        k = ksa_ref[0, pl.ds(pl.multiple_of(c * SEL_CHUNK, SEL_CHUNK), SEL_CHUNK), :]
        s = _dot(k, qa_ref[...])
        if diag:
            shape = (SEL_CHUNK, nl)
            s = jnp.where(krow(shape, c * SEL_CHUNK) <= tq(shape), s, NEG)
        buf[...] = s.astype(BF16)

    def sel_update(buf, c, first):
        update(buf[...], vst_ref[0, c], first)

    cd = start // SEL_CHUNK
    sel_scores(sa_ref, cd, diag=True)
    sel_scores(sb_ref, 0)
    sel_update(sa_ref, cd, True)

    def sel_pair(p, carry):
        c0 = 2 * p
        sel_scores(sa_ref, c0 + 1)
        sel_update(sb_ref, c0, False)
        sel_scores(sb_ref, jnp.minimum(c0 + 2, cd - 1))
        sel_update(sa_ref, c0 + 1, False)
        return carry

    lax.fori_loop(0, cd // 2, sel_pair, 0)

    @pl.when(cd % 2 == 1)
    def _():
        sel_update(sb_ref, cd - 1, False)

    o_sel = normalised(acc_ref[...])
    o_cmp = ocmp_ref[...]
    o_win = owin_ref[...]

    sg = jax.nn.sigmoid(gate_ref[...].T[0:3 * NSA_HEADS, :])
    outs = []
    for h in range(NSA_HEADS):
        lanes = slice(h * QT, (h + 1) * QT)
        outs.append(sg[3 * h:3 * h + 1, :] * o_cmp[:, lanes] + sg[3 * h + 1:3 * h + 2, :] * o_sel[:, lanes]
                    + sg[3 * h + 2:3 * h + 3, :] * o_win[:, lanes])
    o_ref[...] = jnp.concatenate(outs, axis=0).T.astype(o_ref.dtype)


def _nsa_attn(znsa, gq, kc, vct, ovt, ksa, vst, kwn, vwt, b, s):
    ns = s // SLC_BLOCK
    nc = s // CMP_STRIDE
    nq = s // QT
    nl = NSA_HEADS * QT
    width = NSA_HEADS * HEAD_DIM
    kern = functools.partial(_nsa_attn_kernel, ns=ns, nc=nc, topn=min(SLC_TOPN, ns))
    return pl.pallas_call(
        kern,
        out_shape=jax.ShapeDtypeStruct((b * s, width), BF16),
        grid=(b, nq),
        in_specs=[
            pl.BlockSpec((QT, width), lambda bi, i: (bi * nq + i, 0)),
            pl.BlockSpec((QT, LANES), lambda bi, i: (bi * nq + i, 10)),
            pl.BlockSpec((HEAD_DIM, QT), lambda bi, i: (0, 0)),
            pl.BlockSpec((1, nc, LANES), lambda bi, i: (bi, 0, 0)),
            pl.BlockSpec((1, LANES, nc), lambda bi, i: (bi, 0, 0)),
            pl.BlockSpec((ns, nc), lambda bi, i: (0, 0)),
            pl.BlockSpec((1, s, 2 * LANES), lambda bi, i: (bi, 0, 0)),
            pl.BlockSpec((1, s // SEL_CHUNK, NSA_GROUPS * VROWS, SEL_CHUNK), lambda bi, i: (bi, 0, 0, 0)),
            pl.BlockSpec((1, s, LANES), lambda bi, i: (bi, 0, 0)),
            pl.BlockSpec((1, s // WIN_CHUNK, NSA_GROUPS * VROWS, WIN_CHUNK), lambda bi, i: (bi, 0, 0, 0)),
        ],
        out_specs=pl.BlockSpec((QT, width), lambda bi, i: (bi * nq + i, 0)),
        scratch_shapes=[
            pltpu.VMEM((2 * LANES, nl), BF16),
            pltpu.VMEM((SEL_CHUNK, nl), BF16),
            pltpu.VMEM((SEL_CHUNK, nl), BF16),
            pltpu.VMEM((1, nl), F32),
            pltpu.VMEM((VROWS, nl), F32),
            pltpu.VMEM((HEAD_DIM, nl), F32),
            pltpu.VMEM((HEAD_DIM, nl), F32),
        ],
        compiler_params=_params(2),
        name="nsa_attn",
    )(znsa, znsa, gq, kc, vct, ovt, ksa, vst, kwn, vwt)


def _mix_kernel(x_ref, ab_ref, ac_ref, ax_ref, ach_ref, axh_ref, p_ref, ph_ref, sgu_ref, mg0_ref, mg1_ref, mg2_ref,
                mg3_ref, ob_ref, cw_ref, pw_ref, psc_ref, sng_ref, sw_ref, sb_ref, wbr_ref, wo_ref, o_ref, ext_ref,
                ext2_ref, *, tm, tiles_per_seq):
    ti = pl.program_id(0) % tiles_per_seq
    keep = jnp.where(ti == 0, 0.0, 1.0)

    ext_ref[0:HALO, :] = (ach_ref[...].astype(F32) * axh_ref[...].astype(F32)) * keep
    ext_ref[HALO:, :] = ac_ref[...].astype(F32) * ax_ref[...].astype(F32)
    e = ext_ref[...]
    cw = cw_ref[...]
    conv = (cw[0:1, :] * pltpu.roll(e, 2, 0) + cw[1:2, :] * pltpu.roll(e, 1, 0)) + cw[2:3, :] * e
    out_a = ab_ref[...].astype(F32) * conv[HALO:, :]

    p = p_ref[...].astype(F32)
    ext2_ref[0:HALO, :] = ph_ref[...].astype(F32) * keep
    ext2_ref[HALO:, :] = p
    e = ext2_ref[...]
    gw = BRANCH_WIDTH // len(POOL_WINDOWS)
    cnt = (ti * tm + 1 + lax.broadcasted_iota(jnp.int32, (tm, gw), 0)).astype(F32)
    groups = []
    for gi, w in enumerate(POOL_WINDOWS):
        acc = e[:, gi * gw:(gi + 1) * gw]
        span = 1
        while span < w:
            acc = acc + pltpu.roll(acc, span, 0)
            span *= 2
        groups.append(acc[HALO:, :] / jnp.minimum(cnt, float(w)))
    pooled = jnp.concatenate(groups, axis=1) - p
    out_c = _dot(pooled.astype(BF16), pw_ref[...]) * psc_ref[...]

    z = _erf_gelu(sgu_ref[...].astype(F32))
    u = z[:, :BRANCH_WIDTH]
    v = z[:, BRANCH_WIDTH:]
    v = (v * lax.rsqrt(jnp.mean(v * v, axis=-1, keepdims=True) + RMS_EPS)) * sng_ref[...]
    vb = v.astype(BF16)
    nchunk = tm // SGU_CHUNK
    gd = BRANCH_WIDTH // SGU_GROUPS
    tri = (lax.broadcasted_iota(jnp.int32, (SGU_CHUNK, SGU_CHUNK), 0)
           >= lax.broadcasted_iota(jnp.int32, (SGU_CHUNK, SGU_CHUNK), 1))
    mixed_g = []
    for g in range(SGU_GROUPS):
        wg = jnp.where(tri, sw_ref[g], 0.0).astype(BF16)
        rhs = jnp.concatenate(
            [vb[c * SGU_CHUNK:(c + 1) * SGU_CHUNK, g * gd:(g + 1) * gd] for c in range(nchunk)], axis=1)
        res = _dot(wg, rhs)
        mixed_g.append(jnp.concatenate([res[:, c * gd:(c + 1) * gd] for c in range(nchunk)], axis=0))
    bias = jnp.concatenate([sb_ref[...]] * nchunk, axis=0)
    out_d = u * (jnp.concatenate(mixed_g, axis=1) + bias)

    merged = jax.nn.sigmoid(mg0_ref[...].astype(F32)) * _dot(out_a.astype(BF16), wbr_ref[0])
    merged = merged + jax.nn.sigmoid(mg1_ref[...].astype(F32)) * _dot(ob_ref[...], wbr_ref[1])
    merged = merged + jax.nn.sigmoid(mg2_ref[...].astype(F32)) * _dot(out_c.astype(BF16), wbr_ref[2])
    merged = merged + jax.nn.sigmoid(mg3_ref[...].astype(F32)) * _dot(out_d.astype(BF16), wbr_ref[3])
    o_ref[...] = x_ref[...] + _dot(merged.astype(BF16), wo_ref[...])


def _mix_merge(x, za, out_b, cw, pw, psc, sng, sw, sb, wbr, wo, s, tm=256):
    t, d = x.shape
    bw = BRANCH_WIDTH
    tps = s // tm
    hb = tm // HALO
    tile = lambda c: pl.BlockSpec((tm, bw), lambda i, c=c: (i, c))
    halo = lambda c: pl.BlockSpec((HALO, bw), lambda i, c=c: (jnp.maximum(i * hb - 1, 0), c))
    wide = lambda c: pl.BlockSpec((tm, d), lambda i, c=c: (i, c))
    full = lambda a: pl.BlockSpec(a.shape, lambda i, n=a.ndim: (0,) * n)
    consts = (cw, pw, psc, sng, sw, sb, wbr, wo)
    return pl.pallas_call(
        functools.partial(_mix_kernel, tm=tm, tiles_per_seq=tps),
        out_shape=jax.ShapeDtypeStruct((t, d), F32),
        grid=(t // tm,),
        in_specs=[wide(0), tile(0), tile(1), tile(2), halo(1), halo(2), tile(3), halo(3), wide(2),
                  wide(3), wide(4), wide(5), wide(6), tile(0)] + [full(a) for a in consts],
        out_specs=wide(0),
        scratch_shapes=[pltpu.VMEM((tm + HALO, bw), F32), pltpu.VMEM((tm + HALO, bw), F32)],
        compiler_params=_params(1),
        name="mix_merge",
    )(x, za, za, za, za, za, za, za, za, za, za, za, za, out_b, *consts)


def _ffn_act_kernel(x_ref, xh_ref, g_ref, w_ref, cw_ref, o_ref, h_ref, hh_ref, ext_ref, *, tiles_per_seq, rb):
    @pl.when(pl.program_id(1) == 0)
    def _():
        h_ref[...] = _rms_bf16(x_ref[...], g_ref[...])
        hh_ref[...] = _rms_bf16(xh_ref[...], g_ref[...])

    keep = jnp.where(pl.program_id(0) % tiles_per_seq == 0, 0.0, 1.0)
    w = w_ref[...]
    cw = cw_ref[...]
    ext_ref[0:HALO, :] = _dot(hh_ref[...], w[:, :FF_CHUNK]) * keep
    for r0 in range(0, h_ref.shape[0], rb):
        gu = _dot(h_ref[r0:r0 + rb, :], w)
        ext_ref[HALO + r0:HALO + r0 + rb, :] = gu[:, :FF_CHUNK]
        e = ext_ref[r0:r0 + rb + HALO, :]
        conv = ((cw[0:1, :] * pltpu.roll(e, 2, 0) + cw[1:2, :] * pltpu.roll(e, 1, 0)) + cw[2:3, :] * e)[HALO:, :]
        o_ref[r0:r0 + rb, :] = ((conv * jax.nn.sigmoid(conv)) * gu[:, FF_CHUNK:]).astype(o_ref.dtype)


def _ffn_act(x, g, w_gu, cw, s, tm):
    t, d = x.shape
    dff = cw.shape[1]
    hb = tm // HALO
    return pl.pallas_call(
        functools.partial(_ffn_act_kernel, tiles_per_seq=s // tm, rb=min(tm, 512)),
        out_shape=jax.ShapeDtypeStruct((t, dff), BF16),
        grid=(t // tm, dff // FF_CHUNK),
        in_specs=[
            pl.BlockSpec((tm, d), lambda i, j: (i, 0)),
            pl.BlockSpec((HALO, d), lambda i, j: (jnp.maximum(i * hb - 1, 0), 0)),
            pl.BlockSpec((1, d), lambda i, j: (0, 0)),
            pl.BlockSpec((d, 2 * FF_CHUNK), lambda i, j: (0, j)),
            pl.BlockSpec((cw.shape[0], FF_CHUNK), lambda i, j: (0, j)),
        ],
        out_specs=pl.BlockSpec((tm, FF_CHUNK), lambda i, j: (i, j)),
        scratch_shapes=[pltpu.VMEM((tm, d), BF16), pltpu.VMEM((HALO, d), BF16), pltpu.VMEM((tm + HALO, FF_CHUNK), F32)],
        compiler_params=_params(2),
        name="ffn_act",
    )(x, x, g.reshape(1, d), w_gu, cw)


def _ffn_down_kernel(x_ref, a_ref, wd_ref, o_ref):
    o_ref[...] = x_ref[...] + _dot(a_ref[...], wd_ref[...])


def _ffn_down(x, act, wd, tm):
    t, d = x.shape
    dff = wd.shape[0]
    return pl.pallas_call(
        _ffn_down_kernel,
        out_shape=jax.ShapeDtypeStruct((t, d), F32),
        grid=(t // tm,),
        in_specs=[
            pl.BlockSpec((tm, d), lambda i: (i, 0)),
            pl.BlockSpec((tm, dff), lambda i: (i, 0)),
            pl.BlockSpec((dff, d), lambda i: (0, 0)),
        ],
        out_specs=pl.BlockSpec((tm, d), lambda i: (i, 0)),
        compiler_params=_params(1),
        name="ffn_down",
    )(x, act, wd)


def _block_diag(blocks):
    n = len(blocks)
    r, c = blocks[0].shape
    out = jnp.zeros((n * r, n * c), blocks[0].dtype)
    for k, blk in enumerate(blocks):
        out = out.at[k * r:(k + 1) * r, k * c:(k + 1) * c].set(blk)
    return out


def _overlap_t(nc, ns):
    n_cmp = nc - 1
    cs = np.arange(n_cmp) * CMP_STRIDE
    ce = cs + CMP_BLOCK
    ss = np.arange(ns) * SLC_BLOCK
    se = ss + SLC_BLOCK
    ov = np.clip(np.minimum(ce[:, None], se[None]) - np.maximum(cs[:, None], ss[None]), 0, None)
    out = np.zeros((ns, nc), np.float32)
    out[:, :n_cmp] = (ov.astype(np.float32) / CMP_BLOCK).T
    return out


def _compress_params(pe, w1, w2):
    half = CMP_BLOCK // 2
    pe2 = jnp.tile(pe.reshape(2, 2, half, 1, HEAD_DIM), (1, 1, 1, NSA_GROUPS, 1)).reshape(2, 2, half * LANES)
    w1r = w1.reshape(2, 2, half, HEAD_DIM, HEAD_DIM)
    eye = jnp.eye(NSA_GROUPS, dtype=w1.dtype)
    w1bd = jnp.einsum("jpldo,gh->jplgdho", w1r, eye).reshape(2, 2, half * LANES, LANES)
    w2bd = jnp.einsum("jdo,gh->jgdho", w2, eye).reshape(2, LANES, LANES)
    return pe2, w1bd.astype(BF16), w2bd.astype(BF16)


def kernel(x, norm1_g, w_in, conv_a_w, qk_norm_g, cmp_pe, cmp_w1, cmp_w2, pool_w, pool_scale, sgu_norm_g, sgu_w,
           sgu_b, w_branch, w_o, norm2_g, w_up, conv_ff_w, w_down):
    b, s, d = x.shape
    depth = w_in.shape[0]
    t = b * s
    bw = BRANCH_WIDTH
    kvw = 6 * NSA_GROUPS * HEAD_DIM
    ngate = 3 * NSA_HEADS
    o_q = 3 * bw
    o_kv = o_q + bw
    o_gate = o_kv + kvw
    o_pool = o_gate + ngate
    dff = w_down.shape[1]
    ns = s // SLC_BLOCK
    nc = s // CMP_STRIDE
    ovt = jnp.asarray(_overlap_t(nc, ns), BF16)
    xf = x.reshape(t, d)
    tm = min(1024, t)
    tm_wide = min(2048, t)
    for l in range(depth):
        w = w_in[l]
        w_a = jnp.concatenate([w[:, :o_q], w[:, o_pool:]], axis=1).astype(BF16)
        w_n = jnp.concatenate([w[:, o_q:o_pool], jnp.zeros((d, LANES - ngate), w.dtype)], axis=1).astype(BF16)
        za = _norm_matmul(xf, norm1_g[l], w_a, BF16, tm_wide, 2 * bw, "in_proj_a")
        znsa = _norm_matmul(xf, norm1_g[l], w_n, F32, tm, w_n.shape[1], "in_proj_nsa")

        g2 = lambda v: jnp.tile(v, NSA_GROUPS).reshape(1, LANES)
        ksa, vst, kwn, vwt = _kv_prep(znsa, g2(qk_norm_g[l, 2]), g2(qk_norm_g[l, 3]), b, s)
        pe2, w1bd, w2bd = _compress_params(cmp_pe[l], cmp_w1[l], cmp_w2[l])
        kc, vct = _compress(znsa, pe2, w1bd, w2bd, g2(qk_norm_g[l, 1]), b, s)
        gq = jnp.broadcast_to(qk_norm_g[l, 0][:, None], (HEAD_DIM, QT))
        out_b = _nsa_attn(znsa, gq, kc, vct, ovt, ksa, vst, kwn, vwt, b, s)

        pw = _block_diag([pool_w[l, g] for g in range(len(POOL_WINDOWS))]).astype(BF16)
        sb = jnp.repeat(sgu_b[l].T, bw // SGU_GROUPS, axis=1)
        xf = _mix_merge(xf, za, out_b, conv_a_w[l], pw, pool_scale[l].reshape(1, bw), sgu_norm_g[l].reshape(1, bw),
                        sgu_w[l], sb, w_branch[l].astype(BF16), w_o[l].astype(BF16), s)

        nj = dff // FF_CHUNK
        w_gu = jnp.concatenate([w_up[l][:, :dff].reshape(d, nj, FF_CHUNK), w_up[l][:, dff:].reshape(d, nj, FF_CHUNK)],
                               axis=2).reshape(d, 2 * dff).astype(BF16)
        act = _ffn_act(xf, norm2_g[l], w_gu, conv_ff_w[l], s, tm_wide)
        xf = _ffn_down(xf, act, w_down[l].astype(BF16), tm)
    return xf.reshape(b, s, d)
```

```python
import functools

import jax
import jax.numpy as jnp
import numpy as np
from jax import lax
from jax.experimental import pallas as pl
from jax.experimental.pallas import tpu as pltpu

F32 = jnp.float32
BF16 = jnp.bfloat16

HEAD_DIM = 64
NSA_HEADS = 8
NSA_GROUPS = 2
NSA_REP = NSA_HEADS // NSA_GROUPS
CMP_BLOCK = 32
CMP_STRIDE = 16
SLC_BLOCK = 64
SLC_TOPN = 16
WIN = 512
NEG = -1e30
FORCE = 1e6
RMS_EPS = 1e-6
LOG2E = 1.4426950408889634
POOL_WINDOWS = (2, 4, 8, 16)
SGU_CHUNK = 128
SGU_GROUPS = 4
BRANCH_WIDTH = 512
N_BRANCH = 4

LANES = 128
HALO = 16
QT = 128
SEL_CHUNK = 256
WIN_CHUNK = 128
FF_CHUNK = 256
RANK_STEP = 16
VROWS = HEAD_DIM + 16
BF16_ROWS = 16
VMEM_LIMIT = 56 * 1024 * 1024


def _params(n_axes):
    return pltpu.CompilerParams(dimension_semantics=("arbitrary",) * n_axes, vmem_limit_bytes=VMEM_LIMIT)


def _dot(a, b):
    return jnp.dot(a, b, preferred_element_type=F32)


def _layer_spec(arr, l, block=None, index=None):
    block = tuple(arr.shape[1:]) if block is None else tuple(block)
    index = index or (lambda *_: (0,) * len(block))
    return pl.BlockSpec((None,) + block, lambda *grid: (l,) + tuple(index(*grid)))


def _rms_bf16(x, g):
    ms = jnp.mean(x * x, axis=-1, keepdims=True)
    return ((x * lax.rsqrt(ms + RMS_EPS)) * g).astype(BF16)


def _erf_gelu(x):
    return 0.5 * x * (1.0 + lax.erf(x * (2.0 ** -0.5)))


def _norm_matmul_kernel(x_ref, g_ref, w_ref, o_ref, h_ref):
    @pl.when(pl.program_id(1) == 0)
    def _():
        h_ref[...] = _rms_bf16(x_ref[...], g_ref[...])

    o_ref[...] = _dot(h_ref[...], w_ref[...]).astype(o_ref.dtype)


def _norm_matmul(x, g, w, l, out_dtype, tm, tn, name):
    t, d = x.shape
    n = w.shape[2]
    return pl.pallas_call(
        _norm_matmul_kernel,
        out_shape=jax.ShapeDtypeStruct((t, n), out_dtype),
        grid=(t // tm, n // tn),
        in_specs=[
            pl.BlockSpec((tm, d), lambda i, j: (i, 0)),
            _layer_spec(g, l),
            _layer_spec(w, l, (d, tn), lambda i, j: (0, j)),
        ],
        out_specs=pl.BlockSpec((tm, tn), lambda i, j: (i, j)),
        scratch_shapes=[pltpu.VMEM((tm, d), BF16)],
        compiler_params=_params(2),
        name=name,
    )(x, g, w)


def _group_rms(x, g):
    lane = lax.broadcasted_iota(jnp.int32, x.shape, 1)
    lo = lane < HEAD_DIM
    x2 = x * x
    s_lo = jnp.sum(jnp.where(lo, x2, 0.0), axis=-1, keepdims=True)
    s_hi = jnp.sum(jnp.where(lo, 0.0, x2), axis=-1, keepdims=True)
    ms = jnp.where(lo, s_lo, s_hi) * (1.0 / HEAD_DIM)
    return (x * lax.rsqrt(ms + RMS_EPS)) * g


def _kv_prep_kernel(ks_ref, vs_ref, kw_ref, vw_ref, gs_ref, gw_ref, ksa_ref, vst_ref, kwn_ref, vwt_ref, *, ts):
    j = pl.program_id(1)
    ksn = _group_rms(ks_ref[...], gs_ref[...]).astype(BF16)
    row = lax.broadcasted_iota(jnp.int32, (ts, LANES), 0) + j * ts
    col = lax.broadcasted_iota(jnp.int32, (ts, LANES), 1)
    onehot = jnp.where(row // SLC_BLOCK == col, 1.0, 0.0).astype(BF16)
    ksa_ref[0, :, 0:LANES] = ksn
    ksa_ref[0, :, LANES:2 * LANES] = onehot
    kwn_ref[0] = _group_rms(kw_ref[...], gw_ref[...]).astype(BF16)
    vs = vs_ref[...]
    for c in range(ts // SEL_CHUNK):
        vst_ref[0, c] = _values_t(vs[c * SEL_CHUNK:(c + 1) * SEL_CHUNK, :])
    vw = vw_ref[...]
    for c in range(ts // WIN_CHUNK):
        vwt_ref[0, c] = _values_t(vw[c * WIN_CHUNK:(c + 1) * WIN_CHUNK, :])


def _values_t(v):
    vt = v.T
    keys = vt.shape[1]
    ones_row = jnp.where(lax.broadcasted_iota(jnp.int32, (VROWS - HEAD_DIM, keys), 0) == 0, 1.0, 0.0)
    parts = []
    for g in range(NSA_GROUPS):
        parts += [vt[g * HEAD_DIM:(g + 1) * HEAD_DIM, :], ones_row]
    return jnp.concatenate(parts, axis=0).astype(BF16)


def _kv_prep(znsa, gs, gw, l, b, s, ts=2048):
    ts = min(ts, s)
    nt = s // ts
    col = lambda c: pl.BlockSpec((ts, LANES), lambda bi, j, c=c: (bi * nt + j, c))
    vec = _layer_spec(gs, l)
    return pl.pallas_call(
        functools.partial(_kv_prep_kernel, ts=ts),
        out_shape=(
            jax.ShapeDtypeStruct((b, s, 2 * LANES), BF16),
            jax.ShapeDtypeStruct((b, s // SEL_CHUNK, NSA_GROUPS * VROWS, SEL_CHUNK), BF16),
            jax.ShapeDtypeStruct((b, s, LANES), BF16),
            jax.ShapeDtypeStruct((b, s // WIN_CHUNK, NSA_GROUPS * VROWS, WIN_CHUNK), BF16),
        ),
        grid=(b, nt),
        in_specs=[col(6), col(7), col(8), col(9), vec, vec],
        out_specs=(
            pl.BlockSpec((1, ts, 2 * LANES), lambda bi, j: (bi, j, 0)),
            pl.BlockSpec((1, ts // SEL_CHUNK, NSA_GROUPS * VROWS, SEL_CHUNK), lambda bi, j: (bi, j, 0, 0)),
            pl.BlockSpec((1, ts, LANES), lambda bi, j: (bi, j, 0)),
            pl.BlockSpec((1, ts // WIN_CHUNK, NSA_GROUPS * VROWS, WIN_CHUNK), lambda bi, j: (bi, j, 0, 0)),
        ),
        compiler_params=_params(2),
        name="kv_prep",
    )(znsa, znsa, znsa, znsa, gs, gw)


def _compress_kernel(kc_ref, vc_ref, pe_ref, w1_ref, w2_ref, g_ref, kco_ref, vct_ref, u_ref, *, nc):
    half = CMP_BLOCK // 2
    for j, src in enumerate((kc_ref, vc_ref)):
        for l in range(half):
            u_ref[:, l * LANES:(l + 1) * LANES] = src[pl.ds(l, nc, stride=CMP_STRIDE), :]
        u = u_ref[...]
        top = _dot((u + pe_ref[j, 0:1, :]).astype(BF16), w1_ref[j, 0])
        bot = _dot((u + pe_ref[j, 1:2, :]).astype(BF16), w1_ref[j, 1])
        pre = top + pltpu.roll(bot, nc - 1, 0)
        hid = pre * jax.nn.sigmoid(pre)
        out = _dot(hid.astype(BF16), w2_ref[j])
        if j == 0:
            kco_ref[0] = _group_rms(out, g_ref[...]).astype(BF16)
        else:
            vct_ref[0] = out.T.astype(BF16)


def _compress(znsa, pe2, w1bd, w2bd, g1, l, b, s):
    nc = s // CMP_STRIDE
    kw = (CMP_BLOCK // 2) * LANES
    return pl.pallas_call(
        functools.partial(_compress_kernel, nc=nc),
        out_shape=(
            jax.ShapeDtypeStruct((b, nc, LANES), BF16),
            jax.ShapeDtypeStruct((b, LANES, nc), BF16),
        ),
        grid=(b,),
        in_specs=[
            pl.BlockSpec((s, LANES), lambda bi: (bi, 4)),
            pl.BlockSpec((s, LANES), lambda bi: (bi, 5)),
            _layer_spec(pe2, l),
            _layer_spec(w1bd, l),
            _layer_spec(w2bd, l),
            _layer_spec(g1, l),
        ],
        out_specs=(
            pl.BlockSpec((1, nc, LANES), lambda bi: (bi, 0, 0)),
            pl.BlockSpec((1, LANES, nc), lambda bi: (bi, 0, 0)),
        ),
        scratch_shapes=[pltpu.VMEM((nc, kw), F32)],
        compiler_params=_params(1),
        name="compress",
    )(znsa, znsa, pe2, w1bd, w2bd, g1)


def _rank_desc(imp, ns):
    nq = imp.shape[1]
    sub = lax.broadcasted_iota(jnp.int32, (8, nq), 0)
    blocks = [imp[8 * a:8 * a + 8, :] for a in range(ns // 8)]
    rank = [jnp.zeros((8, nq), F32) for _ in blocks]
    for k in range(ns):
        ka, kr = divmod(k, 8)
        rk = jnp.broadcast_to(imp[k:k + 1, :], (8, nq))
        for a, blk in enumerate(blocks):
            if a > ka:
                ahead = rk >= blk
            elif a < ka:
                ahead = rk > blk
            else:
                rank[a] = rank[a] + jnp.where(sub > kr, jnp.where(rk >= blk, 1.0, 0.0), jnp.where(rk > blk, 1.0, 0.0))
                continue
            rank[a] = rank[a] + jnp.where(ahead, 1.0, 0.0)
    return jnp.concatenate(rank, axis=0)


def _nsa_attn_kernel(q_ref, gate_ref, gq_ref, kc_ref, vct_ref, ovt_ref, ksa_ref, vst_ref, kw_ref, vwt_ref,
                     o_ref, qa_ref, sa_ref, sb_ref, m_ref, acc_ref, ocmp_ref, owin_ref, *, ns, nc, topn):
    i = pl.program_id(1)
    start = i * QT
    nl = NSA_HEADS * QT
    gl = NSA_REP * QT

    qt = q_ref[...].T.reshape(NSA_HEADS, HEAD_DIM, QT)
    ms = jnp.mean(qt * qt, axis=1, keepdims=True)
    qn = ((qt * lax.rsqrt(ms + RMS_EPS)) * gq_ref[...][None]) * (HEAD_DIM ** -0.5 * LOG2E)
    qa_ref[...] = jnp.zeros(qa_ref.shape, BF16)
    for h in range(NSA_HEADS):
        g = h // NSA_REP
        qa_ref[g * HEAD_DIM:(g + 1) * HEAD_DIM, h * QT:(h + 1) * QT] = qn[h].astype(BF16)

    def tq(shape):
        return start + (lax.broadcasted_iota(jnp.int32, shape, 1) & (QT - 1))

    def krow(shape, base):
        return base + lax.broadcasted_iota(jnp.int32, shape, 0)

    def group_pv(vt, pb):
        rows = vt.shape[0] // NSA_GROUPS
        return jnp.concatenate(
            [_dot(vt[g * rows:(g + 1) * rows, :], pb[:, g * gl:(g + 1) * gl]) for g in range(NSA_GROUPS)],
            axis=1)

    def compressed_and_select():
        qk = qa_ref[0:LANES, :]
        sc = _dot(kc_ref[0], qk)
        cval = krow((nc, nl), 0) * CMP_STRIDE + (CMP_BLOCK - 1) <= tq((nc, nl))
        sc = jnp.where(cval, sc, NEG)
        e = jnp.exp2(sc - jnp.max(sc, axis=0, keepdims=True))
        pc = e / jnp.sum(e, axis=0, keepdims=True)
        pc = jnp.where(tq((1, nl)) >= CMP_BLOCK - 1, pc, 0.0)
        ocmp_ref[...] = group_pv(vct_ref[0], pc.astype(BF16))

        jb = krow((ns, QT), 0)
        jt = (start + lax.broadcasted_iota(jnp.int32, (ns, QT), 1)) // SLC_BLOCK
        forced = (jb == 0) | (jb == jt) | (jb == jt - 1)
        future = jb > jt
        last_block = (start + QT - 1) // SLC_BLOCK
        n_var = max(ns // RANK_STEP, 1)
        ovt = ovt_ref[...]
        for g in range(NSA_GROUPS):
            psum = pc[:, g * gl:g * gl + QT]
            for r in range(1, NSA_REP):
                psum = psum + pc[:, g * gl + r * QT:g * gl + (r + 1) * QT]
            p_hi = psum.astype(BF16)
            rem = psum - p_hi.astype(F32)
            p_mid = rem.astype(BF16)
            p_lo = (rem - p_mid.astype(F32)).astype(BF16)
            imp = (_dot(ovt, p_hi) + _dot(ovt, p_mid)) + _dot(ovt, p_lo)
            imp = jnp.where(forced, FORCE, jnp.where(future, NEG, imp))
            for v in range(n_var):
                rows = RANK_STEP * (v + 1)

                @pl.when(jnp.minimum(last_block // RANK_STEP, n_var - 1) == v)
                def _(g=g, imp=imp, rows=rows):
                    sel = (_rank_desc(imp[0:rows, :], rows) < topn) & jnp.logical_not(future[0:rows, :])
                    bias = jnp.where(sel, 0.0, NEG)
                    if rows < ns:
                        bias = jnp.concatenate([bias, jnp.full((ns - rows, QT), NEG, F32)], axis=0)
                    for r in range(NSA_REP):
                        h = g * NSA_REP + r
                        qa_ref[LANES:LANES + ns, h * QT:(h + 1) * QT] = bias.astype(BF16)

    def col_max(s):
        part = jnp.max(s.reshape(s.shape[0] // BF16_ROWS, BF16_ROWS, nl), axis=0)
        return jnp.max(part.astype(F32), axis=0, keepdims=True)

    def probs(s, m):
        return jnp.exp2(s - m.astype(BF16))

    def normalised(acc):
        return acc[0:HEAD_DIM, :] / acc[HEAD_DIM:HEAD_DIM + 1, :]

    def update(s, vt, first):
        cm = col_max(s)
        if first:
            m_new = cm
        else:
            m_old = m_ref[...]
            m_new = jnp.maximum(m_old, cm)
        pv = group_pv(vt, probs(s, m_new))
        if first:
            acc_ref[...] = pv
        else:
            acc_ref[...] = jnp.exp2(m_old - m_new) * acc_ref[...] + pv
        m_ref[...] = m_new

    n_back = WIN // WIN_CHUNK
    wshape = (WIN_CHUNK, nl)
    local_q = lax.broadcasted_iota(jnp.int32, wshape, 1) & (QT - 1)
    local_k = lax.broadcasted_iota(jnp.int32, wshape, 0)

    def window():
        ss, vts = [], []
        for c in range(n_back + 1):
            ci = i - n_back + c
            cl = jnp.maximum(ci, 0)
            k = kw_ref[0, pl.ds(pl.multiple_of(cl * WIN_CHUNK, WIN_CHUNK), WIN_CHUNK), :]
            s = _dot(k, qa_ref[0:LANES, :])
            if c == 0:
                s = jnp.where(local_k > local_q + jnp.where(ci >= 0, 0, WIN_CHUNK), s, NEG)
            elif c == n_back:
                s = jnp.where(local_k <= local_q, s, NEG)
            else:
                s = s + jnp.where(ci >= 0, 0.0, NEG)
            ss.append(s.astype(BF16))
            vts.append(vwt_ref[0, cl])
        m = col_max(ss[0])
        for s in ss[1:]:
            m = jnp.maximum(m, col_max(s))
        pb = jnp.concatenate([probs(s, m) for s in ss], axis=0)
        owin_ref[...] = normalised(group_pv(jnp.concatenate(vts, axis=1), pb))

    compressed_and_select()
    window()

    def sel_scores(buf, c, diag=False):
        k = ksa_ref[0, pl.ds(pl.multiple_of(c * SEL_CHUNK, SEL_CHUNK), SEL_CHUNK), :]
        s = _dot(k, qa_ref[...])
        if diag:
            shape = (SEL_CHUNK, nl)
            s = jnp.where(krow(shape, c * SEL_CHUNK) <= tq(shape), s, NEG)
        buf[...] = s.astype(BF16)

    def sel_update(buf, c, first):
        update(buf[...], vst_ref[0, c], first)

    cd = start // SEL_CHUNK
    sel_scores(sa_ref, cd, diag=True)
    sel_scores(sb_ref, 0)
    sel_update(sa_ref, cd, True)

    def sel_pair(p, carry):
        c0 = 2 * p
        sel_scores(sa_ref, c0 + 1)
        sel_update(sb_ref, c0, False)
        sel_scores(sb_ref, jnp.minimum(c0 + 2, cd - 1))
        sel_update(sa_ref, c0 + 1, False)
        return carry

    lax.fori_loop(0, cd // 2, sel_pair, 0)

    @pl.when(cd % 2 == 1)
    def _():
        sel_update(sb_ref, cd - 1, False)

    o_sel = normalised(acc_ref[...])
    o_cmp = ocmp_ref[...]
    o_win = owin_ref[...]

    sg = jax.nn.sigmoid(gate_ref[...].T[0:3 * NSA_HEADS, :])
    outs = []
    for h in range(NSA_HEADS):
        lanes = slice(h * QT, (h + 1) * QT)
        outs.append(sg[3 * h:3 * h + 1, :] * o_cmp[:, lanes] + sg[3 * h + 1:3 * h + 2, :] * o_sel[:, lanes]
                    + sg[3 * h + 2:3 * h + 3, :] * o_win[:, lanes])
    o_ref[...] = jnp.concatenate(outs, axis=0).T.astype(o_ref.dtype)


def _nsa_attn(znsa, gq, l, kc, vct, ovt, ksa, vst, kwn, vwt, b, s):
    ns = s // SLC_BLOCK
    nc = s // CMP_STRIDE
    nq = s // QT
    nl = NSA_HEADS * QT
    width = NSA_HEADS * HEAD_DIM
    kern = functools.partial(_nsa_attn_kernel, ns=ns, nc=nc, topn=min(SLC_TOPN, ns))
    return pl.pallas_call(
        kern,
        out_shape=jax.ShapeDtypeStruct((b * s, width), BF16),
        grid=(b, nq),
        in_specs=[
            pl.BlockSpec((QT, width), lambda bi, i: (bi * nq + i, 0)),
            pl.BlockSpec((QT, LANES), lambda bi, i: (bi * nq + i, 10)),
            _layer_spec(gq, l),
            pl.BlockSpec((1, nc, LANES), lambda bi, i: (bi, 0, 0)),
            pl.BlockSpec((1, LANES, nc), lambda bi, i: (bi, 0, 0)),
            pl.BlockSpec((ns, nc), lambda bi, i: (0, 0)),
            pl.BlockSpec((1, s, 2 * LANES), lambda bi, i: (bi, 0, 0)),
            pl.BlockSpec((1, s // SEL_CHUNK, NSA_GROUPS * VROWS, SEL_CHUNK), lambda bi, i: (bi, 0, 0, 0)),
            pl.BlockSpec((1, s, LANES), lambda bi, i: (bi, 0, 0)),
            pl.BlockSpec((1, s // WIN_CHUNK, NSA_GROUPS * VROWS, WIN_CHUNK), lambda bi, i: (bi, 0, 0, 0)),
        ],
        out_specs=pl.BlockSpec((QT, width), lambda bi, i: (bi * nq + i, 0)),
        scratch_shapes=[
            pltpu.VMEM((2 * LANES, nl), BF16),
            pltpu.VMEM((SEL_CHUNK, nl), BF16),
            pltpu.VMEM((SEL_CHUNK, nl), BF16),
            pltpu.VMEM((1, nl), F32),
            pltpu.VMEM((VROWS, nl), F32),
            pltpu.VMEM((HEAD_DIM, nl), F32),
            pltpu.VMEM((HEAD_DIM, nl), F32),
        ],
        compiler_params=_params(2),
        name="nsa_attn",
    )(znsa, znsa, gq, kc, vct, ovt, ksa, vst, kwn, vwt)


def _mix_kernel(x_ref, ab_ref, ac_ref, ax_ref, ach_ref, axh_ref, p_ref, ph_ref, sgu_ref, mg0_ref, mg1_ref, mg2_ref,
                mg3_ref, ob_ref, cw_ref, pw_ref, psc_ref, sng_ref, sw_ref, sb_ref, wbr_ref, wo_ref, o_ref, ext_ref,
                ext2_ref, *, tm, tiles_per_seq):
    ti = pl.program_id(0) % tiles_per_seq
    keep = jnp.where(ti == 0, 0.0, 1.0)

    ext_ref[0:HALO, :] = (ach_ref[...].astype(F32) * axh_ref[...].astype(F32)) * keep
    ext_ref[HALO:, :] = ac_ref[...].astype(F32) * ax_ref[...].astype(F32)
    e = ext_ref[...]
    cw = cw_ref[...]
    conv = (cw[0:1, :] * pltpu.roll(e, 2, 0) + cw[1:2, :] * pltpu.roll(e, 1, 0)) + cw[2:3, :] * e
    out_a = ab_ref[...].astype(F32) * conv[HALO:, :]

    p = p_ref[...].astype(F32)
    ext2_ref[0:HALO, :] = ph_ref[...].astype(F32) * keep
    ext2_ref[HALO:, :] = p
    e = ext2_ref[...]
    gw = BRANCH_WIDTH // len(POOL_WINDOWS)
    cnt = (ti * tm + 1 + lax.broadcasted_iota(jnp.int32, (tm, gw), 0)).astype(F32)
    groups = []
    for gi, w in enumerate(POOL_WINDOWS):
        acc = e[:, gi * gw:(gi + 1) * gw]
        span = 1
        while span < w:
            acc = acc + pltpu.roll(acc, span, 0)
            span *= 2
        groups.append(acc[HALO:, :] / jnp.minimum(cnt, float(w)))
    pooled = jnp.concatenate(groups, axis=1) - p
    out_c = _dot(pooled.astype(BF16), pw_ref[...]) * psc_ref[...]

    z = _erf_gelu(sgu_ref[...].astype(F32))
    u = z[:, :BRANCH_WIDTH]
    v = z[:, BRANCH_WIDTH:]
    v = (v * lax.rsqrt(jnp.mean(v * v, axis=-1, keepdims=True) + RMS_EPS)) * sng_ref[...]
    vb = v.astype(BF16)
    nchunk = tm // SGU_CHUNK
    gd = BRANCH_WIDTH // SGU_GROUPS
    tri = (lax.broadcasted_iota(jnp.int32, (SGU_CHUNK, SGU_CHUNK), 0)
           >= lax.broadcasted_iota(jnp.int32, (SGU_CHUNK, SGU_CHUNK), 1))
    mixed_g = []
    for g in range(SGU_GROUPS):
        wg = jnp.where(tri, sw_ref[g], 0.0).astype(BF16)
        rhs = jnp.concatenate(
            [vb[c * SGU_CHUNK:(c + 1) * SGU_CHUNK, g * gd:(g + 1) * gd] for c in range(nchunk)], axis=1)
        res = _dot(wg, rhs)
        mixed_g.append(jnp.concatenate([res[:, c * gd:(c + 1) * gd] for c in range(nchunk)], axis=0))
    bias = jnp.concatenate([sb_ref[...]] * nchunk, axis=0)
    out_d = u * (jnp.concatenate(mixed_g, axis=1) + bias)

    merged = jax.nn.sigmoid(mg0_ref[...].astype(F32)) * _dot(out_a.astype(BF16), wbr_ref[0])
    merged = merged + jax.nn.sigmoid(mg1_ref[...].astype(F32)) * _dot(ob_ref[...], wbr_ref[1])
    merged = merged + jax.nn.sigmoid(mg2_ref[...].astype(F32)) * _dot(out_c.astype(BF16), wbr_ref[2])
    merged = merged + jax.nn.sigmoid(mg3_ref[...].astype(F32)) * _dot(out_d.astype(BF16), wbr_ref[3])
    o_ref[...] = x_ref[...] + _dot(merged.astype(BF16), wo_ref[...])


def _mix_merge(x, za, out_b, cw, pw, psc, sng, sw, sb, wbr, wo, l, s, tm=256):
    t, d = x.shape
    bw = BRANCH_WIDTH
    tps = s // tm
    hb = tm // HALO
    tile = lambda c: pl.BlockSpec((tm, bw), lambda i, c=c: (i, c))
    halo = lambda c: pl.BlockSpec((HALO, bw), lambda i, c=c: (jnp.maximum(i * hb - 1, 0), c))
    wide = lambda c: pl.BlockSpec((tm, d), lambda i, c=c: (i, c))
    full = lambda a: _layer_spec(a, l)
    consts = (cw, pw, psc, sng, sw, sb, wbr, wo)
    return pl.pallas_call(
        functools.partial(_mix_kernel, tm=tm, tiles_per_seq=tps),
        out_shape=jax.ShapeDtypeStruct((t, d), F32),
        grid=(t // tm,),
        in_specs=[wide(0), tile(0), tile(1), tile(2), halo(1), halo(2), tile(3), halo(3), wide(2),
                  wide(3), wide(4), wide(5), wide(6), tile(0)] + [full(a) for a in consts],
        out_specs=wide(0),
        scratch_shapes=[pltpu.VMEM((tm + HALO, bw), F32), pltpu.VMEM((tm + HALO, bw), F32)],
        compiler_params=_params(1),
        name="mix_merge",
    )(x, za, za, za, za, za, za, za, za, za, za, za, za, out_b, *consts)


def _ffn_act_kernel(x_ref, xh_ref, g_ref, w_ref, cw_ref, o_ref, h_ref, hh_ref, ext_ref, *, tiles_per_seq, rb):
    @pl.when(pl.program_id(1) == 0)
    def _():
        h_ref[...] = _rms_bf16(x_ref[...], g_ref[...])
        hh_ref[...] = _rms_bf16(xh_ref[...], g_ref[...])

    keep = jnp.where(pl.program_id(0) % tiles_per_seq == 0, 0.0, 1.0)
    w = w_ref[...]
    cw = cw_ref[...]
    ext_ref[0:HALO, :] = _dot(hh_ref[...], w[:, :FF_CHUNK]) * keep
    for r0 in range(0, h_ref.shape[0], rb):
        gu = _dot(h_ref[r0:r0 + rb, :], w)
        ext_ref[HALO + r0:HALO + r0 + rb, :] = gu[:, :FF_CHUNK]
        e = ext_ref[r0:r0 + rb + HALO, :]
        conv = ((cw[0:1, :] * pltpu.roll(e, 2, 0) + cw[1:2, :] * pltpu.roll(e, 1, 0)) + cw[2:3, :] * e)[HALO:, :]
        o_ref[r0:r0 + rb, :] = ((conv * jax.nn.sigmoid(conv)) * gu[:, FF_CHUNK:]).astype(o_ref.dtype)


def _ffn_act(x, g, w_gu, cw, l, s, tm):
    t, d = x.shape
    dff = cw.shape[2]
    hb = tm // HALO
    return pl.pallas_call(
        functools.partial(_ffn_act_kernel, tiles_per_seq=s // tm, rb=min(tm, 512)),
        out_shape=jax.ShapeDtypeStruct((t, dff), BF16),
        grid=(t // tm, dff // FF_CHUNK),
        in_specs=[
            pl.BlockSpec((tm, d), lambda i, j: (i, 0)),
            pl.BlockSpec((HALO, d), lambda i, j: (jnp.maximum(i * hb - 1, 0), 0)),
            _layer_spec(g, l),
            _layer_spec(w_gu, l, (d, 2 * FF_CHUNK), lambda i, j: (0, j)),
            _layer_spec(cw, l, (cw.shape[1], FF_CHUNK), lambda i, j: (0, j)),
        ],
        out_specs=pl.BlockSpec((tm, FF_CHUNK), lambda i, j: (i, j)),
        scratch_shapes=[pltpu.VMEM((tm, d), BF16), pltpu.VMEM((HALO, d), BF16), pltpu.VMEM((tm + HALO, FF_CHUNK), F32)],
        compiler_params=_params(2),
        name="ffn_act",
    )(x, x, g, w_gu, cw)


def _ffn_down_kernel(x_ref, a_ref, wd_ref, o_ref):
    o_ref[...] = x_ref[...] + _dot(a_ref[...], wd_ref[...])


def _ffn_down(x, act, wd, l, tm):
    t, d = x.shape
    dff = wd.shape[1]
    return pl.pallas_call(
        _ffn_down_kernel,
        out_shape=jax.ShapeDtypeStruct((t, d), F32),
        grid=(t // tm,),
        in_specs=[
            pl.BlockSpec((tm, d), lambda i: (i, 0)),
            pl.BlockSpec((tm, dff), lambda i: (i, 0)),
            _layer_spec(wd, l),
        ],
        out_specs=pl.BlockSpec((tm, d), lambda i: (i, 0)),
        compiler_params=_params(1),
        name="ffn_down",
    )(x, act, wd)


def _overlap_t(nc, ns):
    n_cmp = nc - 1
    cs = np.arange(n_cmp) * CMP_STRIDE
    ce = cs + CMP_BLOCK
    ss = np.arange(ns) * SLC_BLOCK
    se = ss + SLC_BLOCK
    ov = np.clip(np.minimum(ce[:, None], se[None]) - np.maximum(cs[:, None], ss[None]), 0, None)
    out = np.zeros((ns, nc), np.float32)
    out[:, :n_cmp] = (ov.astype(np.float32) / CMP_BLOCK).T
    return out


def _compress_params(pe, w1, w2):
    depth = pe.shape[0]
    half = CMP_BLOCK // 2
    pe2 = jnp.tile(pe.reshape(depth, 2, 2, half, 1, HEAD_DIM), (1, 1, 1, 1, NSA_GROUPS, 1))
    pe2 = pe2.reshape(depth, 2, 2, half * LANES)
    w1r = w1.reshape(depth, 2, 2, half, HEAD_DIM, HEAD_DIM)
    eye = jnp.eye(NSA_GROUPS, dtype=w1.dtype)
    w1bd = jnp.einsum("zjpldo,gh->zjplgdho", w1r, eye).reshape(depth, 2, 2, half * LANES, LANES)
    w2bd = jnp.einsum("zjdo,gh->zjgdho", w2, eye).reshape(depth, 2, LANES, LANES)
    return pe2, w1bd.astype(BF16), w2bd.astype(BF16)


def kernel(x, norm1_g, w_in, conv_a_w, qk_norm_g, cmp_pe, cmp_w1, cmp_w2, pool_w, pool_scale, sgu_norm_g, sgu_w,
           sgu_b, w_branch, w_o, norm2_g, w_up, conv_ff_w, w_down):
    b, s, d = x.shape
    depth = w_in.shape[0]
    t = b * s
    bw = BRANCH_WIDTH
    kvw = 6 * NSA_GROUPS * HEAD_DIM
    ngate = 3 * NSA_HEADS
    o_q = 3 * bw
    o_kv = o_q + bw
    o_gate = o_kv + kvw
    o_pool = o_gate + ngate
    dff = w_down.shape[1]
    ns = s // SLC_BLOCK
    nc = s // CMP_STRIDE
    ovt = jnp.asarray(_overlap_t(nc, ns), BF16)
    xf = x.reshape(t, d)
    tm = min(1024, t)
    tm_wide = min(2048, t)
    w_a = jnp.concatenate([w_in[:, :, :o_q], w_in[:, :, o_pool:]], axis=2).astype(BF16)
    w_n = jnp.concatenate([w_in[:, :, o_q:o_pool], jnp.zeros((depth, d, LANES - ngate), w_in.dtype)],
                          axis=2).astype(BF16)
    row = lambda v: v.reshape(depth, 1, v.shape[-1])
    g2 = lambda k: row(jnp.tile(qk_norm_g[:, k], (1, NSA_GROUPS)))
    g_cmp, g_sel, g_win = g2(1), g2(2), g2(3)
    gq = jnp.broadcast_to(qk_norm_g[:, 0][:, :, None], (depth, HEAD_DIM, QT))
    pe2, w1bd, w2bd = _compress_params(cmp_pe, cmp_w1, cmp_w2)
    n_pool = len(POOL_WINDOWS)
    pw = jnp.einsum("zgio,gh->zgiho", pool_w, jnp.eye(n_pool, dtype=pool_w.dtype)).reshape(depth, bw, bw).astype(BF16)
    sb = jnp.repeat(jnp.swapaxes(sgu_b, 1, 2), bw // SGU_GROUPS, axis=2)
    nj = dff // FF_CHUNK
    w_gu = jnp.concatenate([w_up[:, :, :dff].reshape(depth, d, nj, FF_CHUNK),
                            w_up[:, :, dff:].reshape(depth, d, nj, FF_CHUNK)], axis=3)
    w_gu = w_gu.reshape(depth, d, 2 * dff).astype(BF16)
    w_br, w_out, w_dn = w_branch.astype(BF16), w_o.astype(BF16), w_down.astype(BF16)
    n1, n2, psc, sng = row(norm1_g), row(norm2_g), row(pool_scale), row(sgu_norm_g)

    for l in range(depth):
        za = _norm_matmul(xf, n1, w_a, l, BF16, tm_wide, 2 * bw, "in_proj_a")
        znsa = _norm_matmul(xf, n1, w_n, l, F32, tm, w_n.shape[2], "in_proj_nsa")
        ksa, vst, kwn, vwt = _kv_prep(znsa, g_sel, g_win, l, b, s)
        kc, vct = _compress(znsa, pe2, w1bd, w2bd, g_cmp, l, b, s)
        out_b = _nsa_attn(znsa, gq, l, kc, vct, ovt, ksa, vst, kwn, vwt, b, s)
        xf = _mix_merge(xf, za, out_b, conv_a_w, pw, psc, sng, sgu_w, sb, w_br, w_out, l, s)
        act = _ffn_act(xf, n2, w_gu, conv_ff_w, l, s, tm_wide)
        xf = _ffn_down(xf, act, w_dn, l, tm)
    return xf.reshape(b, s, d)
```

```python
import functools

import jax
import jax.numpy as jnp
import numpy as np
from jax import lax
from jax.experimental import pallas as pl
from jax.experimental.pallas import tpu as pltpu

F32 = jnp.float32
BF16 = jnp.bfloat16

HEAD_DIM = 64
NSA_HEADS = 8
NSA_GROUPS = 2
NSA_REP = NSA_HEADS // NSA_GROUPS
CMP_BLOCK = 32
CMP_STRIDE = 16
SLC_BLOCK = 64
SLC_TOPN = 16
WIN = 512
NEG = -1e30
FORCE = 1e6
RMS_EPS = 1e-6
LOG2E = 1.4426950408889634
POOL_WINDOWS = (2, 4, 8, 16)
SGU_CHUNK = 128
SGU_GROUPS = 4
BRANCH_WIDTH = 512
N_BRANCH = 4

LANES = 128
HALO = 16
QT = 128
SEL_CHUNK = 256
WIN_CHUNK = 128
FF_CHUNK = 256
RANK_STEP = 16
VROWS = HEAD_DIM + 16
BF16_ROWS = 16
VMEM_LIMIT = 56 * 1024 * 1024


def _params(n_axes):
    return pltpu.CompilerParams(dimension_semantics=("arbitrary",) * n_axes, vmem_limit_bytes=VMEM_LIMIT)


def _dot(a, b):
    return jnp.dot(a, b, preferred_element_type=F32)


def _layer_spec(arr, l, block=None, index=None):
    block = tuple(arr.shape[1:]) if block is None else tuple(block)
    index = index or (lambda *_: (0,) * len(block))
    return pl.BlockSpec((None,) + block, lambda *grid: (l,) + tuple(index(*grid)))


def _rms_bf16(x, g):
    ms = jnp.mean(x * x, axis=-1, keepdims=True)
    return ((x * lax.rsqrt(ms + RMS_EPS)) * g).astype(BF16)


def _erf_gelu(x):
    return 0.5 * x * (1.0 + lax.erf(x * (2.0 ** -0.5)))


def _norm_matmul_kernel(x_ref, g_ref, w_ref, o_ref, h_ref):
    @pl.when(pl.program_id(1) == 0)
    def _():
        h_ref[...] = _rms_bf16(x_ref[...], g_ref[...])

    o_ref[...] = _dot(h_ref[...], w_ref[...]).astype(o_ref.dtype)


def _norm_matmul(x, g, w, l, out_dtype, tm, tn, name):
    t, d = x.shape
    n = w.shape[2]
    return pl.pallas_call(
        _norm_matmul_kernel,
        out_shape=jax.ShapeDtypeStruct((t, n), out_dtype),
        grid=(t // tm, n // tn),
        in_specs=[
            pl.BlockSpec((tm, d), lambda i, j: (i, 0)),
            _layer_spec(g, l),
            _layer_spec(w, l, (d, tn), lambda i, j: (0, j)),
        ],
        out_specs=pl.BlockSpec((tm, tn), lambda i, j: (i, j)),
        scratch_shapes=[pltpu.VMEM((tm, d), BF16)],
        compiler_params=_params(2),
        name=name,
    )(x, g, w)


def _group_rms(x, g):
    lane = lax.broadcasted_iota(jnp.int32, x.shape, 1)
    lo = lane < HEAD_DIM
    x2 = x * x
    s_lo = jnp.sum(jnp.where(lo, x2, 0.0), axis=-1, keepdims=True)
    s_hi = jnp.sum(jnp.where(lo, 0.0, x2), axis=-1, keepdims=True)
    ms = jnp.where(lo, s_lo, s_hi) * (1.0 / HEAD_DIM)
    return (x * lax.rsqrt(ms + RMS_EPS)) * g


def _kv_prep_kernel(ks_ref, vs_ref, kw_ref, vw_ref, gs_ref, gw_ref, ksa_ref, vst_ref, kwn_ref, vwt_ref, *, ts):
    j = pl.program_id(1)
    ksn = _group_rms(ks_ref[...], gs_ref[...]).astype(BF16)
    row = lax.broadcasted_iota(jnp.int32, (ts, LANES), 0) + j * ts
    col = lax.broadcasted_iota(jnp.int32, (ts, LANES), 1)
    onehot = jnp.where(row // SLC_BLOCK == col, 1.0, 0.0).astype(BF16)
    ksa_ref[0, :, 0:LANES] = ksn
    ksa_ref[0, :, LANES:2 * LANES] = onehot
    kwn_ref[0] = _group_rms(kw_ref[...], gw_ref[...]).astype(BF16)
    vs = vs_ref[...]
    for c in range(ts // SEL_CHUNK):
        vst_ref[0, c] = _values_t(vs[c * SEL_CHUNK:(c + 1) * SEL_CHUNK, :])
    vw = vw_ref[...]
    for c in range(ts // WIN_CHUNK):
        vwt_ref[0, c] = _values_t(vw[c * WIN_CHUNK:(c + 1) * WIN_CHUNK, :])


def _values_t(v):
    vt = v.T
    keys = vt.shape[1]
    ones_row = jnp.where(lax.broadcasted_iota(jnp.int32, (VROWS - HEAD_DIM, keys), 0) == 0, 1.0, 0.0)
    parts = []
    for g in range(NSA_GROUPS):
        parts += [vt[g * HEAD_DIM:(g + 1) * HEAD_DIM, :], ones_row]
    return jnp.concatenate(parts, axis=0).astype(BF16)


def _kv_prep(znsa, gs, gw, l, b, s, ts=2048):
    ts = min(ts, s)
    nt = s // ts
    col = lambda c: pl.BlockSpec((ts, LANES), lambda bi, j, c=c: (bi * nt + j, c))
    vec = _layer_spec(gs, l)
    return pl.pallas_call(
        functools.partial(_kv_prep_kernel, ts=ts),
        out_shape=(
            jax.ShapeDtypeStruct((b, s, 2 * LANES), BF16),
            jax.ShapeDtypeStruct((b, s // SEL_CHUNK, NSA_GROUPS * VROWS, SEL_CHUNK), BF16),
            jax.ShapeDtypeStruct((b, s, LANES), BF16),
            jax.ShapeDtypeStruct((b, s // WIN_CHUNK, NSA_GROUPS * VROWS, WIN_CHUNK), BF16),
        ),
        grid=(b, nt),
        in_specs=[col(6), col(7), col(8), col(9), vec, vec],
        out_specs=(
            pl.BlockSpec((1, ts, 2 * LANES), lambda bi, j: (bi, j, 0)),
            pl.BlockSpec((1, ts // SEL_CHUNK, NSA_GROUPS * VROWS, SEL_CHUNK), lambda bi, j: (bi, j, 0, 0)),
            pl.BlockSpec((1, ts, LANES), lambda bi, j: (bi, j, 0)),
            pl.BlockSpec((1, ts // WIN_CHUNK, NSA_GROUPS * VROWS, WIN_CHUNK), lambda bi, j: (bi, j, 0, 0)),
        ),
        compiler_params=_params(2),
        name="kv_prep",
    )(znsa, znsa, znsa, znsa, gs, gw)


def _compress_kernel(kc_ref, vc_ref, pe_ref, w1_ref, w2_ref, g_ref, kco_ref, vct_ref, u_ref, *, nc):
    half = CMP_BLOCK // 2
    for j, src in enumerate((kc_ref, vc_ref)):
        for l in range(half):
            u_ref[:, l * LANES:(l + 1) * LANES] = src[pl.ds(l, nc, stride=CMP_STRIDE), :]
        u = u_ref[...]
        top = _dot((u + pe_ref[j, 0:1, :]).astype(BF16), w1_ref[j, 0])
        bot = _dot((u + pe_ref[j, 1:2, :]).astype(BF16), w1_ref[j, 1])
        pre = top + pltpu.roll(bot, nc - 1, 0)
        hid = pre * jax.nn.sigmoid(pre)
        out = _dot(hid.astype(BF16), w2_ref[j])
        if j == 0:
            kco_ref[0] = _group_rms(out, g_ref[...]).astype(BF16)
        else:
            vct_ref[0] = out.T.astype(BF16)


def _compress(znsa, pe2, w1bd, w2bd, g1, l, b, s):
    nc = s // CMP_STRIDE
    kw = (CMP_BLOCK // 2) * LANES
    return pl.pallas_call(
        functools.partial(_compress_kernel, nc=nc),
        out_shape=(
            jax.ShapeDtypeStruct((b, nc, LANES), BF16),
            jax.ShapeDtypeStruct((b, LANES, nc), BF16),
        ),
        grid=(b,),
        in_specs=[
            pl.BlockSpec((s, LANES), lambda bi: (bi, 4)),
            pl.BlockSpec((s, LANES), lambda bi: (bi, 5)),
            _layer_spec(pe2, l),
            _layer_spec(w1bd, l),
            _layer_spec(w2bd, l),
            _layer_spec(g1, l),
        ],
        out_specs=(
            pl.BlockSpec((1, nc, LANES), lambda bi: (bi, 0, 0)),
            pl.BlockSpec((1, LANES, nc), lambda bi: (bi, 0, 0)),
        ),
        scratch_shapes=[pltpu.VMEM((nc, kw), F32)],
        compiler_params=_params(1),
        name="compress",
    )(znsa, znsa, pe2, w1bd, w2bd, g1)


def _rank_desc(imp, ns):
    nq = imp.shape[1]
    sub = lax.broadcasted_iota(jnp.int32, (8, nq), 0)
    blocks = [imp[8 * a:8 * a + 8, :] for a in range(ns // 8)]
    rank = [jnp.zeros((8, nq), F32) for _ in blocks]
    for k in range(ns):
        ka, kr = divmod(k, 8)
        rk = jnp.broadcast_to(imp[k:k + 1, :], (8, nq))
        for a, blk in enumerate(blocks):
            if a > ka:
                ahead = rk >= blk
            elif a < ka:
                ahead = rk > blk
            else:
                rank[a] = rank[a] + jnp.where(sub > kr, jnp.where(rk >= blk, 1.0, 0.0), jnp.where(rk > blk, 1.0, 0.0))
                continue
            rank[a] = rank[a] + jnp.where(ahead, 1.0, 0.0)
    return jnp.concatenate(rank, axis=0)


def _nsa_attn_kernel(q_ref, gate_ref, gq_ref, kc_ref, vct_ref, ovt_ref, ksa_ref, vst_ref, kw_ref, vwt_ref,
                     o_ref, qa_ref, sa_ref, sb_ref, m_ref, acc_ref, ocmp_ref, owin_ref, *, ns, nc, topn):
    i = pl.program_id(1)
    start = i * QT
    nl = NSA_HEADS * QT
    gl = NSA_REP * QT

    qt = q_ref[...].T.reshape(NSA_HEADS, HEAD_DIM, QT)
    ms = jnp.mean(qt * qt, axis=1, keepdims=True)
    qn = ((qt * lax.rsqrt(ms + RMS_EPS)) * gq_ref[...][None]) * (HEAD_DIM ** -0.5 * LOG2E)
    qa_ref[...] = jnp.zeros(qa_ref.shape, BF16)
    for h in range(NSA_HEADS):
        g = h // NSA_REP
        qa_ref[g * HEAD_DIM:(g + 1) * HEAD_DIM, h * QT:(h + 1) * QT] = qn[h].astype(BF16)

    def tq(shape):
        return start + (lax.broadcasted_iota(jnp.int32, shape, 1) & (QT - 1))

    def krow(shape, base):
        return base + lax.broadcasted_iota(jnp.int32, shape, 0)

    def group_pv(vt, pb):
        rows = vt.shape[0] // NSA_GROUPS
        return jnp.concatenate(
            [_dot(vt[g * rows:(g + 1) * rows, :], pb[:, g * gl:(g + 1) * gl]) for g in range(NSA_GROUPS)],
            axis=1)

    def compressed_and_select():
        qk = qa_ref[0:LANES, :]
        sc = _dot(kc_ref[0], qk)
        cval = krow((nc, nl), 0) * CMP_STRIDE + (CMP_BLOCK - 1) <= tq((nc, nl))
        sc = jnp.where(cval, sc, NEG)
        e = jnp.exp2(sc - jnp.max(sc, axis=0, keepdims=True))
        pc = e / jnp.sum(e, axis=0, keepdims=True)
        pc = jnp.where(tq((1, nl)) >= CMP_BLOCK - 1, pc, 0.0)
        ocmp_ref[...] = group_pv(vct_ref[0], pc.astype(BF16))

        jb = krow((ns, QT), 0)
        jt = (start + lax.broadcasted_iota(jnp.int32, (ns, QT), 1)) // SLC_BLOCK
        forced = (jb == 0) | (jb == jt) | (jb == jt - 1)
        future = jb > jt
        last_block = (start + QT - 1) // SLC_BLOCK
        n_var = max(ns // RANK_STEP, 1)
        ovt = ovt_ref[...]
        for g in range(NSA_GROUPS):
            psum = pc[:, g * gl:g * gl + QT]
            for r in range(1, NSA_REP):
                psum = psum + pc[:, g * gl + r * QT:g * gl + (r + 1) * QT]
            p_hi = psum.astype(BF16)
            rem = psum - p_hi.astype(F32)
            p_mid = rem.astype(BF16)
            p_lo = (rem - p_mid.astype(F32)).astype(BF16)
            imp = (_dot(ovt, p_hi) + _dot(ovt, p_mid)) + _dot(ovt, p_lo)
            imp = jnp.where(forced, FORCE, jnp.where(future, NEG, imp))
            for v in range(n_var):
                rows = RANK_STEP * (v + 1)

                @pl.when(jnp.minimum(last_block // RANK_STEP, n_var - 1) == v)
                def _(g=g, imp=imp, rows=rows):
                    sel = (_rank_desc(imp[0:rows, :], rows) < topn) & jnp.logical_not(future[0:rows, :])
                    bias = jnp.where(sel, 0.0, NEG)
                    if rows < ns:
                        bias = jnp.concatenate([bias, jnp.full((ns - rows, QT), NEG, F32)], axis=0)
                    for r in range(NSA_REP):
                        h = g * NSA_REP + r
                        qa_ref[LANES:LANES + ns, h * QT:(h + 1) * QT] = bias.astype(BF16)

    def col_max(s):
        part = jnp.max(s.reshape(s.shape[0] // BF16_ROWS, BF16_ROWS, nl), axis=0)
        return jnp.max(part.astype(F32), axis=0, keepdims=True)

    def probs(s, m):
        return jnp.exp2(s - m.astype(BF16))

    def normalised(acc):
        return acc[0:HEAD_DIM, :] / acc[HEAD_DIM:HEAD_DIM + 1, :]

    def update(s, vt, first):
        cm = col_max(s)
        if first:
            m_new = cm
        else:
            m_old = m_ref[...]
            m_new = jnp.maximum(m_old, cm)
        pv = group_pv(vt, probs(s, m_new))
        if first:
            acc_ref[...] = pv
        else:
            acc_ref[...] = jnp.exp2(m_old - m_new) * acc_ref[...] + pv
        m_ref[...] = m_new

    n_back = WIN // WIN_CHUNK
    wshape = (WIN_CHUNK, nl)
    local_q = lax.broadcasted_iota(jnp.int32, wshape, 1) & (QT - 1)
    local_k = lax.broadcasted_iota(jnp.int32, wshape, 0)

    def window():
        ss, vts = [], []
        for c in range(n_back + 1):
            ci = i - n_back + c
            cl = jnp.maximum(ci, 0)
            k = kw_ref[0, pl.ds(pl.multiple_of(cl * WIN_CHUNK, WIN_CHUNK), WIN_CHUNK), :]
            s = _dot(k, qa_ref[0:LANES, :])
            if c == 0:
                s = jnp.where(local_k > local_q + jnp.where(ci >= 0, 0, WIN_CHUNK), s, NEG)
            elif c == n_back:
                s = jnp.where(local_k <= local_q, s, NEG)
            else:
                s = s + jnp.where(ci >= 0, 0.0, NEG)
            ss.append(s.astype(BF16))
            vts.append(vwt_ref[0, cl])
        m = col_max(ss[0])
        for s in ss[1:]:
            m = jnp.maximum(m, col_max(s))
        pb = jnp.concatenate([probs(s, m) for s in ss], axis=0)
        owin_ref[...] = normalised(group_pv(jnp.concatenate(vts, axis=1), pb))

    compressed_and_select()
    window()

    def sel_scores(buf, c, diag=False):
        k = ksa_ref[0, pl.ds(pl.multiple_of(c * SEL_CHUNK, SEL_CHUNK), SEL_CHUNK), :]
        s = _dot(k, qa_ref[...])
        if diag:
            shape = (SEL_CHUNK, nl)
            s = jnp.where(krow(shape, c * SEL_CHUNK) <= tq(shape), s, NEG)
        buf[...] = s.astype(BF16)

    def sel_update(buf, c, first):
        update(buf[...], vst_ref[0, c], first)

    cd = start // SEL_CHUNK
    sel_scores(sa_ref, cd, diag=True)
    sel_scores(sb_ref, 0)
    sel_update(sa_ref, cd, True)

    def sel_pair(p, carry):
        c0 = 2 * p
        sel_scores(sa_ref, c0 + 1)
        sel_update(sb_ref, c0, False)
        sel_scores(sb_ref, jnp.minimum(c0 + 2, cd - 1))
        sel_update(sa_ref, c0 + 1, False)
        return carry

    lax.fori_loop(0, cd // 2, sel_pair, 0)

    @pl.when(cd % 2 == 1)
    def _():
        sel_update(sb_ref, cd - 1, False)

    o_sel = normalised(acc_ref[...])
    o_cmp = ocmp_ref[...]
    o_win = owin_ref[...]

    sg = jax.nn.sigmoid(gate_ref[...].T[0:3 * NSA_HEADS, :])
    outs = []
    for h in range(NSA_HEADS):
        lanes = slice(h * QT, (h + 1) * QT)
        outs.append(sg[3 * h:3 * h + 1, :] * o_cmp[:, lanes] + sg[3 * h + 1:3 * h + 2, :] * o_sel[:, lanes]
                    + sg[3 * h + 2:3 * h + 3, :] * o_win[:, lanes])
    o_ref[...] = jnp.concatenate(outs, axis=0).T.astype(o_ref.dtype)


def _nsa_attn(znsa, gq, l, kc, vct, ovt, ksa, vst, kwn, vwt, b, s):
    ns = s // SLC_BLOCK
    nc = s // CMP_STRIDE
    nq = s // QT
    nl = NSA_HEADS * QT
    width = NSA_HEADS * HEAD_DIM
    kern = functools.partial(_nsa_attn_kernel, ns=ns, nc=nc, topn=min(SLC_TOPN, ns))
    return pl.pallas_call(
        kern,
        out_shape=jax.ShapeDtypeStruct((b * s, width), BF16),
        grid=(b, nq),
        in_specs=[
            pl.BlockSpec((QT, width), lambda bi, i: (bi * nq + i, 0)),
            pl.BlockSpec((QT, LANES), lambda bi, i: (bi * nq + i, 10)),
            _layer_spec(gq, l),
            pl.BlockSpec((1, nc, LANES), lambda bi, i: (bi, 0, 0)),
            pl.BlockSpec((1, LANES, nc), lambda bi, i: (bi, 0, 0)),
            pl.BlockSpec((ns, nc), lambda bi, i: (0, 0)),
            pl.BlockSpec((1, s, 2 * LANES), lambda bi, i: (bi, 0, 0)),
            pl.BlockSpec((1, s // SEL_CHUNK, NSA_GROUPS * VROWS, SEL_CHUNK), lambda bi, i: (bi, 0, 0, 0)),
            pl.BlockSpec((1, s, LANES), lambda bi, i: (bi, 0, 0)),
            pl.BlockSpec((1, s // WIN_CHUNK, NSA_GROUPS * VROWS, WIN_CHUNK), lambda bi, i: (bi, 0, 0, 0)),
        ],
        out_specs=pl.BlockSpec((QT, width), lambda bi, i: (bi * nq + i, 0)),
        scratch_shapes=[
            pltpu.VMEM((2 * LANES, nl), BF16),
            pltpu.VMEM((SEL_CHUNK, nl), BF16),
            pltpu.VMEM((SEL_CHUNK, nl), BF16),
            pltpu.VMEM((1, nl), F32),
            pltpu.VMEM((VROWS, nl), F32),
            pltpu.VMEM((HEAD_DIM, nl), F32),
            pltpu.VMEM((HEAD_DIM, nl), F32),
        ],
        compiler_params=_params(2),
        name="nsa_attn",
    )(znsa, znsa, gq, kc, vct, ovt, ksa, vst, kwn, vwt)


def _mix_kernel(x_ref, ab_ref, ac_ref, ax_ref, ach_ref, axh_ref, p_ref, ph_ref, sgu_ref, mg0_ref, mg1_ref, mg2_ref,
                mg3_ref, ob_ref, cw_ref, pw_ref, psc_ref, sng_ref, sw_ref, sb_ref, wbr_ref, wo_ref, o_ref, ext_ref,
                ext2_ref, *, tm, tiles_per_seq):
    ti = pl.program_id(0) % tiles_per_seq
    keep = jnp.where(ti == 0, 0.0, 1.0)

    ext_ref[0:HALO, :] = (ach_ref[...].astype(F32) * axh_ref[...].astype(F32)) * keep
    ext_ref[HALO:, :] = ac_ref[...].astype(F32) * ax_ref[...].astype(F32)
    e = ext_ref[...]
    cw = cw_ref[...]
    conv = (cw[0:1, :] * pltpu.roll(e, 2, 0) + cw[1:2, :] * pltpu.roll(e, 1, 0)) + cw[2:3, :] * e
    out_a = ab_ref[...].astype(F32) * conv[HALO:, :]

    p = p_ref[...].astype(F32)
    ext2_ref[0:HALO, :] = ph_ref[...].astype(F32) * keep
    ext2_ref[HALO:, :] = p
    e = ext2_ref[...]
    gw = BRANCH_WIDTH // len(POOL_WINDOWS)
    cnt = (ti * tm + 1 + lax.broadcasted_iota(jnp.int32, (tm, gw), 0)).astype(F32)
    groups = []
    for gi, w in enumerate(POOL_WINDOWS):
        acc = e[:, gi * gw:(gi + 1) * gw]
        span = 1
        while span < w:
            acc = acc + pltpu.roll(acc, span, 0)
            span *= 2
        groups.append(acc[HALO:, :] / jnp.minimum(cnt, float(w)))
    pooled = jnp.concatenate(groups, axis=1) - p
    out_c = _dot(pooled.astype(BF16), pw_ref[...]) * psc_ref[...]

    z = _erf_gelu(sgu_ref[...].astype(F32))
    u = z[:, :BRANCH_WIDTH]
    v = z[:, BRANCH_WIDTH:]
    v = (v * lax.rsqrt(jnp.mean(v * v, axis=-1, keepdims=True) + RMS_EPS)) * sng_ref[...]
    vb = v.astype(BF16)
    nchunk = tm // SGU_CHUNK
    gd = BRANCH_WIDTH // SGU_GROUPS
    tri = (lax.broadcasted_iota(jnp.int32, (SGU_CHUNK, SGU_CHUNK), 0)
           >= lax.broadcasted_iota(jnp.int32, (SGU_CHUNK, SGU_CHUNK), 1))
    mixed_g = []
    for g in range(SGU_GROUPS):
        wg = jnp.where(tri, sw_ref[g], 0.0).astype(BF16)
        rhs = jnp.concatenate(
            [vb[c * SGU_CHUNK:(c + 1) * SGU_CHUNK, g * gd:(g + 1) * gd] for c in range(nchunk)], axis=1)
        res = _dot(wg, rhs)
        mixed_g.append(jnp.concatenate([res[:, c * gd:(c + 1) * gd] for c in range(nchunk)], axis=0))
    bias = jnp.concatenate([sb_ref[...]] * nchunk, axis=0)
    out_d = u * (jnp.concatenate(mixed_g, axis=1) + bias)

    merged = jax.nn.sigmoid(mg0_ref[...].astype(F32)) * _dot(out_a.astype(BF16), wbr_ref[0])
    merged = merged + jax.nn.sigmoid(mg1_ref[...].astype(F32)) * _dot(ob_ref[...], wbr_ref[1])
    merged = merged + jax.nn.sigmoid(mg2_ref[...].astype(F32)) * _dot(out_c.astype(BF16), wbr_ref[2])
    merged = merged + jax.nn.sigmoid(mg3_ref[...].astype(F32)) * _dot(out_d.astype(BF16), wbr_ref[3])
    o_ref[...] = x_ref[...] + _dot(merged.astype(BF16), wo_ref[...])


def _mix_merge(x, za, out_b, cw, pw, psc, sng, sw, sb, wbr, wo, l, s, tm=512):
    t, d = x.shape
    bw = BRANCH_WIDTH
    tps = s // tm
    hb = tm // HALO
    tile = lambda c: pl.BlockSpec((tm, bw), lambda i, c=c: (i, c))
    halo = lambda c: pl.BlockSpec((HALO, bw), lambda i, c=c: (jnp.maximum(i * hb - 1, 0), c))
    wide = lambda c: pl.BlockSpec((tm, d), lambda i, c=c: (i, c))
    full = lambda a: _layer_spec(a, l)
    consts = (cw, pw, psc, sng, sw, sb, wbr, wo)
    return pl.pallas_call(
        functools.partial(_mix_kernel, tm=tm, tiles_per_seq=tps),
        out_shape=jax.ShapeDtypeStruct((t, d), F32),
        grid=(t // tm,),
        in_specs=[wide(0), tile(0), tile(1), tile(2), halo(1), halo(2), tile(3), halo(3), wide(2),
                  wide(3), wide(4), wide(5), wide(6), tile(0)] + [full(a) for a in consts],
        out_specs=wide(0),
        scratch_shapes=[pltpu.VMEM((tm + HALO, bw), F32), pltpu.VMEM((tm + HALO, bw), F32)],
        compiler_params=_params(1),
        name="mix_merge",
    )(x, za, za, za, za, za, za, za, za, za, za, za, za, out_b, *consts)


def _ffn_act_kernel(x_ref, xh_ref, g_ref, wg_ref, wu_ref, cw_ref, o_ref, h_ref, hh_ref, ext_ref, *,
                    tiles_per_seq, rb):
    @pl.when(pl.program_id(1) == 0)
    def _():
        h_ref[...] = _rms_bf16(x_ref[...], g_ref[...])
        hh_ref[...] = _rms_bf16(xh_ref[...], g_ref[...])

    keep = jnp.where(pl.program_id(0) % tiles_per_seq == 0, 0.0, 1.0)
    wg = wg_ref[...]
    wu = wu_ref[...]
    cw = cw_ref[...]
    ext_ref[0:HALO, :] = _dot(hh_ref[...], wg) * keep
    for r0 in range(0, h_ref.shape[0], rb):
        h = h_ref[r0:r0 + rb, :]
        ext_ref[HALO + r0:HALO + r0 + rb, :] = _dot(h, wg)
        e = ext_ref[r0:r0 + rb + HALO, :]
        conv = ((cw[0:1, :] * pltpu.roll(e, 2, 0) + cw[1:2, :] * pltpu.roll(e, 1, 0)) + cw[2:3, :] * e)[HALO:, :]
        o_ref[r0:r0 + rb, :] = ((conv * jax.nn.sigmoid(conv)) * _dot(h, wu)).astype(o_ref.dtype)


def _ffn_act(x, g, w_up, cw, l, s, tm):
    t, d = x.shape
    dff = cw.shape[2]
    hb = tm // HALO
    nj = dff // FF_CHUNK
    return pl.pallas_call(
        functools.partial(_ffn_act_kernel, tiles_per_seq=s // tm, rb=min(tm, 512)),
        out_shape=jax.ShapeDtypeStruct((t, dff), BF16),
        grid=(t // tm, dff // FF_CHUNK),
        in_specs=[
            pl.BlockSpec((tm, d), lambda i, j: (i, 0)),
            pl.BlockSpec((HALO, d), lambda i, j: (jnp.maximum(i * hb - 1, 0), 0)),
            _layer_spec(g, l),
            _layer_spec(w_up, l, (d, FF_CHUNK), lambda i, j: (0, j)),
            _layer_spec(w_up, l, (d, FF_CHUNK), lambda i, j: (0, nj + j)),
            _layer_spec(cw, l, (cw.shape[1], FF_CHUNK), lambda i, j: (0, j)),
        ],
        out_specs=pl.BlockSpec((tm, FF_CHUNK), lambda i, j: (i, j)),
        scratch_shapes=[pltpu.VMEM((tm, d), BF16), pltpu.VMEM((HALO, d), BF16), pltpu.VMEM((tm + HALO, FF_CHUNK), F32)],
        compiler_params=_params(2),
        name="ffn_act",
    )(x, x, g, w_up, w_up, cw)


def _ffn_down_kernel(x_ref, a_ref, wd_ref, o_ref):
    o_ref[...] = x_ref[...] + _dot(a_ref[...], wd_ref[...])


def _ffn_down(x, act, wd, l, tm):
    t, d = x.shape
    dff = wd.shape[1]
    return pl.pallas_call(
        _ffn_down_kernel,
        out_shape=jax.ShapeDtypeStruct((t, d), F32),
        grid=(t // tm,),
        in_specs=[
            pl.BlockSpec((tm, d), lambda i: (i, 0)),
            pl.BlockSpec((tm, dff), lambda i: (i, 0)),
            _layer_spec(wd, l),
        ],
        out_specs=pl.BlockSpec((tm, d), lambda i: (i, 0)),
        compiler_params=_params(1),
        name="ffn_down",
    )(x, act, wd)


def _overlap_t(nc, ns):
    n_cmp = nc - 1
    cs = np.arange(n_cmp) * CMP_STRIDE
    ce = cs + CMP_BLOCK
    ss = np.arange(ns) * SLC_BLOCK
    se = ss + SLC_BLOCK
    ov = np.clip(np.minimum(ce[:, None], se[None]) - np.maximum(cs[:, None], ss[None]), 0, None)
    out = np.zeros((ns, nc), np.float32)
    out[:, :n_cmp] = (ov.astype(np.float32) / CMP_BLOCK).T
    return out


def _compress_params(pe, w1, w2):
    depth = pe.shape[0]
    half = CMP_BLOCK // 2
    pe2 = jnp.tile(pe.reshape(depth, 2, 2, half, 1, HEAD_DIM), (1, 1, 1, 1, NSA_GROUPS, 1))
    pe2 = pe2.reshape(depth, 2, 2, half * LANES)
    w1r = w1.reshape(depth, 2, 2, half, HEAD_DIM, HEAD_DIM)
    eye = jnp.eye(NSA_GROUPS, dtype=w1.dtype)
    w1bd = jnp.einsum("zjpldo,gh->zjplgdho", w1r, eye).reshape(depth, 2, 2, half * LANES, LANES)
    w2bd = jnp.einsum("zjdo,gh->zjgdho", w2, eye).reshape(depth, 2, LANES, LANES)
    return pe2, w1bd.astype(BF16), w2bd.astype(BF16)


def kernel(x, norm1_g, w_in, conv_a_w, qk_norm_g, cmp_pe, cmp_w1, cmp_w2, pool_w, pool_scale, sgu_norm_g, sgu_w,
           sgu_b, w_branch, w_o, norm2_g, w_up, conv_ff_w, w_down):
    b, s, d = x.shape
    depth = w_in.shape[0]
    t = b * s
    bw = BRANCH_WIDTH
    kvw = 6 * NSA_GROUPS * HEAD_DIM
    ngate = 3 * NSA_HEADS
    o_q = 3 * bw
    o_kv = o_q + bw
    o_gate = o_kv + kvw
    o_pool = o_gate + ngate
    dff = w_down.shape[1]
    ns = s // SLC_BLOCK
    nc = s // CMP_STRIDE
    ovt = jnp.asarray(_overlap_t(nc, ns), BF16)
    xf = x.reshape(t, d)
    tm = min(1024, t)
    tm_wide = min(2048, t)
    w_a = jnp.concatenate([w_in[:, :, :o_q], w_in[:, :, o_pool:]], axis=2).astype(BF16)
    w_n = jnp.concatenate([w_in[:, :, o_q:o_pool], jnp.zeros((depth, d, LANES - ngate), w_in.dtype)],
                          axis=2).astype(BF16)
    row = lambda v: v.reshape(depth, 1, v.shape[-1])
    g2 = lambda k: row(jnp.tile(qk_norm_g[:, k], (1, NSA_GROUPS)))
    g_cmp, g_sel, g_win = g2(1), g2(2), g2(3)
    gq = jnp.broadcast_to(qk_norm_g[:, 0][:, :, None], (depth, HEAD_DIM, QT))
    pe2, w1bd, w2bd = _compress_params(cmp_pe, cmp_w1, cmp_w2)
    n_pool = len(POOL_WINDOWS)
    pw = jnp.einsum("zgio,gh->zgiho", pool_w, jnp.eye(n_pool, dtype=pool_w.dtype)).reshape(depth, bw, bw).astype(BF16)
    sb = jnp.repeat(jnp.swapaxes(sgu_b, 1, 2), bw // SGU_GROUPS, axis=2)
    w_br, w_out, w_dn, w_gu = w_branch.astype(BF16), w_o.astype(BF16), w_down.astype(BF16), w_up.astype(BF16)
    n1, n2, psc, sng = row(norm1_g), row(norm2_g), row(pool_scale), row(sgu_norm_g)

    for l in range(depth):
        za = _norm_matmul(xf, n1, w_a, l, BF16, tm_wide, 2 * bw, "in_proj_a")
        znsa = _norm_matmul(xf, n1, w_n, l, F32, tm, w_n.shape[2], "in_proj_nsa")
        ksa, vst, kwn, vwt = _kv_prep(znsa, g_sel, g_win, l, b, s)
        kc, vct = _compress(znsa, pe2, w1bd, w2bd, g_cmp, l, b, s)
        out_b = _nsa_attn(znsa, gq, l, kc, vct, ovt, ksa, vst, kwn, vwt, b, s)
        xf = _mix_merge(xf, za, out_b, conv_a_w, pw, psc, sng, sgu_w, sb, w_br, w_out, l, s)
        act = _ffn_act(xf, n2, w_gu, conv_ff_w, l, s, tm_wide)
        xf = _ffn_down(xf, act, w_dn, l, tm)
    return xf.reshape(b, s, d)
```

```python
import functools

import jax
import jax.numpy as jnp
import numpy as np
from jax import lax
from jax.experimental import pallas as pl
from jax.experimental.pallas import tpu as pltpu

F32 = jnp.float32
BF16 = jnp.bfloat16

HEAD_DIM = 64
NSA_HEADS = 8
NSA_GROUPS = 2
NSA_REP = NSA_HEADS // NSA_GROUPS
CMP_BLOCK = 32
CMP_STRIDE = 16
SLC_BLOCK = 64
SLC_TOPN = 16
WIN = 512
NEG = -1e30
FORCE = 1e6
RMS_EPS = 1e-6
LOG2E = 1.4426950408889634
POOL_WINDOWS = (2, 4, 8, 16)
SGU_CHUNK = 128
SGU_GROUPS = 4
BRANCH_WIDTH = 512
N_BRANCH = 4

LANES = 128
HALO = 16
QT = 128
SEL_CHUNK = 256
WIN_CHUNK = 128
IN_PROJ_COL_TILES = 4
FF_CHUNK = 1408
RANK_STEP = 16
VROWS = HEAD_DIM + 16
BF16_ROWS = 16
VMEM_LIMIT = 56 * 1024 * 1024


def _params(n_axes):
    return pltpu.CompilerParams(dimension_semantics=("arbitrary",) * n_axes, vmem_limit_bytes=VMEM_LIMIT)


def _dot(a, b):
    return jnp.dot(a, b, preferred_element_type=F32)


def _layer_spec(arr, l, block=None, index=None):
    block = tuple(arr.shape[1:]) if block is None else tuple(block)
    index = index or (lambda *_: (0,) * len(block))
    return pl.BlockSpec((None,) + block, lambda *grid: (l,) + tuple(index(*grid)))


def _rms_bf16(x, g):
    ms = jnp.mean(x * x, axis=-1, keepdims=True)
    return ((x * lax.rsqrt(ms + RMS_EPS)) * g).astype(BF16)


def _erf_gelu(x):
    return 0.5 * x * (1.0 + lax.erf(x * (2.0 ** -0.5)))


def _norm_matmul_kernel(x_ref, g_ref, w_ref, o_ref, h_ref):
    @pl.when(pl.program_id(1) == 0)
    def _():
        h_ref[...] = _rms_bf16(x_ref[...], g_ref[...])

    o_ref[...] = _dot(h_ref[...], w_ref[...]).astype(o_ref.dtype)


def _norm_matmul(x, g, w, l, out_dtype, tm, tn, name):
    t, d = x.shape
    n = w.shape[2]
    return pl.pallas_call(
        _norm_matmul_kernel,
        out_shape=jax.ShapeDtypeStruct((t, n), out_dtype),
        grid=(t // tm, n // tn),
        in_specs=[
            pl.BlockSpec((tm, d), lambda i, j: (i, 0)),
            _layer_spec(g, l),
            _layer_spec(w, l, (d, tn), lambda i, j: (0, j)),
        ],
        out_specs=pl.BlockSpec((tm, tn), lambda i, j: (i, j)),
        scratch_shapes=[pltpu.VMEM((tm, d), BF16)],
        compiler_params=_params(2),
        name=name,
    )(x, g, w)


def _group_rms(x, g):
    lane = lax.broadcasted_iota(jnp.int32, x.shape, 1)
    lo = lane < HEAD_DIM
    x2 = x * x
    s_lo = jnp.sum(jnp.where(lo, x2, 0.0), axis=-1, keepdims=True)
    s_hi = jnp.sum(jnp.where(lo, 0.0, x2), axis=-1, keepdims=True)
    ms = jnp.where(lo, s_lo, s_hi) * (1.0 / HEAD_DIM)
    return (x * lax.rsqrt(ms + RMS_EPS)) * g


def _kv_prep_kernel(ks_ref, vs_ref, kw_ref, vw_ref, gs_ref, gw_ref, ksa_ref, vst_ref, kwn_ref, vwt_ref, *, ts):
    j = pl.program_id(1)
    ksn = _group_rms(ks_ref[...], gs_ref[...]).astype(BF16)
    row = lax.broadcasted_iota(jnp.int32, (ts, LANES), 0) + j * ts
    col = lax.broadcasted_iota(jnp.int32, (ts, LANES), 1)
    onehot = jnp.where(row // SLC_BLOCK == col, 1.0, 0.0).astype(BF16)
    ksa_ref[0, :, 0:LANES] = ksn
    ksa_ref[0, :, LANES:2 * LANES] = onehot
    kwn_ref[0] = _group_rms(kw_ref[...], gw_ref[...]).astype(BF16)
    vs = vs_ref[...]
    for c in range(ts // SEL_CHUNK):
        vst_ref[0, c] = _values_t(vs[c * SEL_CHUNK:(c + 1) * SEL_CHUNK, :])
    vw = vw_ref[...]
    for c in range(ts // WIN_CHUNK):
        vwt_ref[0, c] = _values_t(vw[c * WIN_CHUNK:(c + 1) * WIN_CHUNK, :])


def _values_t(v):
    vt = v.T
    keys = vt.shape[1]
    ones_row = jnp.where(lax.broadcasted_iota(jnp.int32, (VROWS - HEAD_DIM, keys), 0) == 0, 1.0, 0.0)
    parts = []
    for g in range(NSA_GROUPS):
        parts += [vt[g * HEAD_DIM:(g + 1) * HEAD_DIM, :], ones_row]
    return jnp.concatenate(parts, axis=0).astype(BF16)


def _kv_prep(znsa, gs, gw, l, b, s, ts=2048):
    ts = min(ts, s)
    nt = s // ts
    col = lambda c: pl.BlockSpec((ts, LANES), lambda bi, j, c=c: (bi * nt + j, c))
    vec = _layer_spec(gs, l)
    return pl.pallas_call(
        functools.partial(_kv_prep_kernel, ts=ts),
        out_shape=(
            jax.ShapeDtypeStruct((b, s, 2 * LANES), BF16),
            jax.ShapeDtypeStruct((b, s // SEL_CHUNK, NSA_GROUPS * VROWS, SEL_CHUNK), BF16),
            jax.ShapeDtypeStruct((b, s, LANES), BF16),
            jax.ShapeDtypeStruct((b, s // WIN_CHUNK, NSA_GROUPS * VROWS, WIN_CHUNK), BF16),
        ),
        grid=(b, nt),
        in_specs=[col(6), col(7), col(8), col(9), vec, vec],
        out_specs=(
            pl.BlockSpec((1, ts, 2 * LANES), lambda bi, j: (bi, j, 0)),
            pl.BlockSpec((1, ts // SEL_CHUNK, NSA_GROUPS * VROWS, SEL_CHUNK), lambda bi, j: (bi, j, 0, 0)),
            pl.BlockSpec((1, ts, LANES), lambda bi, j: (bi, j, 0)),
            pl.BlockSpec((1, ts // WIN_CHUNK, NSA_GROUPS * VROWS, WIN_CHUNK), lambda bi, j: (bi, j, 0, 0)),
        ),
        compiler_params=_params(2),
        name="kv_prep",
    )(znsa, znsa, znsa, znsa, gs, gw)


def _compress_kernel(kc_ref, vc_ref, pe_ref, w1_ref, w2_ref, g_ref, kco_ref, vct_ref, u_ref, *, nc):
    half = CMP_BLOCK // 2
    for j, src in enumerate((kc_ref, vc_ref)):
        for l in range(half):
            u_ref[:, l * LANES:(l + 1) * LANES] = src[pl.ds(l, nc, stride=CMP_STRIDE), :]
        u = u_ref[...]
        top = _dot((u + pe_ref[j, 0:1, :]).astype(BF16), w1_ref[j, 0])
        bot = _dot((u + pe_ref[j, 1:2, :]).astype(BF16), w1_ref[j, 1])
        pre = top + pltpu.roll(bot, nc - 1, 0)
        hid = pre * jax.nn.sigmoid(pre)
        out = _dot(hid.astype(BF16), w2_ref[j])
        if j == 0:
            kco_ref[0] = _group_rms(out, g_ref[...]).astype(BF16)
        else:
            vct_ref[0] = out.T.astype(BF16)


def _compress(znsa, pe2, w1bd, w2bd, g1, l, b, s):
    nc = s // CMP_STRIDE
    kw = (CMP_BLOCK // 2) * LANES
    return pl.pallas_call(
        functools.partial(_compress_kernel, nc=nc),
        out_shape=(
            jax.ShapeDtypeStruct((b, nc, LANES), BF16),
            jax.ShapeDtypeStruct((b, LANES, nc), BF16),
        ),
        grid=(b,),
        in_specs=[
            pl.BlockSpec((s, LANES), lambda bi: (bi, 4)),
            pl.BlockSpec((s, LANES), lambda bi: (bi, 5)),
            _layer_spec(pe2, l),
            _layer_spec(w1bd, l),
            _layer_spec(w2bd, l),
            _layer_spec(g1, l),
        ],
        out_specs=(
            pl.BlockSpec((1, nc, LANES), lambda bi: (bi, 0, 0)),
            pl.BlockSpec((1, LANES, nc), lambda bi: (bi, 0, 0)),
        ),
        scratch_shapes=[pltpu.VMEM((nc, kw), F32)],
        compiler_params=_params(1),
        name="compress",
    )(znsa, znsa, pe2, w1bd, w2bd, g1)


def _rank_desc(imp, ns):
    nq = imp.shape[1]
    sub = lax.broadcasted_iota(jnp.int32, (8, nq), 0)
    blocks = [imp[8 * a:8 * a + 8, :] for a in range(ns // 8)]
    rank = [jnp.zeros((8, nq), F32) for _ in blocks]
    for k in range(ns):
        ka, kr = divmod(k, 8)
        rk = jnp.broadcast_to(imp[k:k + 1, :], (8, nq))
        for a, blk in enumerate(blocks):
            if a > ka:
                ahead = rk >= blk
            elif a < ka:
                ahead = rk > blk
            else:
                rank[a] = rank[a] + jnp.where(sub > kr, jnp.where(rk >= blk, 1.0, 0.0), jnp.where(rk > blk, 1.0, 0.0))
                continue
            rank[a] = rank[a] + jnp.where(ahead, 1.0, 0.0)
    return jnp.concatenate(rank, axis=0)


def _nsa_attn_kernel(q_ref, gate_ref, gq_ref, kc_ref, vct_ref, ovt_ref, ksa_ref, vst_ref, kw_ref, vwt_ref,
                     o_ref, qa_ref, sa_ref, sb_ref, m_ref, acc_ref, ocmp_ref, owin_ref, *, ns, nc, topn):
    i = pl.program_id(1)
    start = i * QT
    nl = NSA_HEADS * QT
    gl = NSA_REP * QT

    qt = q_ref[...].T.reshape(NSA_HEADS, HEAD_DIM, QT)
    ms = jnp.mean(qt * qt, axis=1, keepdims=True)
    qn = ((qt * lax.rsqrt(ms + RMS_EPS)) * gq_ref[...][None]) * (HEAD_DIM ** -0.5 * LOG2E)
    qa_ref[...] = jnp.zeros(qa_ref.shape, BF16)
    for h in range(NSA_HEADS):
        g = h // NSA_REP
        qa_ref[g * HEAD_DIM:(g + 1) * HEAD_DIM, h * QT:(h + 1) * QT] = qn[h].astype(BF16)

    def tq(shape):
        return start + (lax.broadcasted_iota(jnp.int32, shape, 1) & (QT - 1))

    def krow(shape, base):
        return base + lax.broadcasted_iota(jnp.int32, shape, 0)

    def group_pv(vt, pb):
        rows = vt.shape[0] // NSA_GROUPS
        return jnp.concatenate(
            [_dot(vt[g * rows:(g + 1) * rows, :], pb[:, g * gl:(g + 1) * gl]) for g in range(NSA_GROUPS)],
            axis=1)

    def compressed_and_select():
        qk = qa_ref[0:LANES, :]
        sc = _dot(kc_ref[0], qk)
        cval = krow((nc, nl), 0) * CMP_STRIDE + (CMP_BLOCK - 1) <= tq((nc, nl))
        sc = jnp.where(cval, sc, NEG)
        e = jnp.exp2(sc - jnp.max(sc, axis=0, keepdims=True))
        pc = e / jnp.sum(e, axis=0, keepdims=True)
        pc = jnp.where(tq((1, nl)) >= CMP_BLOCK - 1, pc, 0.0)
        ocmp_ref[...] = group_pv(vct_ref[0], pc.astype(BF16))

        jb = krow((ns, QT), 0)
        jt = (start + lax.broadcasted_iota(jnp.int32, (ns, QT), 1)) // SLC_BLOCK
        forced = (jb == 0) | (jb == jt) | (jb == jt - 1)
        future = jb > jt
        last_block = (start + QT - 1) // SLC_BLOCK
        n_var = max(ns // RANK_STEP, 1)
        ovt = ovt_ref[...]
        for g in range(NSA_GROUPS):
            psum = pc[:, g * gl:g * gl + QT]
            for r in range(1, NSA_REP):
                psum = psum + pc[:, g * gl + r * QT:g * gl + (r + 1) * QT]
            p_hi = psum.astype(BF16)
            rem = psum - p_hi.astype(F32)
            p_mid = rem.astype(BF16)
            p_lo = (rem - p_mid.astype(F32)).astype(BF16)
            imp = (_dot(ovt, p_hi) + _dot(ovt, p_mid)) + _dot(ovt, p_lo)
            imp = jnp.where(forced, FORCE, jnp.where(future, NEG, imp))
            for v in range(n_var):
                rows = RANK_STEP * (v + 1)

                @pl.when(jnp.minimum(last_block // RANK_STEP, n_var - 1) == v)
                def _(g=g, imp=imp, rows=rows):
                    sel = (_rank_desc(imp[0:rows, :], rows) < topn) & jnp.logical_not(future[0:rows, :])
                    bias = jnp.where(sel, 0.0, NEG)
                    if rows < ns:
                        bias = jnp.concatenate([bias, jnp.full((ns - rows, QT), NEG, F32)], axis=0)
                    for r in range(NSA_REP):
                        h = g * NSA_REP + r
                        qa_ref[LANES:LANES + ns, h * QT:(h + 1) * QT] = bias.astype(BF16)

    def col_max(s):
        part = jnp.max(s.reshape(s.shape[0] // BF16_ROWS, BF16_ROWS, nl), axis=0)
        return jnp.max(part.astype(F32), axis=0, keepdims=True)

    def probs(s, m):
        return jnp.exp2(s - m.astype(BF16))

    def normalised(acc):
        return acc[0:HEAD_DIM, :] / acc[HEAD_DIM:HEAD_DIM + 1, :]

    def update(s, vt, first):
        cm = col_max(s)
        if first:
            m_new = cm
        else:
            m_old = m_ref[...]
            m_new = jnp.maximum(m_old, cm)
        pv = group_pv(vt, probs(s, m_new))
        if first:
            acc_ref[...] = pv
        else:
            acc_ref[...] = jnp.exp2(m_old - m_new) * acc_ref[...] + pv
        m_ref[...] = m_new

    n_back = WIN // WIN_CHUNK
    wshape = (WIN_CHUNK, nl)
    local_q = lax.broadcasted_iota(jnp.int32, wshape, 1) & (QT - 1)
    local_k = lax.broadcasted_iota(jnp.int32, wshape, 0)

    def window():
        ss, vts = [], []
        for c in range(n_back + 1):
            ci = i - n_back + c
            cl = jnp.maximum(ci, 0)
            k = kw_ref[0, pl.ds(pl.multiple_of(cl * WIN_CHUNK, WIN_CHUNK), WIN_CHUNK), :]
            s = _dot(k, qa_ref[0:LANES, :])
            if c == 0:
                s = jnp.where(local_k > local_q + jnp.where(ci >= 0, 0, WIN_CHUNK), s, NEG)
            elif c == n_back:
                s = jnp.where(local_k <= local_q, s, NEG)
            else:
                s = s + jnp.where(ci >= 0, 0.0, NEG)
            ss.append(s.astype(BF16))
            vts.append(vwt_ref[0, cl])
        m = col_max(ss[0])
        for s in ss[1:]:
            m = jnp.maximum(m, col_max(s))
        pb = jnp.concatenate([probs(s, m) for s in ss], axis=0)
        owin_ref[...] = normalised(group_pv(jnp.concatenate(vts, axis=1), pb))

    compressed_and_select()
    window()

    def sel_scores(buf, c, diag=False):
        k = ksa_ref[0, pl.ds(pl.multiple_of(c * SEL_CHUNK, SEL_CHUNK), SEL_CHUNK), :]
        s = _dot(k, qa_ref[...])
        if diag:
            shape = (SEL_CHUNK, nl)
            s = jnp.where(krow(shape, c * SEL_CHUNK) <= tq(shape), s, NEG)
        buf[...] = s.astype(BF16)

    def sel_update(buf, c, first):
        update(buf[...], vst_ref[0, c], first)

    cd = start // SEL_CHUNK
    sel_scores(sa_ref, cd, diag=True)
    sel_scores(sb_ref, 0)
    sel_update(sa_ref, cd, True)

    def sel_pair(p, carry):
        c0 = 2 * p
        sel_scores(sa_ref, c0 + 1)
        sel_update(sb_ref, c0, False)
        sel_scores(sb_ref, jnp.minimum(c0 + 2, cd - 1))
        sel_update(sa_ref, c0 + 1, False)
        return carry

    lax.fori_loop(0, cd // 2, sel_pair, 0)

    @pl.when(cd % 2 == 1)
    def _():
        sel_update(sb_ref, cd - 1, False)

    o_sel = normalised(acc_ref[...])
    o_cmp = ocmp_ref[...]
    o_win = owin_ref[...]

    sg = jax.nn.sigmoid(gate_ref[...].T[0:3 * NSA_HEADS, :])
    outs = []
    for h in range(NSA_HEADS):
        lanes = slice(h * QT, (h + 1) * QT)
        outs.append(sg[3 * h:3 * h + 1, :] * o_cmp[:, lanes] + sg[3 * h + 1:3 * h + 2, :] * o_sel[:, lanes]
                    + sg[3 * h + 2:3 * h + 3, :] * o_win[:, lanes])
    o_ref[...] = jnp.concatenate(outs, axis=0).T.astype(o_ref.dtype)


def _nsa_attn(znsa, gq, l, kc, vct, ovt, ksa, vst, kwn, vwt, b, s):
    ns = s // SLC_BLOCK
    nc = s // CMP_STRIDE
    nq = s // QT
    nl = NSA_HEADS * QT
    width = NSA_HEADS * HEAD_DIM
    kern = functools.partial(_nsa_attn_kernel, ns=ns, nc=nc, topn=min(SLC_TOPN, ns))
    return pl.pallas_call(
        kern,
        out_shape=jax.ShapeDtypeStruct((b * s, width), BF16),
        grid=(b, nq),
        in_specs=[
            pl.BlockSpec((QT, width), lambda bi, i: (bi * nq + i, 0)),
            pl.BlockSpec((QT, LANES), lambda bi, i: (bi * nq + i, 10)),
            _layer_spec(gq, l),
            pl.BlockSpec((1, nc, LANES), lambda bi, i: (bi, 0, 0)),
            pl.BlockSpec((1, LANES, nc), lambda bi, i: (bi, 0, 0)),
            pl.BlockSpec((ns, nc), lambda bi, i: (0, 0)),
            pl.BlockSpec((1, s, 2 * LANES), lambda bi, i: (bi, 0, 0)),
            pl.BlockSpec((1, s // SEL_CHUNK, NSA_GROUPS * VROWS, SEL_CHUNK), lambda bi, i: (bi, 0, 0, 0)),
            pl.BlockSpec((1, s, LANES), lambda bi, i: (bi, 0, 0)),
            pl.BlockSpec((1, s // WIN_CHUNK, NSA_GROUPS * VROWS, WIN_CHUNK), lambda bi, i: (bi, 0, 0, 0)),
        ],
        out_specs=pl.BlockSpec((QT, width), lambda bi, i: (bi * nq + i, 0)),
        scratch_shapes=[
            pltpu.VMEM((2 * LANES, nl), BF16),
            pltpu.VMEM((SEL_CHUNK, nl), BF16),
            pltpu.VMEM((SEL_CHUNK, nl), BF16),
            pltpu.VMEM((1, nl), F32),
            pltpu.VMEM((VROWS, nl), F32),
            pltpu.VMEM((HEAD_DIM, nl), F32),
            pltpu.VMEM((HEAD_DIM, nl), F32),
        ],
        compiler_params=_params(2),
        name="nsa_attn",
    )(znsa, znsa, gq, kc, vct, ovt, ksa, vst, kwn, vwt)


def _mix_kernel(x_ref, ab_ref, ac_ref, ax_ref, ach_ref, axh_ref, p_ref, ph_ref, sgu_ref, mg0_ref, mg1_ref, mg2_ref,
                mg3_ref, ob_ref, cw_ref, pw_ref, psc_ref, sng_ref, sw_ref, sb_ref, wbr_ref, wo_ref, o_ref, ext_ref,
                ext2_ref, *, tm, tiles_per_seq):
    ti = pl.program_id(0) % tiles_per_seq
    keep = jnp.where(ti == 0, 0.0, 1.0)

    ext_ref[0:HALO, :] = (ach_ref[...].astype(F32) * axh_ref[...].astype(F32)) * keep
    ext_ref[HALO:, :] = ac_ref[...].astype(F32) * ax_ref[...].astype(F32)
    e = ext_ref[...]
    cw = cw_ref[...]
    conv = (cw[0:1, :] * pltpu.roll(e, 2, 0) + cw[1:2, :] * pltpu.roll(e, 1, 0)) + cw[2:3, :] * e
    out_a = ab_ref[...].astype(F32) * conv[HALO:, :]

    p = p_ref[...].astype(F32)
    ext2_ref[0:HALO, :] = ph_ref[...].astype(F32) * keep
    ext2_ref[HALO:, :] = p
    e = ext2_ref[...]
    gw = BRANCH_WIDTH // len(POOL_WINDOWS)
    cnt = (ti * tm + 1 + lax.broadcasted_iota(jnp.int32, (tm, gw), 0)).astype(F32)
    groups = []
    for gi, w in enumerate(POOL_WINDOWS):
        acc = e[:, gi * gw:(gi + 1) * gw]
        span = 1
        while span < w:
            acc = acc + pltpu.roll(acc, span, 0)
            span *= 2
        groups.append(acc[HALO:, :] / jnp.minimum(cnt, float(w)))
    pooled = jnp.concatenate(groups, axis=1) - p
    out_c = _dot(pooled.astype(BF16), pw_ref[...]) * psc_ref[...]

    z = _erf_gelu(sgu_ref[...].astype(F32))
    u = z[:, :BRANCH_WIDTH]
    v = z[:, BRANCH_WIDTH:]
    v = (v * lax.rsqrt(jnp.mean(v * v, axis=-1, keepdims=True) + RMS_EPS)) * sng_ref[...]
    vb = v.astype(BF16)
    nchunk = tm // SGU_CHUNK
    gd = BRANCH_WIDTH // SGU_GROUPS
    tri = (lax.broadcasted_iota(jnp.int32, (SGU_CHUNK, SGU_CHUNK), 0)
           >= lax.broadcasted_iota(jnp.int32, (SGU_CHUNK, SGU_CHUNK), 1))
    mixed_g = []
    for g in range(SGU_GROUPS):
        wg = jnp.where(tri, sw_ref[g], 0.0).astype(BF16)
        rhs = jnp.concatenate(
            [vb[c * SGU_CHUNK:(c + 1) * SGU_CHUNK, g * gd:(g + 1) * gd] for c in range(nchunk)], axis=1)
        res = _dot(wg, rhs)
        mixed_g.append(jnp.concatenate([res[:, c * gd:(c + 1) * gd] for c in range(nchunk)], axis=0))
    bias = jnp.concatenate([sb_ref[...]] * nchunk, axis=0)
    out_d = u * (jnp.concatenate(mixed_g, axis=1) + bias)

    merged = jax.nn.sigmoid(mg0_ref[...].astype(F32)) * _dot(out_a.astype(BF16), wbr_ref[0])
    merged = merged + jax.nn.sigmoid(mg1_ref[...].astype(F32)) * _dot(ob_ref[...], wbr_ref[1])
    merged = merged + jax.nn.sigmoid(mg2_ref[...].astype(F32)) * _dot(out_c.astype(BF16), wbr_ref[2])
    merged = merged + jax.nn.sigmoid(mg3_ref[...].astype(F32)) * _dot(out_d.astype(BF16), wbr_ref[3])
    o_ref[...] = x_ref[...] + _dot(merged.astype(BF16), wo_ref[...])


def _mix_merge(x, za, out_b, cw, pw, psc, sng, sw, sb, wbr, wo, l, s, tm=512):
    t, d = x.shape
    bw = BRANCH_WIDTH
    tps = s // tm
    hb = tm // HALO
    tile = lambda c: pl.BlockSpec((tm, bw), lambda i, c=c: (i, c))
    halo = lambda c: pl.BlockSpec((HALO, bw), lambda i, c=c: (jnp.maximum(i * hb - 1, 0), c))
    wide = lambda c: pl.BlockSpec((tm, d), lambda i, c=c: (i, c))
    full = lambda a: _layer_spec(a, l)
    consts = (cw, pw, psc, sng, sw, sb, wbr, wo)
    return pl.pallas_call(
        functools.partial(_mix_kernel, tm=tm, tiles_per_seq=tps),
        out_shape=jax.ShapeDtypeStruct((t, d), F32),
        grid=(t // tm,),
        in_specs=[wide(0), tile(0), tile(1), tile(2), halo(1), halo(2), tile(3), halo(3), wide(2),
                  wide(3), wide(4), wide(5), wide(6), tile(0)] + [full(a) for a in consts],
        out_specs=wide(0),
        scratch_shapes=[pltpu.VMEM((tm + HALO, bw), F32), pltpu.VMEM((tm + HALO, bw), F32)],
        compiler_params=_params(1),
        name="mix_merge",
    )(x, za, za, za, za, za, za, za, za, za, za, za, za, out_b, *consts)


def _ffn_act_kernel(x_ref, xh_ref, g_ref, wg_ref, wu_ref, cw_ref, o_ref, h_ref, hh_ref, ext_ref, *,
                    tiles_per_seq, rb):
    @pl.when(pl.program_id(1) == 0)
    def _():
        h_ref[...] = _rms_bf16(x_ref[...], g_ref[...])
        hh_ref[...] = _rms_bf16(xh_ref[...], g_ref[...])

    keep = jnp.where(pl.program_id(0) % tiles_per_seq == 0, 0.0, 1.0)
    wg = wg_ref[...]
    wu = wu_ref[...]
    cw = cw_ref[...]
    ext_ref[0:HALO, :] = _dot(hh_ref[...], wg) * keep
    for r0 in range(0, h_ref.shape[0], rb):
        h = h_ref[r0:r0 + rb, :]
        ext_ref[HALO + r0:HALO + r0 + rb, :] = _dot(h, wg)
        e = ext_ref[r0:r0 + rb + HALO, :]
        conv = ((cw[0:1, :] * pltpu.roll(e, 2, 0) + cw[1:2, :] * pltpu.roll(e, 1, 0)) + cw[2:3, :] * e)[HALO:, :]
        o_ref[r0:r0 + rb, :] = ((conv * jax.nn.sigmoid(conv)) * _dot(h, wu)).astype(o_ref.dtype)


def _ffn_act(x, g, w_up, cw, l, s, tm):
    t, d = x.shape
    dff = cw.shape[2]
    hb = tm // HALO
    nj = dff // FF_CHUNK
    return pl.pallas_call(
        functools.partial(_ffn_act_kernel, tiles_per_seq=s // tm, rb=min(tm, 512)),
        out_shape=jax.ShapeDtypeStruct((t, dff), BF16),
        grid=(t // tm, dff // FF_CHUNK),
        in_specs=[
            pl.BlockSpec((tm, d), lambda i, j: (i, 0)),
            pl.BlockSpec((HALO, d), lambda i, j: (jnp.maximum(i * hb - 1, 0), 0)),
            _layer_spec(g, l),
            _layer_spec(w_up, l, (d, FF_CHUNK), lambda i, j: (0, j)),
            _layer_spec(w_up, l, (d, FF_CHUNK), lambda i, j: (0, nj + j)),
            _layer_spec(cw, l, (cw.shape[1], FF_CHUNK), lambda i, j: (0, j)),
        ],
        out_specs=pl.BlockSpec((tm, FF_CHUNK), lambda i, j: (i, j)),
        scratch_shapes=[pltpu.VMEM((tm, d), BF16), pltpu.VMEM((HALO, d), BF16), pltpu.VMEM((tm + HALO, FF_CHUNK), F32)],
        compiler_params=_params(2),
        name="ffn_act",
    )(x, x, g, w_up, w_up, cw)


def _ffn_down_kernel(x_ref, a_ref, wd_ref, o_ref):
    o_ref[...] = x_ref[...] + _dot(a_ref[...], wd_ref[...])


def _ffn_down(x, act, wd, l, tm):
    t, d = x.shape
    dff = wd.shape[1]
    return pl.pallas_call(
        _ffn_down_kernel,
        out_shape=jax.ShapeDtypeStruct((t, d), F32),
        grid=(t // tm,),
        in_specs=[
            pl.BlockSpec((tm, d), lambda i: (i, 0)),
            pl.BlockSpec((tm, dff), lambda i: (i, 0)),
            _layer_spec(wd, l),
        ],
        out_specs=pl.BlockSpec((tm, d), lambda i: (i, 0)),
        compiler_params=_params(1),
        name="ffn_down",
    )(x, act, wd)


def _overlap_t(nc, ns):
    n_cmp = nc - 1
    cs = np.arange(n_cmp) * CMP_STRIDE
    ce = cs + CMP_BLOCK
    ss = np.arange(ns) * SLC_BLOCK
    se = ss + SLC_BLOCK
    ov = np.clip(np.minimum(ce[:, None], se[None]) - np.maximum(cs[:, None], ss[None]), 0, None)
    out = np.zeros((ns, nc), np.float32)
    out[:, :n_cmp] = (ov.astype(np.float32) / CMP_BLOCK).T
    return out


def _compress_params(pe, w1, w2):
    depth = pe.shape[0]
    half = CMP_BLOCK // 2
    pe2 = jnp.tile(pe.reshape(depth, 2, 2, half, 1, HEAD_DIM), (1, 1, 1, 1, NSA_GROUPS, 1))
    pe2 = pe2.reshape(depth, 2, 2, half * LANES)
    w1r = w1.astype(BF16).reshape(depth, 2, 2, half, HEAD_DIM, HEAD_DIM)
    zero = jnp.zeros_like(w1r)
    w1bd = jnp.stack([jnp.concatenate([w1r, zero], axis=-1), jnp.concatenate([zero, w1r], axis=-1)], axis=4)
    w1bd = w1bd.reshape(depth, 2, 2, half * LANES, LANES)
    eye = jnp.eye(NSA_GROUPS, dtype=w2.dtype)
    w2bd = jnp.einsum("zjdo,gh->zjgdho", w2, eye).reshape(depth, 2, LANES, LANES)
    return pe2, w1bd, w2bd.astype(BF16)


def kernel(x, norm1_g, w_in, conv_a_w, qk_norm_g, cmp_pe, cmp_w1, cmp_w2, pool_w, pool_scale, sgu_norm_g, sgu_w,
           sgu_b, w_branch, w_o, norm2_g, w_up, conv_ff_w, w_down):
    b, s, d = x.shape
    depth = w_in.shape[0]
    t = b * s
    bw = BRANCH_WIDTH
    kvw = 6 * NSA_GROUPS * HEAD_DIM
    ngate = 3 * NSA_HEADS
    o_q = 3 * bw
    o_kv = o_q + bw
    o_gate = o_kv + kvw
    o_pool = o_gate + ngate
    dff = w_down.shape[1]
    ns = s // SLC_BLOCK
    nc = s // CMP_STRIDE
    ovt = jnp.asarray(_overlap_t(nc, ns), BF16)
    xf = x.reshape(t, d)
    tm = min(1024, t)
    tm_wide = min(2048, t)
    w_a = jnp.concatenate([w_in[:, :, :o_q], w_in[:, :, o_pool:]], axis=2).astype(BF16)
    w_n = jnp.concatenate([w_in[:, :, o_q:o_pool], jnp.zeros((depth, d, LANES - ngate), w_in.dtype)],
                          axis=2).astype(BF16)
    row = lambda v: v.reshape(depth, 1, v.shape[-1])
    g2 = lambda k: row(jnp.tile(qk_norm_g[:, k], (1, NSA_GROUPS)))
    g_cmp, g_sel, g_win = g2(1), g2(2), g2(3)
    gq = jnp.broadcast_to(qk_norm_g[:, 0][:, :, None], (depth, HEAD_DIM, QT))
    pe2, w1bd, w2bd = _compress_params(cmp_pe, cmp_w1, cmp_w2)
    n_pool = len(POOL_WINDOWS)
    pw = jnp.einsum("zgio,gh->zgiho", pool_w, jnp.eye(n_pool, dtype=pool_w.dtype)).reshape(depth, bw, bw).astype(BF16)
    sb = jnp.repeat(jnp.swapaxes(sgu_b, 1, 2), bw // SGU_GROUPS, axis=2)
    w_br, w_out, w_dn, w_gu = w_branch.astype(BF16), w_o.astype(BF16), w_down.astype(BF16), w_up.astype(BF16)
    n1, n2, psc, sng = row(norm1_g), row(norm2_g), row(pool_scale), row(sgu_norm_g)

    for l in range(depth):
        za = _norm_matmul(xf, n1, w_a, l, BF16, tm_wide, w_a.shape[2] // IN_PROJ_COL_TILES, "in_proj_a")
        znsa = _norm_matmul(xf, n1, w_n, l, F32, tm, w_n.shape[2], "in_proj_nsa")
        ksa, vst, kwn, vwt = _kv_prep(znsa, g_sel, g_win, l, b, s)
        kc, vct = _compress(znsa, pe2, w1bd, w2bd, g_cmp, l, b, s)
        out_b = _nsa_attn(znsa, gq, l, kc, vct, ovt, ksa, vst, kwn, vwt, b, s)
        xf = _mix_merge(xf, za, out_b, conv_a_w, pw, psc, sng, sgu_w, sb, w_br, w_out, l, s)
        act = _ffn_act(xf, n2, w_gu, conv_ff_w, l, s, tm)
        xf = _ffn_down(xf, act, w_dn, l, tm)
    return xf.reshape(b, s, d)
```

```python
import functools

import jax
import jax.numpy as jnp
import numpy as np
from jax import lax
from jax.experimental import pallas as pl
from jax.experimental.pallas import tpu as pltpu

F32 = jnp.float32
BF16 = jnp.bfloat16

HEAD_DIM = 64
NSA_HEADS = 8
NSA_GROUPS = 2
NSA_REP = NSA_HEADS // NSA_GROUPS
CMP_BLOCK = 32
CMP_STRIDE = 16
SLC_BLOCK = 64
SLC_TOPN = 16
WIN = 512
NEG = -1e30
FORCE = 1e6
RMS_EPS = 1e-6
LOG2E = 1.4426950408889634
POOL_WINDOWS = (2, 4, 8, 16)
SGU_CHUNK = 128
SGU_GROUPS = 4
BRANCH_WIDTH = 512
N_BRANCH = 4

LANES = 128
HALO = 16
QT = 256
SEL_CHUNK = 256
WIN_CHUNK = 128
IN_PROJ_COL_TILES = 4
FF_CHUNK = 1408
RANK_STEP = 16
VROWS = HEAD_DIM + 16
BF16_ROWS = 16
VMEM_LIMIT = 56 * 1024 * 1024


def _params(n_axes):
    return pltpu.CompilerParams(dimension_semantics=("arbitrary",) * n_axes, vmem_limit_bytes=VMEM_LIMIT)


def _dot(a, b):
    return jnp.dot(a, b, preferred_element_type=F32)


def _layer_spec(arr, l, block=None, index=None):
    block = tuple(arr.shape[1:]) if block is None else tuple(block)
    index = index or (lambda *_: (0,) * len(block))
    return pl.BlockSpec((None,) + block, lambda *grid: (l,) + tuple(index(*grid)))


def _rms_bf16(x, g):
    ms = jnp.mean(x * x, axis=-1, keepdims=True)
    return ((x * lax.rsqrt(ms + RMS_EPS)) * g).astype(BF16)


def _erf_gelu(x):
    return 0.5 * x * (1.0 + lax.erf(x * (2.0 ** -0.5)))


def _norm_matmul_kernel(x_ref, g_ref, w_ref, o_ref, h_ref):
    @pl.when(pl.program_id(1) == 0)
    def _():
        h_ref[...] = _rms_bf16(x_ref[...], g_ref[...])

    o_ref[...] = _dot(h_ref[...], w_ref[...]).astype(o_ref.dtype)


def _norm_matmul(x, g, w, l, out_dtype, tm, tn, name):
    t, d = x.shape
    n = w.shape[2]
    return pl.pallas_call(
        _norm_matmul_kernel,
        out_shape=jax.ShapeDtypeStruct((t, n), out_dtype),
        grid=(t // tm, n // tn),
        in_specs=[
            pl.BlockSpec((tm, d), lambda i, j: (i, 0)),
            _layer_spec(g, l),
            _layer_spec(w, l, (d, tn), lambda i, j: (0, j)),
        ],
        out_specs=pl.BlockSpec((tm, tn), lambda i, j: (i, j)),
        scratch_shapes=[pltpu.VMEM((tm, d), BF16)],
        compiler_params=_params(2),
        name=name,
    )(x, g, w)


def _group_rms(x, g):
    lane = lax.broadcasted_iota(jnp.int32, x.shape, 1)
    lo = lane < HEAD_DIM
    x2 = x * x
    s_lo = jnp.sum(jnp.where(lo, x2, 0.0), axis=-1, keepdims=True)
    s_hi = jnp.sum(jnp.where(lo, 0.0, x2), axis=-1, keepdims=True)
    ms = jnp.where(lo, s_lo, s_hi) * (1.0 / HEAD_DIM)
    return (x * lax.rsqrt(ms + RMS_EPS)) * g


def _kv_prep_kernel(ks_ref, vs_ref, kw_ref, vw_ref, gs_ref, gw_ref, ksa_ref, vst_ref, kwn_ref, vwt_ref, *, ts):
    j = pl.program_id(1)
    ksn = _group_rms(ks_ref[...], gs_ref[...]).astype(BF16)
    row = lax.broadcasted_iota(jnp.int32, (ts, LANES), 0) + j * ts
    col = lax.broadcasted_iota(jnp.int32, (ts, LANES), 1)
    onehot = jnp.where(row // SLC_BLOCK == col, 1.0, 0.0).astype(BF16)
    ksa_ref[0, :, 0:LANES] = ksn
    ksa_ref[0, :, LANES:2 * LANES] = onehot
    kwn_ref[0] = _group_rms(kw_ref[...], gw_ref[...]).astype(BF16)
    vs = vs_ref[...]
    for c in range(ts // SEL_CHUNK):
        vst_ref[0, c] = _values_t(vs[c * SEL_CHUNK:(c + 1) * SEL_CHUNK, :])
    vw = vw_ref[...]
    for c in range(ts // WIN_CHUNK):
        vwt_ref[0, c] = _values_t(vw[c * WIN_CHUNK:(c + 1) * WIN_CHUNK, :])


def _values_t(v):
    vt = v.T
    keys = vt.shape[1]
    ones_row = jnp.where(lax.broadcasted_iota(jnp.int32, (VROWS - HEAD_DIM, keys), 0) == 0, 1.0, 0.0)
    parts = []
    for g in range(NSA_GROUPS):
        parts += [vt[g * HEAD_DIM:(g + 1) * HEAD_DIM, :], ones_row]
    return jnp.concatenate(parts, axis=0).astype(BF16)


def _kv_prep(znsa, gs, gw, l, b, s, ts=2048):
    ts = min(ts, s)
    nt = s // ts
    col = lambda c: pl.BlockSpec((ts, LANES), lambda bi, j, c=c: (bi * nt + j, c))
    vec = _layer_spec(gs, l)
    return pl.pallas_call(
        functools.partial(_kv_prep_kernel, ts=ts),
        out_shape=(
            jax.ShapeDtypeStruct((b, s, 2 * LANES), BF16),
            jax.ShapeDtypeStruct((b, s // SEL_CHUNK, NSA_GROUPS * VROWS, SEL_CHUNK), BF16),
            jax.ShapeDtypeStruct((b, s, LANES), BF16),
            jax.ShapeDtypeStruct((b, s // WIN_CHUNK, NSA_GROUPS * VROWS, WIN_CHUNK), BF16),
        ),
        grid=(b, nt),
        in_specs=[col(6), col(7), col(8), col(9), vec, vec],
        out_specs=(
            pl.BlockSpec((1, ts, 2 * LANES), lambda bi, j: (bi, j, 0)),
            pl.BlockSpec((1, ts // SEL_CHUNK, NSA_GROUPS * VROWS, SEL_CHUNK), lambda bi, j: (bi, j, 0, 0)),
            pl.BlockSpec((1, ts, LANES), lambda bi, j: (bi, j, 0)),
            pl.BlockSpec((1, ts // WIN_CHUNK, NSA_GROUPS * VROWS, WIN_CHUNK), lambda bi, j: (bi, j, 0, 0)),
        ),
        compiler_params=_params(2),
        name="kv_prep",
    )(znsa, znsa, znsa, znsa, gs, gw)


def _compress_kernel(kc_ref, vc_ref, pe_ref, w1_ref, w2_ref, g_ref, kco_ref, vct_ref, u_ref, *, nc):
    half = CMP_BLOCK // 2
    for j, src in enumerate((kc_ref, vc_ref)):
        for l in range(half):
            u_ref[:, l * LANES:(l + 1) * LANES] = src[pl.ds(l, nc, stride=CMP_STRIDE), :]
        u = u_ref[...]
        top = _dot((u + pe_ref[j, 0:1, :]).astype(BF16), w1_ref[j, 0])
        bot = _dot((u + pe_ref[j, 1:2, :]).astype(BF16), w1_ref[j, 1])
        pre = top + pltpu.roll(bot, nc - 1, 0)
        hid = pre * jax.nn.sigmoid(pre)
        out = _dot(hid.astype(BF16), w2_ref[j])
        if j == 0:
            kco_ref[0] = _group_rms(out, g_ref[...]).astype(BF16)
        else:
            vct_ref[0] = out.T.astype(BF16)


def _compress(znsa, pe2, w1bd, w2bd, g1, l, b, s):
    nc = s // CMP_STRIDE
    kw = (CMP_BLOCK // 2) * LANES
    return pl.pallas_call(
        functools.partial(_compress_kernel, nc=nc),
        out_shape=(
            jax.ShapeDtypeStruct((b, nc, LANES), BF16),
            jax.ShapeDtypeStruct((b, LANES, nc), BF16),
        ),
        grid=(b,),
        in_specs=[
            pl.BlockSpec((s, LANES), lambda bi: (bi, 4)),
            pl.BlockSpec((s, LANES), lambda bi: (bi, 5)),
            _layer_spec(pe2, l),
            _layer_spec(w1bd, l),
            _layer_spec(w2bd, l),
            _layer_spec(g1, l),
        ],
        out_specs=(
            pl.BlockSpec((1, nc, LANES), lambda bi: (bi, 0, 0)),
            pl.BlockSpec((1, LANES, nc), lambda bi: (bi, 0, 0)),
        ),
        scratch_shapes=[pltpu.VMEM((nc, kw), F32)],
        compiler_params=_params(1),
        name="compress",
    )(znsa, znsa, pe2, w1bd, w2bd, g1)


def _rank_desc(imp, ns):
    nq = imp.shape[1]
    sub = lax.broadcasted_iota(jnp.int32, (8, nq), 0)
    blocks = [imp[8 * a:8 * a + 8, :] for a in range(ns // 8)]
    rank = [jnp.zeros((8, nq), F32) for _ in blocks]
    for k in range(ns):
        ka, kr = divmod(k, 8)
        rk = jnp.broadcast_to(imp[k:k + 1, :], (8, nq))
        for a, blk in enumerate(blocks):
            if a > ka:
                ahead = rk >= blk
            elif a < ka:
                ahead = rk > blk
            else:
                rank[a] = rank[a] + jnp.where(sub > kr, jnp.where(rk >= blk, 1.0, 0.0), jnp.where(rk > blk, 1.0, 0.0))
                continue
            rank[a] = rank[a] + jnp.where(ahead, 1.0, 0.0)
    return jnp.concatenate(rank, axis=0)


def _nsa_attn_kernel(q_ref, gate_ref, gq_ref, kc_ref, vct_ref, ovt_ref, ksa_ref, vst_ref, kw_ref, vwt_ref,
                     o_ref, qa_ref, sa_ref, sb_ref, m_ref, acc_ref, ocmp_ref, owin_ref, *, ns, nc, topn):
    i = pl.program_id(1)
    start = i * QT
    nl = NSA_HEADS * QT
    gl = NSA_REP * QT

    qt = q_ref[...].T.reshape(NSA_HEADS, HEAD_DIM, QT)
    ms = jnp.mean(qt * qt, axis=1, keepdims=True)
    qn = ((qt * lax.rsqrt(ms + RMS_EPS)) * gq_ref[...][None]) * (HEAD_DIM ** -0.5 * LOG2E)
    qa_ref[...] = jnp.zeros(qa_ref.shape, BF16)
    for h in range(NSA_HEADS):
        g = h // NSA_REP
        qa_ref[g * HEAD_DIM:(g + 1) * HEAD_DIM, h * QT:(h + 1) * QT] = qn[h].astype(BF16)

    def tq(shape):
        return start + (lax.broadcasted_iota(jnp.int32, shape, 1) & (QT - 1))

    def krow(shape, base):
        return base + lax.broadcasted_iota(jnp.int32, shape, 0)

    def group_pv(vt, pb):
        rows = vt.shape[0] // NSA_GROUPS
        return jnp.concatenate(
            [_dot(vt[g * rows:(g + 1) * rows, :], pb[:, g * gl:(g + 1) * gl]) for g in range(NSA_GROUPS)],
            axis=1)

    def compressed_and_select():
        qk = qa_ref[0:LANES, :]
        sc = _dot(kc_ref[0], qk)
        cval = krow((nc, nl), 0) * CMP_STRIDE + (CMP_BLOCK - 1) <= tq((nc, nl))
        sc = jnp.where(cval, sc, NEG)
        e = jnp.exp2(sc - jnp.max(sc, axis=0, keepdims=True))
        pc = e / jnp.sum(e, axis=0, keepdims=True)
        pc = jnp.where(tq((1, nl)) >= CMP_BLOCK - 1, pc, 0.0)
        ocmp_ref[...] = group_pv(vct_ref[0], pc.astype(BF16))

        jb = krow((ns, QT), 0)
        jt = (start + lax.broadcasted_iota(jnp.int32, (ns, QT), 1)) // SLC_BLOCK
        forced = (jb == 0) | (jb == jt) | (jb == jt - 1)
        future = jb > jt
        last_block = (start + QT - 1) // SLC_BLOCK
        n_var = max(ns // RANK_STEP, 1)
        ovt = ovt_ref[...]
        for g in range(NSA_GROUPS):
            psum = pc[:, g * gl:g * gl + QT]
            for r in range(1, NSA_REP):
                psum = psum + pc[:, g * gl + r * QT:g * gl + (r + 1) * QT]
            p_hi = psum.astype(BF16)
            rem = psum - p_hi.astype(F32)
            p_mid = rem.astype(BF16)
            p_lo = (rem - p_mid.astype(F32)).astype(BF16)
            imp = (_dot(ovt, p_hi) + _dot(ovt, p_mid)) + _dot(ovt, p_lo)
            imp = jnp.where(forced, FORCE, jnp.where(future, NEG, imp))
            for v in range(n_var):
                rows = RANK_STEP * (v + 1)

                @pl.when(jnp.minimum(last_block // RANK_STEP, n_var - 1) == v)
                def _(g=g, imp=imp, rows=rows):
                    sel = (_rank_desc(imp[0:rows, :], rows) < topn) & jnp.logical_not(future[0:rows, :])
                    bias = jnp.where(sel, 0.0, NEG)
                    if rows < ns:
                        bias = jnp.concatenate([bias, jnp.full((ns - rows, QT), NEG, F32)], axis=0)
                    for r in range(NSA_REP):
                        h = g * NSA_REP + r
                        qa_ref[LANES:LANES + ns, h * QT:(h + 1) * QT] = bias.astype(BF16)

    def col_max(s):
        part = jnp.max(s.reshape(s.shape[0] // BF16_ROWS, BF16_ROWS, nl), axis=0)
        return jnp.max(part.astype(F32), axis=0, keepdims=True)

    def probs(s, m):
        return jnp.exp2(s - m.astype(BF16))

    def normalised(acc):
        return acc[0:HEAD_DIM, :] / acc[HEAD_DIM:HEAD_DIM + 1, :]

    def update(s, vt, first):
        cm = col_max(s)
        if first:
            m_new = cm
        else:
            m_old = m_ref[...]
            m_new = jnp.maximum(m_old, cm)
        pv = group_pv(vt, probs(s, m_new))
        if first:
            acc_ref[...] = pv
        else:
            acc_ref[...] = jnp.exp2(m_old - m_new) * acc_ref[...] + pv
        m_ref[...] = m_new

    n_back = WIN // WIN_CHUNK
    n_own = QT // WIN_CHUNK
    wshape = (WIN_CHUNK, nl)
    local_q = lax.broadcasted_iota(jnp.int32, wshape, 1) & (QT - 1)
    local_k = lax.broadcasted_iota(jnp.int32, wshape, 0)

    def window():
        ss, vts = [], []
        for c in range(n_back + n_own):
            off = c * WIN_CHUNK
            ci = i * n_own - n_back + c
            cl = jnp.maximum(ci, 0)
            k = kw_ref[0, pl.ds(pl.multiple_of(cl * WIN_CHUNK, WIN_CHUNK), WIN_CHUNK), :]
            s = _dot(k, qa_ref[0:LANES, :])
            if off < QT:
                s = jnp.where(local_k + off > local_q + jnp.where(ci >= 0, 0, QT + WIN_CHUNK), s, NEG)
            elif off + WIN_CHUNK > WIN:
                s = jnp.where(local_k + (off - WIN) <= local_q, s, NEG)
            elif c < n_back:
                s = s + jnp.where(ci >= 0, 0.0, NEG)
            ss.append(s.astype(BF16))
            vts.append(vwt_ref[0, cl])
        m = col_max(ss[0])
        for s in ss[1:]:
            m = jnp.maximum(m, col_max(s))
        pb = jnp.concatenate([probs(s, m) for s in ss], axis=0)
        owin_ref[...] = normalised(group_pv(jnp.concatenate(vts, axis=1), pb))

    compressed_and_select()
    window()

    def sel_scores(buf, c, diag=False):
        k = ksa_ref[0, pl.ds(pl.multiple_of(c * SEL_CHUNK, SEL_CHUNK), SEL_CHUNK), :]
        s = _dot(k, qa_ref[...])
        if diag:
            shape = (SEL_CHUNK, nl)
            s = jnp.where(krow(shape, c * SEL_CHUNK) <= tq(shape), s, NEG)
        buf[...] = s.astype(BF16)

    def sel_update(buf, c, first):
        update(buf[...], vst_ref[0, c], first)

    cd = start // SEL_CHUNK
    sel_scores(sa_ref, cd, diag=True)
    sel_scores(sb_ref, 0)
    sel_update(sa_ref, cd, True)

    def sel_pair(p, carry):
        c0 = 2 * p
        sel_scores(sa_ref, c0 + 1)
        sel_update(sb_ref, c0, False)
        sel_scores(sb_ref, jnp.minimum(c0 + 2, cd - 1))
        sel_update(sa_ref, c0 + 1, False)
        return carry

    lax.fori_loop(0, cd // 2, sel_pair, 0)

    @pl.when(cd % 2 == 1)
    def _():
        sel_update(sb_ref, cd - 1, False)

    o_sel = normalised(acc_ref[...])
    o_cmp = ocmp_ref[...]
    o_win = owin_ref[...]

    sg = jax.nn.sigmoid(gate_ref[...].T[0:3 * NSA_HEADS, :])
    outs = []
    for h in range(NSA_HEADS):
        lanes = slice(h * QT, (h + 1) * QT)
        outs.append(sg[3 * h:3 * h + 1, :] * o_cmp[:, lanes] + sg[3 * h + 1:3 * h + 2, :] * o_sel[:, lanes]
                    + sg[3 * h + 2:3 * h + 3, :] * o_win[:, lanes])
    o_ref[...] = jnp.concatenate(outs, axis=0).T.astype(o_ref.dtype)


def _nsa_attn(znsa, gq, l, kc, vct, ovt, ksa, vst, kwn, vwt, b, s):
    ns = s // SLC_BLOCK
    nc = s // CMP_STRIDE
    nq = s // QT
    nl = NSA_HEADS * QT
    width = NSA_HEADS * HEAD_DIM
    kern = functools.partial(_nsa_attn_kernel, ns=ns, nc=nc, topn=min(SLC_TOPN, ns))
    return pl.pallas_call(
        kern,
        out_shape=jax.ShapeDtypeStruct((b * s, width), BF16),
        grid=(b, nq),
        in_specs=[
            pl.BlockSpec((QT, width), lambda bi, i: (bi * nq + i, 0)),
            pl.BlockSpec((QT, LANES), lambda bi, i: (bi * nq + i, 10)),
            _layer_spec(gq, l),
            pl.BlockSpec((1, nc, LANES), lambda bi, i: (bi, 0, 0)),
            pl.BlockSpec((1, LANES, nc), lambda bi, i: (bi, 0, 0)),
            pl.BlockSpec((ns, nc), lambda bi, i: (0, 0)),
            pl.BlockSpec((1, s, 2 * LANES), lambda bi, i: (bi, 0, 0)),
            pl.BlockSpec((1, s // SEL_CHUNK, NSA_GROUPS * VROWS, SEL_CHUNK), lambda bi, i: (bi, 0, 0, 0)),
            pl.BlockSpec((1, s, LANES), lambda bi, i: (bi, 0, 0)),
            pl.BlockSpec((1, s // WIN_CHUNK, NSA_GROUPS * VROWS, WIN_CHUNK), lambda bi, i: (bi, 0, 0, 0)),
        ],
        out_specs=pl.BlockSpec((QT, width), lambda bi, i: (bi * nq + i, 0)),
        scratch_shapes=[
            pltpu.VMEM((2 * LANES, nl), BF16),
            pltpu.VMEM((SEL_CHUNK, nl), BF16),
            pltpu.VMEM((SEL_CHUNK, nl), BF16),
            pltpu.VMEM((1, nl), F32),
            pltpu.VMEM((VROWS, nl), F32),
            pltpu.VMEM((HEAD_DIM, nl), F32),
            pltpu.VMEM((HEAD_DIM, nl), F32),
        ],
        compiler_params=_params(2),
        name="nsa_attn",
    )(znsa, znsa, gq, kc, vct, ovt, ksa, vst, kwn, vwt)


def _mix_kernel(x_ref, ab_ref, ac_ref, ax_ref, ach_ref, axh_ref, p_ref, ph_ref, sgu_ref, mg0_ref, mg1_ref, mg2_ref,
                mg3_ref, ob_ref, cw_ref, pw_ref, psc_ref, sng_ref, sw_ref, sb_ref, wbr_ref, wo_ref, o_ref, ext_ref,
                ext2_ref, *, tm, tiles_per_seq):
    ti = pl.program_id(0) % tiles_per_seq
    keep = jnp.where(ti == 0, 0.0, 1.0)

    ext_ref[0:HALO, :] = (ach_ref[...].astype(F32) * axh_ref[...].astype(F32)) * keep
    ext_ref[HALO:, :] = ac_ref[...].astype(F32) * ax_ref[...].astype(F32)
    e = ext_ref[...]
    cw = cw_ref[...]
    conv = (cw[0:1, :] * pltpu.roll(e, 2, 0) + cw[1:2, :] * pltpu.roll(e, 1, 0)) + cw[2:3, :] * e
    out_a = ab_ref[...].astype(F32) * conv[HALO:, :]

    p = p_ref[...].astype(F32)
    ext2_ref[0:HALO, :] = ph_ref[...].astype(F32) * keep
    ext2_ref[HALO:, :] = p
    e = ext2_ref[...]
    gw = BRANCH_WIDTH // len(POOL_WINDOWS)
    cnt = (ti * tm + 1 + lax.broadcasted_iota(jnp.int32, (tm, gw), 0)).astype(F32)
    groups = []
    for gi, w in enumerate(POOL_WINDOWS):
        acc = e[:, gi * gw:(gi + 1) * gw]
        span = 1
        while span < w:
            acc = acc + pltpu.roll(acc, span, 0)
            span *= 2
        groups.append(acc[HALO:, :] / jnp.minimum(cnt, float(w)))
    pooled = jnp.concatenate(groups, axis=1) - p
    out_c = _dot(pooled.astype(BF16), pw_ref[...]) * psc_ref[...]

    z = _erf_gelu(sgu_ref[...].astype(F32))
    u = z[:, :BRANCH_WIDTH]
    v = z[:, BRANCH_WIDTH:]
    v = (v * lax.rsqrt(jnp.mean(v * v, axis=-1, keepdims=True) + RMS_EPS)) * sng_ref[...]
    vb = v.astype(BF16)
    nchunk = tm // SGU_CHUNK
    gd = BRANCH_WIDTH // SGU_GROUPS
    tri = (lax.broadcasted_iota(jnp.int32, (SGU_CHUNK, SGU_CHUNK), 0)
           >= lax.broadcasted_iota(jnp.int32, (SGU_CHUNK, SGU_CHUNK), 1))
    mixed_g = []
    for g in range(SGU_GROUPS):
        wg = jnp.where(tri, sw_ref[g], 0.0).astype(BF16)
        rhs = jnp.concatenate(
            [vb[c * SGU_CHUNK:(c + 1) * SGU_CHUNK, g * gd:(g + 1) * gd] for c in range(nchunk)], axis=1)
        res = _dot(wg, rhs)
        mixed_g.append(jnp.concatenate([res[:, c * gd:(c + 1) * gd] for c in range(nchunk)], axis=0))
    bias = jnp.concatenate([sb_ref[...]] * nchunk, axis=0)
    out_d = u * (jnp.concatenate(mixed_g, axis=1) + bias)

    merged = jax.nn.sigmoid(mg0_ref[...].astype(F32)) * _dot(out_a.astype(BF16), wbr_ref[0])
    merged = merged + jax.nn.sigmoid(mg1_ref[...].astype(F32)) * _dot(ob_ref[...], wbr_ref[1])
    merged = merged + jax.nn.sigmoid(mg2_ref[...].astype(F32)) * _dot(out_c.astype(BF16), wbr_ref[2])
    merged = merged + jax.nn.sigmoid(mg3_ref[...].astype(F32)) * _dot(out_d.astype(BF16), wbr_ref[3])
    o_ref[...] = x_ref[...] + _dot(merged.astype(BF16), wo_ref[...])


def _mix_merge(x, za, out_b, cw, pw, psc, sng, sw, sb, wbr, wo, l, s, tm=512):
    t, d = x.shape
    bw = BRANCH_WIDTH
    tps = s // tm
    hb = tm // HALO
    tile = lambda c: pl.BlockSpec((tm, bw), lambda i, c=c: (i, c))
    halo = lambda c: pl.BlockSpec((HALO, bw), lambda i, c=c: (jnp.maximum(i * hb - 1, 0), c))
    wide = lambda c: pl.BlockSpec((tm, d), lambda i, c=c: (i, c))
    full = lambda a: _layer_spec(a, l)
    consts = (cw, pw, psc, sng, sw, sb, wbr, wo)
    return pl.pallas_call(
        functools.partial(_mix_kernel, tm=tm, tiles_per_seq=tps),
        out_shape=jax.ShapeDtypeStruct((t, d), F32),
        grid=(t // tm,),
        in_specs=[wide(0), tile(0), tile(1), tile(2), halo(1), halo(2), tile(3), halo(3), wide(2),
                  wide(3), wide(4), wide(5), wide(6), tile(0)] + [full(a) for a in consts],
        out_specs=wide(0),
        scratch_shapes=[pltpu.VMEM((tm + HALO, bw), F32), pltpu.VMEM((tm + HALO, bw), F32)],
        compiler_params=_params(1),
        name="mix_merge",
    )(x, za, za, za, za, za, za, za, za, za, za, za, za, out_b, *consts)


def _ffn_act_kernel(x_ref, xh_ref, g_ref, wg_ref, wu_ref, cw_ref, o_ref, h_ref, hh_ref, ext_ref, *,
                    tiles_per_seq, rb):
    @pl.when(pl.program_id(1) == 0)
    def _():
        h_ref[...] = _rms_bf16(x_ref[...], g_ref[...])
        hh_ref[...] = _rms_bf16(xh_ref[...], g_ref[...])

    keep = jnp.where(pl.program_id(0) % tiles_per_seq == 0, 0.0, 1.0)
    wg = wg_ref[...]
    wu = wu_ref[...]
    cw = cw_ref[...]
    ext_ref[0:HALO, :] = _dot(hh_ref[...], wg) * keep
    for r0 in range(0, h_ref.shape[0], rb):
        h = h_ref[r0:r0 + rb, :]
        ext_ref[HALO + r0:HALO + r0 + rb, :] = _dot(h, wg)
        e = ext_ref[r0:r0 + rb + HALO, :]
        conv = ((cw[0:1, :] * pltpu.roll(e, 2, 0) + cw[1:2, :] * pltpu.roll(e, 1, 0)) + cw[2:3, :] * e)[HALO:, :]
        o_ref[r0:r0 + rb, :] = ((conv * jax.nn.sigmoid(conv)) * _dot(h, wu)).astype(o_ref.dtype)


def _ffn_act(x, g, w_up, cw, l, s, tm):
    t, d = x.shape
    dff = cw.shape[2]
    hb = tm // HALO
    nj = dff // FF_CHUNK
    return pl.pallas_call(
        functools.partial(_ffn_act_kernel, tiles_per_seq=s // tm, rb=min(tm, 512)),
        out_shape=jax.ShapeDtypeStruct((t, dff), BF16),
        grid=(t // tm, dff // FF_CHUNK),
        in_specs=[
            pl.BlockSpec((tm, d), lambda i, j: (i, 0)),
            pl.BlockSpec((HALO, d), lambda i, j: (jnp.maximum(i * hb - 1, 0), 0)),
            _layer_spec(g, l),
            _layer_spec(w_up, l, (d, FF_CHUNK), lambda i, j: (0, j)),
            _layer_spec(w_up, l, (d, FF_CHUNK), lambda i, j: (0, nj + j)),
            _layer_spec(cw, l, (cw.shape[1], FF_CHUNK), lambda i, j: (0, j)),
        ],
        out_specs=pl.BlockSpec((tm, FF_CHUNK), lambda i, j: (i, j)),
        scratch_shapes=[pltpu.VMEM((tm, d), BF16), pltpu.VMEM((HALO, d), BF16), pltpu.VMEM((tm + HALO, FF_CHUNK), F32)],
        compiler_params=_params(2),
        name="ffn_act",
    )(x, x, g, w_up, w_up, cw)


def _ffn_down_kernel(x_ref, a_ref, wd_ref, o_ref):
    o_ref[...] = x_ref[...] + _dot(a_ref[...], wd_ref[...])


def _ffn_down(x, act, wd, l, tm):
    t, d = x.shape
    dff = wd.shape[1]
    return pl.pallas_call(
        _ffn_down_kernel,
        out_shape=jax.ShapeDtypeStruct((t, d), F32),
        grid=(t // tm,),
        in_specs=[
            pl.BlockSpec((tm, d), lambda i: (i, 0)),
            pl.BlockSpec((tm, dff), lambda i: (i, 0)),
            _layer_spec(wd, l),
        ],
        out_specs=pl.BlockSpec((tm, d), lambda i: (i, 0)),
        compiler_params=_params(1),
        name="ffn_down",
    )(x, act, wd)


def _overlap_t(nc, ns):
    n_cmp = nc - 1
    cs = np.arange(n_cmp) * CMP_STRIDE
    ce = cs + CMP_BLOCK
    ss = np.arange(ns) * SLC_BLOCK
    se = ss + SLC_BLOCK
    ov = np.clip(np.minimum(ce[:, None], se[None]) - np.maximum(cs[:, None], ss[None]), 0, None)
    out = np.zeros((ns, nc), np.float32)
    out[:, :n_cmp] = (ov.astype(np.float32) / CMP_BLOCK).T
    return out


def _compress_params(pe, w1, w2):
    depth = pe.shape[0]
    half = CMP_BLOCK // 2
    pe2 = jnp.tile(pe.reshape(depth, 2, 2, half, 1, HEAD_DIM), (1, 1, 1, 1, NSA_GROUPS, 1))
    pe2 = pe2.reshape(depth, 2, 2, half * LANES)
    w1r = w1.astype(BF16).reshape(depth, 2, 2, half, HEAD_DIM, HEAD_DIM)
    zero = jnp.zeros_like(w1r)
    w1bd = jnp.stack([jnp.concatenate([w1r, zero], axis=-1), jnp.concatenate([zero, w1r], axis=-1)], axis=4)
    w1bd = w1bd.reshape(depth, 2, 2, half * LANES, LANES)
    eye = jnp.eye(NSA_GROUPS, dtype=w2.dtype)
    w2bd = jnp.einsum("zjdo,gh->zjgdho", w2, eye).reshape(depth, 2, LANES, LANES)
    return pe2, w1bd, w2bd.astype(BF16)


def kernel(x, norm1_g, w_in, conv_a_w, qk_norm_g, cmp_pe, cmp_w1, cmp_w2, pool_w, pool_scale, sgu_norm_g, sgu_w,
           sgu_b, w_branch, w_o, norm2_g, w_up, conv_ff_w, w_down):
    b, s, d = x.shape
    depth = w_in.shape[0]
    t = b * s
    bw = BRANCH_WIDTH
    kvw = 6 * NSA_GROUPS * HEAD_DIM
    ngate = 3 * NSA_HEADS
    o_q = 3 * bw
    o_kv = o_q + bw
    o_gate = o_kv + kvw
    o_pool = o_gate + ngate
    dff = w_down.shape[1]
    ns = s // SLC_BLOCK
    nc = s // CMP_STRIDE
    ovt = jnp.asarray(_overlap_t(nc, ns), BF16)
    xf = x.reshape(t, d)
    tm = min(1024, t)
    tm_wide = min(2048, t)
    w_a = jnp.concatenate([w_in[:, :, :o_q], w_in[:, :, o_pool:]], axis=2).astype(BF16)
    w_n = jnp.concatenate([w_in[:, :, o_q:o_pool], jnp.zeros((depth, d, LANES - ngate), w_in.dtype)],
                          axis=2).astype(BF16)
    row = lambda v: v.reshape(depth, 1, v.shape[-1])
    g2 = lambda k: row(jnp.tile(qk_norm_g[:, k], (1, NSA_GROUPS)))
    g_cmp, g_sel, g_win = g2(1), g2(2), g2(3)
    gq = jnp.broadcast_to(qk_norm_g[:, 0][:, :, None], (depth, HEAD_DIM, QT))
    pe2, w1bd, w2bd = _compress_params(cmp_pe, cmp_w1, cmp_w2)
    n_pool = len(POOL_WINDOWS)
    pw = jnp.einsum("zgio,gh->zgiho", pool_w, jnp.eye(n_pool, dtype=pool_w.dtype)).reshape(depth, bw, bw).astype(BF16)
    sb = jnp.repeat(jnp.swapaxes(sgu_b, 1, 2), bw // SGU_GROUPS, axis=2)
    w_br, w_out, w_dn, w_gu = w_branch.astype(BF16), w_o.astype(BF16), w_down.astype(BF16), w_up.astype(BF16)
    n1, n2, psc, sng = row(norm1_g), row(norm2_g), row(pool_scale), row(sgu_norm_g)

    for l in range(depth):
        za = _norm_matmul(xf, n1, w_a, l, BF16, tm_wide, w_a.shape[2] // IN_PROJ_COL_TILES, "in_proj_a")
        znsa = _norm_matmul(xf, n1, w_n, l, F32, tm_wide, w_n.shape[2], "in_proj_nsa")
        ksa, vst, kwn, vwt = _kv_prep(znsa, g_sel, g_win, l, b, s)
        kc, vct = _compress(znsa, pe2, w1bd, w2bd, g_cmp, l, b, s)
        out_b = _nsa_attn(znsa, gq, l, kc, vct, ovt, ksa, vst, kwn, vwt, b, s)
        xf = _mix_merge(xf, za, out_b, conv_a_w, pw, psc, sng, sgu_w, sb, w_br, w_out, l, s)
        act = _ffn_act(xf, n2, w_gu, conv_ff_w, l, s, tm)
        xf = _ffn_down(xf, act, w_dn, l, tm)
    return xf.reshape(b, s, d)
```

```python
import functools

import jax
import jax.numpy as jnp
import numpy as np
from jax import lax
from jax.experimental import pallas as pl
from jax.experimental.pallas import tpu as pltpu

F32 = jnp.float32
BF16 = jnp.bfloat16

HEAD_DIM = 64
NSA_HEADS = 8
NSA_GROUPS = 2
NSA_REP = NSA_HEADS // NSA_GROUPS
CMP_BLOCK = 32
CMP_STRIDE = 16
SLC_BLOCK = 64
SLC_TOPN = 16
WIN = 512
NEG = -1e30
FORCE = 1e6
RMS_EPS = 1e-6
LOG2E = 1.4426950408889634
POOL_WINDOWS = (2, 4, 8, 16)
SGU_CHUNK = 128
SGU_GROUPS = 4
BRANCH_WIDTH = 512
N_BRANCH = 4

LANES = 128
HALO = 16
QT = 512
SEL_CHUNK = 256
WIN_CHUNK = 128
IN_PROJ_COL_TILES = 4
FF_CHUNK = 1408
RANK_STEP = 16
VROWS = HEAD_DIM + 16
BF16_ROWS = 16
VMEM_LIMIT = 56 * 1024 * 1024


def _params(n_axes):
    return pltpu.CompilerParams(dimension_semantics=("arbitrary",) * n_axes, vmem_limit_bytes=VMEM_LIMIT)


def _dot(a, b):
    return jnp.dot(a, b, preferred_element_type=F32)


def _layer_spec(arr, l, block=None, index=None):
    block = tuple(arr.shape[1:]) if block is None else tuple(block)
    index = index or (lambda *_: (0,) * len(block))
    return pl.BlockSpec((None,) + block, lambda *grid: (l,) + tuple(index(*grid)))


def _rms_bf16(x, g):
    ms = jnp.mean(x * x, axis=-1, keepdims=True)
    return ((x * lax.rsqrt(ms + RMS_EPS)) * g).astype(BF16)


def _erf_gelu(x):
    return 0.5 * x * (1.0 + lax.erf(x * (2.0 ** -0.5)))


def _norm_matmul_kernel(x_ref, g_ref, w_ref, o_ref, h_ref):
    @pl.when(pl.program_id(1) == 0)
    def _():
        h_ref[...] = _rms_bf16(x_ref[...], g_ref[...])

    o_ref[...] = _dot(h_ref[...], w_ref[...]).astype(o_ref.dtype)


def _norm_matmul(x, g, w, l, out_dtype, tm, tn, name):
    t, d = x.shape
    n = w.shape[2]
    return pl.pallas_call(
        _norm_matmul_kernel,
        out_shape=jax.ShapeDtypeStruct((t, n), out_dtype),
        grid=(t // tm, n // tn),
        in_specs=[
            pl.BlockSpec((tm, d), lambda i, j: (i, 0)),
            _layer_spec(g, l),
            _layer_spec(w, l, (d, tn), lambda i, j: (0, j)),
        ],
        out_specs=pl.BlockSpec((tm, tn), lambda i, j: (i, j)),
        scratch_shapes=[pltpu.VMEM((tm, d), BF16)],
        compiler_params=_params(2),
        name=name,
    )(x, g, w)


def _group_rms(x, g):
    lane = lax.broadcasted_iota(jnp.int32, x.shape, 1)
    lo = lane < HEAD_DIM
    x2 = x * x
    s_lo = jnp.sum(jnp.where(lo, x2, 0.0), axis=-1, keepdims=True)
    s_hi = jnp.sum(jnp.where(lo, 0.0, x2), axis=-1, keepdims=True)
    ms = jnp.where(lo, s_lo, s_hi) * (1.0 / HEAD_DIM)
    return (x * lax.rsqrt(ms + RMS_EPS)) * g


def _kv_prep_kernel(ks_ref, vs_ref, kw_ref, vw_ref, gs_ref, gw_ref, ksa_ref, vst_ref, kwn_ref, vwt_ref, *, ts):
    j = pl.program_id(1)
    ksn = _group_rms(ks_ref[...], gs_ref[...]).astype(BF16)
    row = lax.broadcasted_iota(jnp.int32, (ts, LANES), 0) + j * ts
    col = lax.broadcasted_iota(jnp.int32, (ts, LANES), 1)
    onehot = jnp.where(row // SLC_BLOCK == col, 1.0, 0.0).astype(BF16)
    ksa_ref[0, :, 0:LANES] = ksn
    ksa_ref[0, :, LANES:2 * LANES] = onehot
    kwn_ref[0] = _group_rms(kw_ref[...], gw_ref[...]).astype(BF16)
    vs = vs_ref[...]
    for c in range(ts // SEL_CHUNK):
        vst_ref[0, c] = _values_t(vs[c * SEL_CHUNK:(c + 1) * SEL_CHUNK, :])
    vw = vw_ref[...]
    for c in range(ts // WIN_CHUNK):
        vwt_ref[0, c] = _values_t(vw[c * WIN_CHUNK:(c + 1) * WIN_CHUNK, :])


def _values_t(v):
    vt = v.T
    keys = vt.shape[1]
    ones_row = jnp.where(lax.broadcasted_iota(jnp.int32, (VROWS - HEAD_DIM, keys), 0) == 0, 1.0, 0.0)
    parts = []
    for g in range(NSA_GROUPS):
        parts += [vt[g * HEAD_DIM:(g + 1) * HEAD_DIM, :], ones_row]
    return jnp.concatenate(parts, axis=0).astype(BF16)


def _kv_prep(znsa, gs, gw, l, b, s, ts=2048):
    ts = min(ts, s)
    nt = s // ts
    col = lambda c: pl.BlockSpec((ts, LANES), lambda bi, j, c=c: (bi * nt + j, c))
    vec = _layer_spec(gs, l)
    return pl.pallas_call(
        functools.partial(_kv_prep_kernel, ts=ts),
        out_shape=(
            jax.ShapeDtypeStruct((b, s, 2 * LANES), BF16),
            jax.ShapeDtypeStruct((b, s // SEL_CHUNK, NSA_GROUPS * VROWS, SEL_CHUNK), BF16),
            jax.ShapeDtypeStruct((b, s, LANES), BF16),
            jax.ShapeDtypeStruct((b, s // WIN_CHUNK, NSA_GROUPS * VROWS, WIN_CHUNK), BF16),
        ),
        grid=(b, nt),
        in_specs=[col(6), col(7), col(8), col(9), vec, vec],
        out_specs=(
            pl.BlockSpec((1, ts, 2 * LANES), lambda bi, j: (bi, j, 0)),
            pl.BlockSpec((1, ts // SEL_CHUNK, NSA_GROUPS * VROWS, SEL_CHUNK), lambda bi, j: (bi, j, 0, 0)),
            pl.BlockSpec((1, ts, LANES), lambda bi, j: (bi, j, 0)),
            pl.BlockSpec((1, ts // WIN_CHUNK, NSA_GROUPS * VROWS, WIN_CHUNK), lambda bi, j: (bi, j, 0, 0)),
        ),
        compiler_params=_params(2),
        name="kv_prep",
    )(znsa, znsa, znsa, znsa, gs, gw)


def _compress_kernel(kc_ref, vc_ref, pe_ref, w1_ref, w2_ref, g_ref, kco_ref, vct_ref, u_ref, *, nc):
    half = CMP_BLOCK // 2
    for j, src in enumerate((kc_ref, vc_ref)):
        for l in range(half):
            u_ref[:, l * LANES:(l + 1) * LANES] = src[pl.ds(l, nc, stride=CMP_STRIDE), :]
        u = u_ref[...]
        top = _dot((u + pe_ref[j, 0:1, :]).astype(BF16), w1_ref[j, 0])
        bot = _dot((u + pe_ref[j, 1:2, :]).astype(BF16), w1_ref[j, 1])
        pre = top + pltpu.roll(bot, nc - 1, 0)
        hid = pre * jax.nn.sigmoid(pre)
        out = _dot(hid.astype(BF16), w2_ref[j])
        if j == 0:
            kco_ref[0] = _group_rms(out, g_ref[...]).astype(BF16)
        else:
            vct_ref[0] = out.T.astype(BF16)


def _compress(znsa, pe2, w1bd, w2bd, g1, l, b, s):
    nc = s // CMP_STRIDE
    kw = (CMP_BLOCK // 2) * LANES
    return pl.pallas_call(
        functools.partial(_compress_kernel, nc=nc),
        out_shape=(
            jax.ShapeDtypeStruct((b, nc, LANES), BF16),
            jax.ShapeDtypeStruct((b, LANES, nc), BF16),
        ),
        grid=(b,),
        in_specs=[
            pl.BlockSpec((s, LANES), lambda bi: (bi, 4)),
            pl.BlockSpec((s, LANES), lambda bi: (bi, 5)),
            _layer_spec(pe2, l),
            _layer_spec(w1bd, l),
            _layer_spec(w2bd, l),
            _layer_spec(g1, l),
        ],
        out_specs=(
            pl.BlockSpec((1, nc, LANES), lambda bi: (bi, 0, 0)),
            pl.BlockSpec((1, LANES, nc), lambda bi: (bi, 0, 0)),
        ),
        scratch_shapes=[pltpu.VMEM((nc, kw), F32)],
        compiler_params=_params(1),
        name="compress",
    )(znsa, znsa, pe2, w1bd, w2bd, g1)


def _rank_desc(imp, ns):
    nq = imp.shape[1]
    sub = lax.broadcasted_iota(jnp.int32, (8, nq), 0)
    blocks = [imp[8 * a:8 * a + 8, :] for a in range(ns // 8)]
    rank = [jnp.zeros((8, nq), F32) for _ in blocks]
    for k in range(ns):
        ka, kr = divmod(k, 8)
        rk = jnp.broadcast_to(imp[k:k + 1, :], (8, nq))
        for a, blk in enumerate(blocks):
            if a > ka:
                ahead = rk >= blk
            elif a < ka:
                ahead = rk > blk
            else:
                rank[a] = rank[a] + jnp.where(sub > kr, jnp.where(rk >= blk, 1.0, 0.0), jnp.where(rk > blk, 1.0, 0.0))
                continue
            rank[a] = rank[a] + jnp.where(ahead, 1.0, 0.0)
    return jnp.concatenate(rank, axis=0)


def _nsa_attn_kernel(q_ref, gate_ref, gq_ref, kc_ref, vct_ref, ovt_ref, ksa_ref, vst_ref, kw_ref, vwt_ref,
                     o_ref, qa_ref, sa_ref, sb_ref, m_ref, acc_ref, ocmp_ref, owin_ref, *, ns, nc, topn):
    i = pl.program_id(1)
    start = i * QT
    nl = NSA_HEADS * QT
    gl = NSA_REP * QT

    qt = q_ref[...].T.reshape(NSA_HEADS, HEAD_DIM, QT)
    ms = jnp.mean(qt * qt, axis=1, keepdims=True)
    qn = ((qt * lax.rsqrt(ms + RMS_EPS)) * gq_ref[...][None]) * (HEAD_DIM ** -0.5 * LOG2E)
    qa_ref[...] = jnp.zeros(qa_ref.shape, BF16)
    for h in range(NSA_HEADS):
        g = h // NSA_REP
        qa_ref[g * HEAD_DIM:(g + 1) * HEAD_DIM, h * QT:(h + 1) * QT] = qn[h].astype(BF16)

    def tq(shape):
        return start + (lax.broadcasted_iota(jnp.int32, shape, 1) & (QT - 1))

    def krow(shape, base):
        return base + lax.broadcasted_iota(jnp.int32, shape, 0)

    def group_pv(vt, pb):
        rows = vt.shape[0] // NSA_GROUPS
        return jnp.concatenate(
            [_dot(vt[g * rows:(g + 1) * rows, :], pb[:, g * gl:(g + 1) * gl]) for g in range(NSA_GROUPS)],
            axis=1)

    def compressed_and_select():
        qk = qa_ref[0:LANES, :]
        sc = _dot(kc_ref[0], qk)
        cval = krow((nc, nl), 0) * CMP_STRIDE + (CMP_BLOCK - 1) <= tq((nc, nl))
        sc = jnp.where(cval, sc, NEG)
        e = jnp.exp2(sc - jnp.max(sc, axis=0, keepdims=True))
        pc = e / jnp.sum(e, axis=0, keepdims=True)
        pc = jnp.where(tq((1, nl)) >= CMP_BLOCK - 1, pc, 0.0)
        ocmp_ref[...] = group_pv(vct_ref[0], pc.astype(BF16))

        jb = krow((ns, QT), 0)
        jt = (start + lax.broadcasted_iota(jnp.int32, (ns, QT), 1)) // SLC_BLOCK
        forced = (jb == 0) | (jb == jt) | (jb == jt - 1)
        future = jb > jt
        last_block = (start + QT - 1) // SLC_BLOCK
        n_var = max(ns // RANK_STEP, 1)
        ovt = ovt_ref[...]
        for g in range(NSA_GROUPS):
            psum = pc[:, g * gl:g * gl + QT]
            for r in range(1, NSA_REP):
                psum = psum + pc[:, g * gl + r * QT:g * gl + (r + 1) * QT]
            p_hi = psum.astype(BF16)
            rem = psum - p_hi.astype(F32)
            p_mid = rem.astype(BF16)
            p_lo = (rem - p_mid.astype(F32)).astype(BF16)
            imp = (_dot(ovt, p_hi) + _dot(ovt, p_mid)) + _dot(ovt, p_lo)
            imp = jnp.where(forced, FORCE, jnp.where(future, NEG, imp))
            for v in range(n_var):
                rows = RANK_STEP * (v + 1)

                @pl.when(jnp.minimum(last_block // RANK_STEP, n_var - 1) == v)
                def _(g=g, imp=imp, rows=rows):
                    sel = (_rank_desc(imp[0:rows, :], rows) < topn) & jnp.logical_not(future[0:rows, :])
                    bias = jnp.where(sel, 0.0, NEG)
                    if rows < ns:
                        bias = jnp.concatenate([bias, jnp.full((ns - rows, QT), NEG, F32)], axis=0)
                    for r in range(NSA_REP):
                        h = g * NSA_REP + r
                        qa_ref[LANES:LANES + ns, h * QT:(h + 1) * QT] = bias.astype(BF16)

    def col_max(s):
        part = jnp.max(s.reshape(s.shape[0] // BF16_ROWS, BF16_ROWS, nl), axis=0)
        return jnp.max(part.astype(F32), axis=0, keepdims=True)

    def probs(s, m):
        return jnp.exp2(s - m.astype(BF16))

    def normalised(acc):
        return acc[0:HEAD_DIM, :] / acc[HEAD_DIM:HEAD_DIM + 1, :]

    def update(s, vt):
        m_old = m_ref[...]
        m_new = jnp.maximum(m_old, col_max(s))
        pv = group_pv(vt, probs(s, m_new))
        acc_ref[...] = jnp.exp2(m_old - m_new) * acc_ref[...] + pv
        m_ref[...] = m_new

    n_back = WIN // WIN_CHUNK
    n_own = QT // WIN_CHUNK
    wshape = (WIN_CHUNK, nl)
    local_q = lax.broadcasted_iota(jnp.int32, wshape, 1) & (QT - 1)
    local_k = lax.broadcasted_iota(jnp.int32, wshape, 0)

    def window():
        ss, vts = [], []
        for c in range(n_back + n_own):
            off = c * WIN_CHUNK
            ci = i * n_own - n_back + c
            cl = jnp.maximum(ci, 0)
            k = kw_ref[0, pl.ds(pl.multiple_of(cl * WIN_CHUNK, WIN_CHUNK), WIN_CHUNK), :]
            s = _dot(k, qa_ref[0:LANES, :])
            if off < QT:
                s = jnp.where(local_k + off > local_q + jnp.where(ci >= 0, 0, QT + WIN_CHUNK), s, NEG)
            elif off + WIN_CHUNK > WIN:
                s = jnp.where(local_k + (off - WIN) <= local_q, s, NEG)
            elif c < n_back:
                s = s + jnp.where(ci >= 0, 0.0, NEG)
            ss.append(s.astype(BF16))
            vts.append(vwt_ref[0, cl])
        m = col_max(ss[0])
        for s in ss[1:]:
            m = jnp.maximum(m, col_max(s))
        pb = jnp.concatenate([probs(s, m) for s in ss], axis=0)
        owin_ref[...] = normalised(group_pv(jnp.concatenate(vts, axis=1), pb))

    compressed_and_select()
    window()

    def sel_scores(buf, c, own=False):
        k = ksa_ref[0, pl.ds(pl.multiple_of(c * SEL_CHUNK, SEL_CHUNK), SEL_CHUNK), :]
        s = _dot(k, qa_ref[...])
        if own:
            shape = (SEL_CHUNK, nl)
            s = jnp.where(krow(shape, c * SEL_CHUNK) <= tq(shape), s, NEG)
        buf[...] = s.astype(BF16)

    def sel_update(buf, c):
        update(buf[...], vst_ref[0, c])

    m_ref[...] = jnp.full(m_ref.shape, NEG, F32)
    acc_ref[...] = jnp.zeros(acc_ref.shape, F32)
    cd = start // SEL_CHUNK
    sel_scores(sa_ref, 0)

    def sel_pair(p, carry):
        c0 = 2 * p
        sel_scores(sb_ref, c0 + 1)
        sel_update(sa_ref, c0)
        sel_scores(sa_ref, jnp.minimum(c0 + 2, cd - 1))
        sel_update(sb_ref, c0 + 1)
        return carry

    lax.fori_loop(0, cd // 2, sel_pair, 0)

    @pl.when(cd % 2 == 1)
    def _():
        sel_update(sa_ref, cd - 1)

    for o in range(QT // SEL_CHUNK):
        sel_scores(sb_ref, cd + o, own=True)
        sel_update(sb_ref, cd + o)

    o_sel = normalised(acc_ref[...])
    o_cmp = ocmp_ref[...]
    o_win = owin_ref[...]

    sg = jax.nn.sigmoid(gate_ref[...].T[0:3 * NSA_HEADS, :])
    outs = []
    for h in range(NSA_HEADS):
        lanes = slice(h * QT, (h + 1) * QT)
        outs.append(sg[3 * h:3 * h + 1, :] * o_cmp[:, lanes] + sg[3 * h + 1:3 * h + 2, :] * o_sel[:, lanes]
                    + sg[3 * h + 2:3 * h + 3, :] * o_win[:, lanes])
    o_ref[...] = jnp.concatenate(outs, axis=0).T.astype(o_ref.dtype)


def _nsa_attn(znsa, gq, l, kc, vct, ovt, ksa, vst, kwn, vwt, b, s):
    ns = s // SLC_BLOCK
    nc = s // CMP_STRIDE
    nq = s // QT
    nl = NSA_HEADS * QT
    width = NSA_HEADS * HEAD_DIM
    kern = functools.partial(_nsa_attn_kernel, ns=ns, nc=nc, topn=min(SLC_TOPN, ns))
    return pl.pallas_call(
        kern,
        out_shape=jax.ShapeDtypeStruct((b * s, width), BF16),
        grid=(b, nq),
        in_specs=[
            pl.BlockSpec((QT, width), lambda bi, i: (bi * nq + i, 0)),
            pl.BlockSpec((QT, LANES), lambda bi, i: (bi * nq + i, 10)),
            _layer_spec(gq, l),
            pl.BlockSpec((1, nc, LANES), lambda bi, i: (bi, 0, 0)),
            pl.BlockSpec((1, LANES, nc), lambda bi, i: (bi, 0, 0)),
            pl.BlockSpec((ns, nc), lambda bi, i: (0, 0)),
            pl.BlockSpec((1, s, 2 * LANES), lambda bi, i: (bi, 0, 0)),
            pl.BlockSpec((1, s // SEL_CHUNK, NSA_GROUPS * VROWS, SEL_CHUNK), lambda bi, i: (bi, 0, 0, 0)),
            pl.BlockSpec((1, s, LANES), lambda bi, i: (bi, 0, 0)),
            pl.BlockSpec((1, s // WIN_CHUNK, NSA_GROUPS * VROWS, WIN_CHUNK), lambda bi, i: (bi, 0, 0, 0)),
        ],
        out_specs=pl.BlockSpec((QT, width), lambda bi, i: (bi * nq + i, 0)),
        scratch_shapes=[
            pltpu.VMEM((2 * LANES, nl), BF16),
            pltpu.VMEM((SEL_CHUNK, nl), BF16),
            pltpu.VMEM((SEL_CHUNK, nl), BF16),
            pltpu.VMEM((1, nl), F32),
            pltpu.VMEM((VROWS, nl), F32),
            pltpu.VMEM((HEAD_DIM, nl), F32),
            pltpu.VMEM((HEAD_DIM, nl), F32),
        ],
        compiler_params=_params(2),
        name="nsa_attn",
    )(znsa, znsa, gq, kc, vct, ovt, ksa, vst, kwn, vwt)


def _mix_kernel(x_ref, ab_ref, ac_ref, ax_ref, ach_ref, axh_ref, p_ref, ph_ref, sgu_ref, mg0_ref, mg1_ref, mg2_ref,
                mg3_ref, ob_ref, cw_ref, pw_ref, psc_ref, sng_ref, sw_ref, sb_ref, wbr_ref, wo_ref, o_ref, ext_ref,
                ext2_ref, *, tm, tiles_per_seq):
    ti = pl.program_id(0) % tiles_per_seq
    keep = jnp.where(ti == 0, 0.0, 1.0)

    ext_ref[0:HALO, :] = (ach_ref[...].astype(F32) * axh_ref[...].astype(F32)) * keep
    ext_ref[HALO:, :] = ac_ref[...].astype(F32) * ax_ref[...].astype(F32)
    e = ext_ref[...]
    cw = cw_ref[...]
    conv = (cw[0:1, :] * pltpu.roll(e, 2, 0) + cw[1:2, :] * pltpu.roll(e, 1, 0)) + cw[2:3, :] * e
    out_a = ab_ref[...].astype(F32) * conv[HALO:, :]

    p = p_ref[...].astype(F32)
    ext2_ref[0:HALO, :] = ph_ref[...].astype(F32) * keep
    ext2_ref[HALO:, :] = p
    e = ext2_ref[...]
    gw = BRANCH_WIDTH // len(POOL_WINDOWS)
    cnt = (ti * tm + 1 + lax.broadcasted_iota(jnp.int32, (tm, gw), 0)).astype(F32)
    groups = []
    for gi, w in enumerate(POOL_WINDOWS):
        acc = e[:, gi * gw:(gi + 1) * gw]
        span = 1
        while span < w:
            acc = acc + pltpu.roll(acc, span, 0)
            span *= 2
        groups.append(acc[HALO:, :] / jnp.minimum(cnt, float(w)))
    pooled = jnp.concatenate(groups, axis=1) - p
    out_c = _dot(pooled.astype(BF16), pw_ref[...]) * psc_ref[...]

    z = _erf_gelu(sgu_ref[...].astype(F32))
    u = z[:, :BRANCH_WIDTH]
    v = z[:, BRANCH_WIDTH:]
    v = (v * lax.rsqrt(jnp.mean(v * v, axis=-1, keepdims=True) + RMS_EPS)) * sng_ref[...]
    vb = v.astype(BF16)
    nchunk = tm // SGU_CHUNK
    gd = BRANCH_WIDTH // SGU_GROUPS
    tri = (lax.broadcasted_iota(jnp.int32, (SGU_CHUNK, SGU_CHUNK), 0)
           >= lax.broadcasted_iota(jnp.int32, (SGU_CHUNK, SGU_CHUNK), 1))
    mixed_g = []
    for g in range(SGU_GROUPS):
        wg = jnp.where(tri, sw_ref[g], 0.0).astype(BF16)
        rhs = jnp.concatenate(
            [vb[c * SGU_CHUNK:(c + 1) * SGU_CHUNK, g * gd:(g + 1) * gd] for c in range(nchunk)], axis=1)
        res = _dot(wg, rhs)
        mixed_g.append(jnp.concatenate([res[:, c * gd:(c + 1) * gd] for c in range(nchunk)], axis=0))
    bias = jnp.concatenate([sb_ref[...]] * nchunk, axis=0)
    out_d = u * (jnp.concatenate(mixed_g, axis=1) + bias)

    merged = jax.nn.sigmoid(mg0_ref[...].astype(F32)) * _dot(out_a.astype(BF16), wbr_ref[0])
    merged = merged + jax.nn.sigmoid(mg1_ref[...].astype(F32)) * _dot(ob_ref[...], wbr_ref[1])
    merged = merged + jax.nn.sigmoid(mg2_ref[...].astype(F32)) * _dot(out_c.astype(BF16), wbr_ref[2])
    merged = merged + jax.nn.sigmoid(mg3_ref[...].astype(F32)) * _dot(out_d.astype(BF16), wbr_ref[3])
    o_ref[...] = x_ref[...] + _dot(merged.astype(BF16), wo_ref[...])


def _mix_merge(x, za, out_b, cw, pw, psc, sng, sw, sb, wbr, wo, l, s, tm=512):
    t, d = x.shape
    bw = BRANCH_WIDTH
    tps = s // tm
    hb = tm // HALO
    tile = lambda c: pl.BlockSpec((tm, bw), lambda i, c=c: (i, c))
    halo = lambda c: pl.BlockSpec((HALO, bw), lambda i, c=c: (jnp.maximum(i * hb - 1, 0), c))
    wide = lambda c: pl.BlockSpec((tm, d), lambda i, c=c: (i, c))
    full = lambda a: _layer_spec(a, l)
    consts = (cw, pw, psc, sng, sw, sb, wbr, wo)
    return pl.pallas_call(
        functools.partial(_mix_kernel, tm=tm, tiles_per_seq=tps),
        out_shape=jax.ShapeDtypeStruct((t, d), F32),
        grid=(t // tm,),
        in_specs=[wide(0), tile(0), tile(1), tile(2), halo(1), halo(2), tile(3), halo(3), wide(2),
                  wide(3), wide(4), wide(5), wide(6), tile(0)] + [full(a) for a in consts],
        out_specs=wide(0),
        scratch_shapes=[pltpu.VMEM((tm + HALO, bw), F32), pltpu.VMEM((tm + HALO, bw), F32)],
        compiler_params=_params(1),
        name="mix_merge",
    )(x, za, za, za, za, za, za, za, za, za, za, za, za, out_b, *consts)


def _ffn_act_kernel(x_ref, xh_ref, g_ref, wg_ref, wu_ref, cw_ref, o_ref, h_ref, hh_ref, ext_ref, *,
                    tiles_per_seq, rb):
    @pl.when(pl.program_id(1) == 0)
    def _():
        h_ref[...] = _rms_bf16(x_ref[...], g_ref[...])
        hh_ref[...] = _rms_bf16(xh_ref[...], g_ref[...])

    keep = jnp.where(pl.program_id(0) % tiles_per_seq == 0, 0.0, 1.0)
    wg = wg_ref[...]
    wu = wu_ref[...]
    cw = cw_ref[...]
    ext_ref[0:HALO, :] = _dot(hh_ref[...], wg) * keep
    for r0 in range(0, h_ref.shape[0], rb):
        h = h_ref[r0:r0 + rb, :]
        ext_ref[HALO + r0:HALO + r0 + rb, :] = _dot(h, wg)
        e = ext_ref[r0:r0 + rb + HALO, :]
        conv = ((cw[0:1, :] * pltpu.roll(e, 2, 0) + cw[1:2, :] * pltpu.roll(e, 1, 0)) + cw[2:3, :] * e)[HALO:, :]
        o_ref[r0:r0 + rb, :] = ((conv * jax.nn.sigmoid(conv)) * _dot(h, wu)).astype(o_ref.dtype)


def _ffn_act(x, g, w_up, cw, l, s, tm):
    t, d = x.shape
    dff = cw.shape[2]
    hb = tm // HALO
    nj = dff // FF_CHUNK
    return pl.pallas_call(
        functools.partial(_ffn_act_kernel, tiles_per_seq=s // tm, rb=min(tm, 512)),
        out_shape=jax.ShapeDtypeStruct((t, dff), BF16),
        grid=(t // tm, dff // FF_CHUNK),
        in_specs=[
            pl.BlockSpec((tm, d), lambda i, j: (i, 0)),
            pl.BlockSpec((HALO, d), lambda i, j: (jnp.maximum(i * hb - 1, 0), 0)),
            _layer_spec(g, l),
            _layer_spec(w_up, l, (d, FF_CHUNK), lambda i, j: (0, j)),
            _layer_spec(w_up, l, (d, FF_CHUNK), lambda i, j: (0, nj + j)),
            _layer_spec(cw, l, (cw.shape[1], FF_CHUNK), lambda i, j: (0, j)),
        ],
        out_specs=pl.BlockSpec((tm, FF_CHUNK), lambda i, j: (i, j)),
        scratch_shapes=[pltpu.VMEM((tm, d), BF16), pltpu.VMEM((HALO, d), BF16), pltpu.VMEM((tm + HALO, FF_CHUNK), F32)],
        compiler_params=_params(2),
        name="ffn_act",
    )(x, x, g, w_up, w_up, cw)


def _ffn_down_kernel(x_ref, a_ref, wd_ref, o_ref):
    o_ref[...] = x_ref[...] + _dot(a_ref[...], wd_ref[...])


def _ffn_down(x, act, wd, l, tm):
    t, d = x.shape
    dff = wd.shape[1]
    return pl.pallas_call(
        _ffn_down_kernel,
        out_shape=jax.ShapeDtypeStruct((t, d), F32),
        grid=(t // tm,),
        in_specs=[
            pl.BlockSpec((tm, d), lambda i: (i, 0)),
            pl.BlockSpec((tm, dff), lambda i: (i, 0)),
            _layer_spec(wd, l),
        ],
        out_specs=pl.BlockSpec((tm, d), lambda i: (i, 0)),
        compiler_params=_params(1),
        name="ffn_down",
    )(x, act, wd)


def _overlap_t(nc, ns):
    n_cmp = nc - 1
    cs = np.arange(n_cmp) * CMP_STRIDE
    ce = cs + CMP_BLOCK
    ss = np.arange(ns) * SLC_BLOCK
    se = ss + SLC_BLOCK
    ov = np.clip(np.minimum(ce[:, None], se[None]) - np.maximum(cs[:, None], ss[None]), 0, None)
    out = np.zeros((ns, nc), np.float32)
    out[:, :n_cmp] = (ov.astype(np.float32) / CMP_BLOCK).T
    return out


def _compress_params(pe, w1, w2):
    depth = pe.shape[0]
    half = CMP_BLOCK // 2
    pe2 = jnp.tile(pe.reshape(depth, 2, 2, half, 1, HEAD_DIM), (1, 1, 1, 1, NSA_GROUPS, 1))
    pe2 = pe2.reshape(depth, 2, 2, half * LANES)
    w1r = w1.astype(BF16).reshape(depth, 2, 2, half, HEAD_DIM, HEAD_DIM)
    zero = jnp.zeros_like(w1r)
    w1bd = jnp.stack([jnp.concatenate([w1r, zero], axis=-1), jnp.concatenate([zero, w1r], axis=-1)], axis=4)
    w1bd = w1bd.reshape(depth, 2, 2, half * LANES, LANES)
    eye = jnp.eye(NSA_GROUPS, dtype=w2.dtype)
    w2bd = jnp.einsum("zjdo,gh->zjgdho", w2, eye).reshape(depth, 2, LANES, LANES)
    return pe2, w1bd, w2bd.astype(BF16)


def kernel(x, norm1_g, w_in, conv_a_w, qk_norm_g, cmp_pe, cmp_w1, cmp_w2, pool_w, pool_scale, sgu_norm_g, sgu_w,
           sgu_b, w_branch, w_o, norm2_g, w_up, conv_ff_w, w_down):
    b, s, d = x.shape
    depth = w_in.shape[0]
    t = b * s
    bw = BRANCH_WIDTH
    kvw = 6 * NSA_GROUPS * HEAD_DIM
    ngate = 3 * NSA_HEADS
    o_q = 3 * bw
    o_kv = o_q + bw
    o_gate = o_kv + kvw
    o_pool = o_gate + ngate
    dff = w_down.shape[1]
    ns = s // SLC_BLOCK
    nc = s // CMP_STRIDE
    ovt = jnp.asarray(_overlap_t(nc, ns), BF16)
    xf = x.reshape(t, d)
    tm = min(1024, t)
    tm_wide = min(2048, t)
    w_a = jnp.concatenate([w_in[:, :, :o_q], w_in[:, :, o_pool:]], axis=2).astype(BF16)
    w_n = jnp.concatenate([w_in[:, :, o_q:o_pool], jnp.zeros((depth, d, LANES - ngate), w_in.dtype)],
                          axis=2).astype(BF16)
    row = lambda v: v.reshape(depth, 1, v.shape[-1])
    g2 = lambda k: row(jnp.tile(qk_norm_g[:, k], (1, NSA_GROUPS)))
    g_cmp, g_sel, g_win = g2(1), g2(2), g2(3)
    gq = jnp.broadcast_to(qk_norm_g[:, 0][:, :, None], (depth, HEAD_DIM, QT))
    pe2, w1bd, w2bd = _compress_params(cmp_pe, cmp_w1, cmp_w2)
    n_pool = len(POOL_WINDOWS)
    pw = jnp.einsum("zgio,gh->zgiho", pool_w, jnp.eye(n_pool, dtype=pool_w.dtype)).reshape(depth, bw, bw).astype(BF16)
    sb = jnp.repeat(jnp.swapaxes(sgu_b, 1, 2), bw // SGU_GROUPS, axis=2)
    w_br, w_out, w_dn, w_gu = w_branch.astype(BF16), w_o.astype(BF16), w_down.astype(BF16), w_up.astype(BF16)
    n1, n2, psc, sng = row(norm1_g), row(norm2_g), row(pool_scale), row(sgu_norm_g)

    for l in range(depth):
        za = _norm_matmul(xf, n1, w_a, l, BF16, tm_wide, w_a.shape[2] // IN_PROJ_COL_TILES, "in_proj_a")
        znsa = _norm_matmul(xf, n1, w_n, l, F32, tm_wide, w_n.shape[2], "in_proj_nsa")
        ksa, vst, kwn, vwt = _kv_prep(znsa, g_sel, g_win, l, b, s)
        kc, vct = _compress(znsa, pe2, w1bd, w2bd, g_cmp, l, b, s)
        out_b = _nsa_attn(znsa, gq, l, kc, vct, ovt, ksa, vst, kwn, vwt, b, s)
        xf = _mix_merge(xf, za, out_b, conv_a_w, pw, psc, sng, sgu_w, sb, w_br, w_out, l, s)
        act = _ffn_act(xf, n2, w_gu, conv_ff_w, l, s, tm)
        xf = _ffn_down(xf, act, w_dn, l, tm)
    return xf.reshape(b, s, d)
```

```python
import functools

import jax
import jax.numpy as jnp
import numpy as np
from jax import lax
from jax.experimental import pallas as pl
from jax.experimental.pallas import tpu as pltpu

F32 = jnp.float32
BF16 = jnp.bfloat16

HEAD_DIM = 64
NSA_HEADS = 8
NSA_GROUPS = 2
NSA_REP = NSA_HEADS // NSA_GROUPS
CMP_BLOCK = 32
CMP_STRIDE = 16
SLC_BLOCK = 64
SLC_TOPN = 16
WIN = 512
NEG = -1e30
FORCE = 1e6
RMS_EPS = 1e-6
LOG2E = 1.4426950408889634
POOL_WINDOWS = (2, 4, 8, 16)
SGU_CHUNK = 128
SGU_GROUPS = 4
BRANCH_WIDTH = 512
N_BRANCH = 4

LANES = 128
HALO = 16
QT = 512
SEL_CHUNK = 256
WIN_CHUNK = 128
IN_PROJ_COL_TILES = 4
FF_ROWS = 512
FF_CHUNK = 1408
RANK_STEP = 16
VROWS = HEAD_DIM + 16
BF16_ROWS = 16
VMEM_LIMIT = 56 * 1024 * 1024


def _params(n_axes):
    return pltpu.CompilerParams(dimension_semantics=("arbitrary",) * n_axes, vmem_limit_bytes=VMEM_LIMIT)


def _dot(a, b):
    return jnp.dot(a, b, preferred_element_type=F32)


def _layer_spec(arr, l, block=None, index=None):
    block = tuple(arr.shape[1:]) if block is None else tuple(block)
    index = index or (lambda *_: (0,) * len(block))
    return pl.BlockSpec((None,) + block, lambda *grid: (l,) + tuple(index(*grid)))


def _rms_bf16(x, g):
    ms = jnp.mean(x * x, axis=-1, keepdims=True)
    return ((x * lax.rsqrt(ms + RMS_EPS)) * g).astype(BF16)


def _erf_gelu(x):
    return 0.5 * x * (1.0 + lax.erf(x * (2.0 ** -0.5)))


def _norm_matmul_kernel(x_ref, g_ref, w_ref, o_ref, h_ref):
    @pl.when(pl.program_id(1) == 0)
    def _():
        h_ref[...] = _rms_bf16(x_ref[...], g_ref[...])

    o_ref[...] = _dot(h_ref[...], w_ref[...]).astype(o_ref.dtype)


def _norm_matmul(x, g, w, l, out_dtype, tm, tn, name):
    t, d = x.shape
    n = w.shape[2]
    return pl.pallas_call(
        _norm_matmul_kernel,
        out_shape=jax.ShapeDtypeStruct((t, n), out_dtype),
        grid=(t // tm, n // tn),
        in_specs=[
            pl.BlockSpec((tm, d), lambda i, j: (i, 0)),
            _layer_spec(g, l),
            _layer_spec(w, l, (d, tn), lambda i, j: (0, j)),
        ],
        out_specs=pl.BlockSpec((tm, tn), lambda i, j: (i, j)),
        scratch_shapes=[pltpu.VMEM((tm, d), BF16)],
        compiler_params=_params(2),
        name=name,
    )(x, g, w)


def _group_rms(x, g):
    lane = lax.broadcasted_iota(jnp.int32, x.shape, 1)
    lo = lane < HEAD_DIM
    x2 = x * x
    s_lo = jnp.sum(jnp.where(lo, x2, 0.0), axis=-1, keepdims=True)
    s_hi = jnp.sum(jnp.where(lo, 0.0, x2), axis=-1, keepdims=True)
    ms = jnp.where(lo, s_lo, s_hi) * (1.0 / HEAD_DIM)
    return (x * lax.rsqrt(ms + RMS_EPS)) * g


def _kv_prep_kernel(ks_ref, vs_ref, kw_ref, vw_ref, gs_ref, gw_ref, ksa_ref, vst_ref, kwn_ref, vwt_ref, *, ts):
    j = pl.program_id(1)
    ksn = _group_rms(ks_ref[...], gs_ref[...]).astype(BF16)
    row = lax.broadcasted_iota(jnp.int32, (ts, LANES), 0) + j * ts
    col = lax.broadcasted_iota(jnp.int32, (ts, LANES), 1)
    onehot = jnp.where(row // SLC_BLOCK == col, 1.0, 0.0).astype(BF16)
    ksa_ref[0, :, 0:LANES] = ksn
    ksa_ref[0, :, LANES:2 * LANES] = onehot
    kwn_ref[0] = _group_rms(kw_ref[...], gw_ref[...]).astype(BF16)
    vs = vs_ref[...]
    for c in range(ts // SEL_CHUNK):
        vst_ref[0, c] = _values_t(vs[c * SEL_CHUNK:(c + 1) * SEL_CHUNK, :])
    vw = vw_ref[...]
    for c in range(ts // WIN_CHUNK):
        vwt_ref[0, c] = _values_t(vw[c * WIN_CHUNK:(c + 1) * WIN_CHUNK, :])


def _values_t(v):
    vt = v.T
    keys = vt.shape[1]
    ones_row = jnp.where(lax.broadcasted_iota(jnp.int32, (VROWS - HEAD_DIM, keys), 0) == 0, 1.0, 0.0)
    parts = []
    for g in range(NSA_GROUPS):
        parts += [vt[g * HEAD_DIM:(g + 1) * HEAD_DIM, :], ones_row]
    return jnp.concatenate(parts, axis=0).astype(BF16)


def _kv_prep(znsa, gs, gw, l, b, s, ts=2048):
    ts = min(ts, s)
    nt = s // ts
    col = lambda c: pl.BlockSpec((ts, LANES), lambda bi, j, c=c: (bi * nt + j, c))
    vec = _layer_spec(gs, l)
    return pl.pallas_call(
        functools.partial(_kv_prep_kernel, ts=ts),
        out_shape=(
            jax.ShapeDtypeStruct((b, s, 2 * LANES), BF16),
            jax.ShapeDtypeStruct((b, s // SEL_CHUNK, NSA_GROUPS * VROWS, SEL_CHUNK), BF16),
            jax.ShapeDtypeStruct((b, s, LANES), BF16),
            jax.ShapeDtypeStruct((b, s // WIN_CHUNK, NSA_GROUPS * VROWS, WIN_CHUNK), BF16),
        ),
        grid=(b, nt),
        in_specs=[col(6), col(7), col(8), col(9), vec, vec],
        out_specs=(
            pl.BlockSpec((1, ts, 2 * LANES), lambda bi, j: (bi, j, 0)),
            pl.BlockSpec((1, ts // SEL_CHUNK, NSA_GROUPS * VROWS, SEL_CHUNK), lambda bi, j: (bi, j, 0, 0)),
            pl.BlockSpec((1, ts, LANES), lambda bi, j: (bi, j, 0)),
            pl.BlockSpec((1, ts // WIN_CHUNK, NSA_GROUPS * VROWS, WIN_CHUNK), lambda bi, j: (bi, j, 0, 0)),
        ),
        compiler_params=_params(2),
        name="kv_prep",
    )(znsa, znsa, znsa, znsa, gs, gw)


def _compress_kernel(kc_ref, vc_ref, pe_ref, w1_ref, w2_ref, g_ref, kco_ref, vct_ref, u_ref, *, nc):
    half = CMP_BLOCK // 2
    for j, src in enumerate((kc_ref, vc_ref)):
        for l in range(half):
            u_ref[:, l * LANES:(l + 1) * LANES] = src[pl.ds(l, nc, stride=CMP_STRIDE), :]
        u = u_ref[...]
        top = _dot((u + pe_ref[j, 0:1, :]).astype(BF16), w1_ref[j, 0])
        bot = _dot((u + pe_ref[j, 1:2, :]).astype(BF16), w1_ref[j, 1])
        pre = top + pltpu.roll(bot, nc - 1, 0)
        hid = pre * jax.nn.sigmoid(pre)
        out = _dot(hid.astype(BF16), w2_ref[j])
        if j == 0:
            kco_ref[0] = _group_rms(out, g_ref[...]).astype(BF16)
        else:
            vct_ref[0] = out.T.astype(BF16)


def _compress(znsa, pe2, w1bd, w2bd, g1, l, b, s):
    nc = s // CMP_STRIDE
    kw = (CMP_BLOCK // 2) * LANES
    return pl.pallas_call(
        functools.partial(_compress_kernel, nc=nc),
        out_shape=(
            jax.ShapeDtypeStruct((b, nc, LANES), BF16),
            jax.ShapeDtypeStruct((b, LANES, nc), BF16),
        ),
        grid=(b,),
        in_specs=[
            pl.BlockSpec((s, LANES), lambda bi: (bi, 4)),
            pl.BlockSpec((s, LANES), lambda bi: (bi, 5)),
            _layer_spec(pe2, l),
            _layer_spec(w1bd, l),
            _layer_spec(w2bd, l),
            _layer_spec(g1, l),
        ],
        out_specs=(
            pl.BlockSpec((1, nc, LANES), lambda bi: (bi, 0, 0)),
            pl.BlockSpec((1, LANES, nc), lambda bi: (bi, 0, 0)),
        ),
        scratch_shapes=[pltpu.VMEM((nc, kw), F32)],
        compiler_params=_params(1),
        name="compress",
    )(znsa, znsa, pe2, w1bd, w2bd, g1)


def _rank_desc(imp, ns):
    nq = imp.shape[1]
    sub = lax.broadcasted_iota(jnp.int32, (8, nq), 0)
    blocks = [imp[8 * a:8 * a + 8, :] for a in range(ns // 8)]
    rank = [jnp.zeros((8, nq), F32) for _ in blocks]
    for k in range(ns):
        ka, kr = divmod(k, 8)
        rk = jnp.broadcast_to(imp[k:k + 1, :], (8, nq))
        for a, blk in enumerate(blocks):
            if a > ka:
                ahead = rk >= blk
            elif a < ka:
                ahead = rk > blk
            else:
                rank[a] = rank[a] + jnp.where(sub > kr, jnp.where(rk >= blk, 1.0, 0.0), jnp.where(rk > blk, 1.0, 0.0))
                continue
            rank[a] = rank[a] + jnp.where(ahead, 1.0, 0.0)
    return jnp.concatenate(rank, axis=0)


def _nsa_attn_kernel(q_ref, gate_ref, gq_ref, kc_ref, vct_ref, ovt_ref, ksa_ref, vst_ref, kw_ref, vwt_ref,
                     o_ref, qa_ref, sa_ref, sb_ref, m_ref, acc_ref, ocmp_ref, owin_ref, imp_ref, *, ns, nc, topn):
    i = pl.program_id(1)
    start = i * QT
    nl = NSA_HEADS * QT
    gl = NSA_REP * QT

    qt = q_ref[...].T.reshape(NSA_HEADS, HEAD_DIM, QT)
    ms = jnp.mean(qt * qt, axis=1, keepdims=True)
    qn = ((qt * lax.rsqrt(ms + RMS_EPS)) * gq_ref[...][None]) * (HEAD_DIM ** -0.5 * LOG2E)
    qa_ref[...] = jnp.zeros(qa_ref.shape, BF16)
    for h in range(NSA_HEADS):
        g = h // NSA_REP
        qa_ref[g * HEAD_DIM:(g + 1) * HEAD_DIM, h * QT:(h + 1) * QT] = qn[h].astype(BF16)

    def tq(shape):
        return start + (lax.broadcasted_iota(jnp.int32, shape, 1) & (QT - 1))

    def krow(shape, base):
        return base + lax.broadcasted_iota(jnp.int32, shape, 0)

    def group_pv(vt, pb):
        rows = vt.shape[0] // NSA_GROUPS
        return jnp.concatenate(
            [_dot(vt[g * rows:(g + 1) * rows, :], pb[:, g * gl:(g + 1) * gl]) for g in range(NSA_GROUPS)],
            axis=1)

    def compressed(rows):
        qk = qa_ref[0:LANES, :]
        sc = _dot(kc_ref[0, 0:rows, :], qk)
        cval = krow((rows, nl), 0) * CMP_STRIDE + (CMP_BLOCK - 1) <= tq((rows, nl))
        sc = jnp.where(cval, sc, NEG)
        e = jnp.exp2(sc - jnp.max(sc, axis=0, keepdims=True))
        pc = e / jnp.sum(e, axis=0, keepdims=True)
        pc = jnp.where(tq((1, nl)) >= CMP_BLOCK - 1, pc, 0.0)
        ocmp_ref[...] = group_pv(vct_ref[0, :, 0:rows], pc.astype(BF16))
        ovt = ovt_ref[:, 0:rows]
        for g in range(NSA_GROUPS):
            psum = pc[:, g * gl:g * gl + QT]
            for r in range(1, NSA_REP):
                psum = psum + pc[:, g * gl + r * QT:g * gl + (r + 1) * QT]
            p_hi = psum.astype(BF16)
            rem = psum - p_hi.astype(F32)
            p_mid = rem.astype(BF16)
            p_lo = (rem - p_mid.astype(F32)).astype(BF16)
            imp_ref[g] = (_dot(ovt, p_hi) + _dot(ovt, p_mid)) + _dot(ovt, p_lo)

    def compressed_and_select():
        last_cmp = jnp.maximum(start + QT - CMP_BLOCK, 0) // CMP_STRIDE
        n_cv = max(nc // LANES, 1)
        for v in range(n_cv):
            @pl.when(jnp.minimum(last_cmp // LANES, n_cv - 1) == v)
            def _(v=v):
                compressed(min(LANES * (v + 1), nc))

        jb = krow((ns, QT), 0)
        jt = (start + lax.broadcasted_iota(jnp.int32, (ns, QT), 1)) // SLC_BLOCK
        forced = (jb == 0) | (jb == jt) | (jb == jt - 1)
        future = jb > jt
        last_block = (start + QT - 1) // SLC_BLOCK
        n_var = max(ns // RANK_STEP, 1)
        for g in range(NSA_GROUPS):
            imp = jnp.where(forced, FORCE, jnp.where(future, NEG, imp_ref[g]))
            for v in range(n_var):
                rows = RANK_STEP * (v + 1)

                @pl.when(jnp.minimum(last_block // RANK_STEP, n_var - 1) == v)
                def _(g=g, imp=imp, rows=rows):
                    sel = (_rank_desc(imp[0:rows, :], rows) < topn) & jnp.logical_not(future[0:rows, :])
                    bias = jnp.where(sel, 0.0, NEG)
                    if rows < ns:
                        bias = jnp.concatenate([bias, jnp.full((ns - rows, QT), NEG, F32)], axis=0)
                    for r in range(NSA_REP):
                        h = g * NSA_REP + r
                        qa_ref[LANES:LANES + ns, h * QT:(h + 1) * QT] = bias.astype(BF16)

    def col_max(s):
        part = jnp.max(s.reshape(s.shape[0] // BF16_ROWS, BF16_ROWS, nl), axis=0)
        return jnp.max(part.astype(F32), axis=0, keepdims=True)

    def probs(s, m):
        return jnp.exp2(s - m.astype(BF16))

    def normalised(acc):
        return acc[0:HEAD_DIM, :] / acc[HEAD_DIM:HEAD_DIM + 1, :]

    def update(s, vt):
        m_old = m_ref[...]
        m_new = jnp.maximum(m_old, col_max(s))
        pv = group_pv(vt, probs(s, m_new))
        acc_ref[...] = jnp.exp2(m_old - m_new) * acc_ref[...] + pv
        m_ref[...] = m_new

    n_back = WIN // WIN_CHUNK
    n_own = QT // WIN_CHUNK
    wshape = (WIN_CHUNK, nl)
    local_q = lax.broadcasted_iota(jnp.int32, wshape, 1) & (QT - 1)
    local_k = lax.broadcasted_iota(jnp.int32, wshape, 0)

    def window():
        ss, vts = [], []
        for c in range(n_back + n_own):
            off = c * WIN_CHUNK
            ci = i * n_own - n_back + c
            cl = jnp.maximum(ci, 0)
            k = kw_ref[0, pl.ds(pl.multiple_of(cl * WIN_CHUNK, WIN_CHUNK), WIN_CHUNK), :]
            s = _dot(k, qa_ref[0:LANES, :])
            if off < QT:
                s = jnp.where(local_k + off > local_q + jnp.where(ci >= 0, 0, QT + WIN_CHUNK), s, NEG)
            elif off + WIN_CHUNK > WIN:
                s = jnp.where(local_k + (off - WIN) <= local_q, s, NEG)
            elif c < n_back:
                s = s + jnp.where(ci >= 0, 0.0, NEG)
            ss.append(s.astype(BF16))
            vts.append(vwt_ref[0, cl])
        m = col_max(ss[0])
        for s in ss[1:]:
            m = jnp.maximum(m, col_max(s))
        pb = jnp.concatenate([probs(s, m) for s in ss], axis=0)
        owin_ref[...] = normalised(group_pv(jnp.concatenate(vts, axis=1), pb))

    compressed_and_select()
    window()

    def sel_scores(buf, c, own=False):
        k = ksa_ref[0, pl.ds(pl.multiple_of(c * SEL_CHUNK, SEL_CHUNK), SEL_CHUNK), :]
        s = _dot(k, qa_ref[...])
        if own:
            shape = (SEL_CHUNK, nl)
            s = jnp.where(krow(shape, c * SEL_CHUNK) <= tq(shape), s, NEG)
        buf[...] = s.astype(BF16)

    def sel_update(buf, c):
        update(buf[...], vst_ref[0, c])

    m_ref[...] = jnp.full(m_ref.shape, NEG, F32)
    acc_ref[...] = jnp.zeros(acc_ref.shape, F32)
    cd = start // SEL_CHUNK
    sel_scores(sa_ref, 0)

    def sel_pair(p, carry):
        c0 = 2 * p
        sel_scores(sb_ref, c0 + 1)
        sel_update(sa_ref, c0)
        sel_scores(sa_ref, jnp.minimum(c0 + 2, cd - 1))
        sel_update(sb_ref, c0 + 1)
        return carry

    lax.fori_loop(0, cd // 2, sel_pair, 0)

    @pl.when(cd % 2 == 1)
    def _():
        sel_update(sa_ref, cd - 1)

    for o in range(QT // SEL_CHUNK):
        buf = sa_ref if o % 2 else sb_ref
        sel_scores(buf, cd + o, own=True)
        sel_update(buf, cd + o)

    o_sel = normalised(acc_ref[...])
    o_cmp = ocmp_ref[...]
    o_win = owin_ref[...]

    sg = jax.nn.sigmoid(gate_ref[...].T[0:3 * NSA_HEADS, :])
    outs = []
    for h in range(NSA_HEADS):
        lanes = slice(h * QT, (h + 1) * QT)
        outs.append(sg[3 * h:3 * h + 1, :] * o_cmp[:, lanes] + sg[3 * h + 1:3 * h + 2, :] * o_sel[:, lanes]
                    + sg[3 * h + 2:3 * h + 3, :] * o_win[:, lanes])
    o_ref[...] = jnp.concatenate(outs, axis=0).T.astype(o_ref.dtype)


def _nsa_attn(znsa, gq, l, kc, vct, ovt, ksa, vst, kwn, vwt, b, s):
    ns = s // SLC_BLOCK
    nc = s // CMP_STRIDE
    nq = s // QT
    nl = NSA_HEADS * QT
    width = NSA_HEADS * HEAD_DIM
    kern = functools.partial(_nsa_attn_kernel, ns=ns, nc=nc, topn=min(SLC_TOPN, ns))
    return pl.pallas_call(
        kern,
        out_shape=jax.ShapeDtypeStruct((b * s, width), BF16),
        grid=(b, nq),
        in_specs=[
            pl.BlockSpec((QT, width), lambda bi, i: (bi * nq + i, 0)),
            pl.BlockSpec((QT, LANES), lambda bi, i: (bi * nq + i, 10)),
            _layer_spec(gq, l),
            pl.BlockSpec((1, nc, LANES), lambda bi, i: (bi, 0, 0)),
            pl.BlockSpec((1, LANES, nc), lambda bi, i: (bi, 0, 0)),
            pl.BlockSpec((ns, nc), lambda bi, i: (0, 0)),
            pl.BlockSpec((1, s, 2 * LANES), lambda bi, i: (bi, 0, 0)),
            pl.BlockSpec((1, s // SEL_CHUNK, NSA_GROUPS * VROWS, SEL_CHUNK), lambda bi, i: (bi, 0, 0, 0)),
            pl.BlockSpec((1, s, LANES), lambda bi, i: (bi, 0, 0)),
            pl.BlockSpec((1, s // WIN_CHUNK, NSA_GROUPS * VROWS, WIN_CHUNK), lambda bi, i: (bi, 0, 0, 0)),
        ],
        out_specs=pl.BlockSpec((QT, width), lambda bi, i: (bi * nq + i, 0)),
        scratch_shapes=[
            pltpu.VMEM((2 * LANES, nl), BF16),
            pltpu.VMEM((SEL_CHUNK, nl), BF16),
            pltpu.VMEM((SEL_CHUNK, nl), BF16),
            pltpu.VMEM((1, nl), F32),
            pltpu.VMEM((VROWS, nl), F32),
            pltpu.VMEM((HEAD_DIM, nl), F32),
            pltpu.VMEM((HEAD_DIM, nl), F32),
            pltpu.VMEM((NSA_GROUPS, ns, QT), F32),
        ],
        compiler_params=_params(2),
        name="nsa_attn",
    )(znsa, znsa, gq, kc, vct, ovt, ksa, vst, kwn, vwt)


def _mix_kernel(x_ref, ab_ref, ac_ref, ax_ref, ach_ref, axh_ref, p_ref, ph_ref, sgu_ref, mg0_ref, mg1_ref, mg2_ref,
                mg3_ref, ob_ref, cw_ref, pw_ref, psc_ref, sng_ref, sw_ref, sb_ref, wbr_ref, wo_ref, o_ref, ext_ref,
                ext2_ref, *, tm, tiles_per_seq):
    ti = pl.program_id(0) % tiles_per_seq
    keep = jnp.where(ti == 0, 0.0, 1.0)

    ext_ref[0:HALO, :] = (ach_ref[...].astype(F32) * axh_ref[...].astype(F32)) * keep
    ext_ref[HALO:, :] = ac_ref[...].astype(F32) * ax_ref[...].astype(F32)
    e = ext_ref[...]
    cw = cw_ref[...]
    conv = (cw[0:1, :] * pltpu.roll(e, 2, 0) + cw[1:2, :] * pltpu.roll(e, 1, 0)) + cw[2:3, :] * e
    out_a = ab_ref[...].astype(F32) * conv[HALO:, :]

    p = p_ref[...].astype(F32)
    ext2_ref[0:HALO, :] = ph_ref[...].astype(F32) * keep
    ext2_ref[HALO:, :] = p
    e = ext2_ref[...]
    gw = BRANCH_WIDTH // len(POOL_WINDOWS)
    cnt = (ti * tm + 1 + lax.broadcasted_iota(jnp.int32, (tm, gw), 0)).astype(F32)
    groups = []
    for gi, w in enumerate(POOL_WINDOWS):
        acc = e[:, gi * gw:(gi + 1) * gw]
        span = 1
        while span < w:
            acc = acc + pltpu.roll(acc, span, 0)
            span *= 2
        groups.append(acc[HALO:, :] / jnp.minimum(cnt, float(w)))
    pooled = jnp.concatenate(groups, axis=1) - p
    out_c = _dot(pooled.astype(BF16), pw_ref[...]) * psc_ref[...]

    z = _erf_gelu(sgu_ref[...].astype(F32))
    u = z[:, :BRANCH_WIDTH]
    v = z[:, BRANCH_WIDTH:]
    v = (v * lax.rsqrt(jnp.mean(v * v, axis=-1, keepdims=True) + RMS_EPS)) * sng_ref[...]
    vb = v.astype(BF16)
    nchunk = tm // SGU_CHUNK
    gd = BRANCH_WIDTH // SGU_GROUPS
    tri = (lax.broadcasted_iota(jnp.int32, (SGU_CHUNK, SGU_CHUNK), 0)
           >= lax.broadcasted_iota(jnp.int32, (SGU_CHUNK, SGU_CHUNK), 1))
    mixed_g = []
    for g in range(SGU_GROUPS):
        wg = jnp.where(tri, sw_ref[g], 0.0).astype(BF16)
        rhs = jnp.concatenate(
            [vb[c * SGU_CHUNK:(c + 1) * SGU_CHUNK, g * gd:(g + 1) * gd] for c in range(nchunk)], axis=1)
        res = _dot(wg, rhs)
        mixed_g.append(jnp.concatenate([res[:, c * gd:(c + 1) * gd] for c in range(nchunk)], axis=0))
    bias = jnp.concatenate([sb_ref[...]] * nchunk, axis=0)
    out_d = u * (jnp.concatenate(mixed_g, axis=1) + bias)

    merged = jax.nn.sigmoid(mg0_ref[...].astype(F32)) * _dot(out_a.astype(BF16), wbr_ref[0])
    merged = merged + jax.nn.sigmoid(mg1_ref[...].astype(F32)) * _dot(ob_ref[...], wbr_ref[1])
    merged = merged + jax.nn.sigmoid(mg2_ref[...].astype(F32)) * _dot(out_c.astype(BF16), wbr_ref[2])
    merged = merged + jax.nn.sigmoid(mg3_ref[...].astype(F32)) * _dot(out_d.astype(BF16), wbr_ref[3])
    o_ref[...] = x_ref[...] + _dot(merged.astype(BF16), wo_ref[...])


def _mix_merge(x, za, out_b, cw, pw, psc, sng, sw, sb, wbr, wo, l, s, tm=512):
    t, d = x.shape
    bw = BRANCH_WIDTH
    tps = s // tm
    hb = tm // HALO
    tile = lambda c: pl.BlockSpec((tm, bw), lambda i, c=c: (i, c))
    halo = lambda c: pl.BlockSpec((HALO, bw), lambda i, c=c: (jnp.maximum(i * hb - 1, 0), c))
    wide = lambda c: pl.BlockSpec((tm, d), lambda i, c=c: (i, c))
    full = lambda a: _layer_spec(a, l)
    consts = (cw, pw, psc, sng, sw, sb, wbr, wo)
    return pl.pallas_call(
        functools.partial(_mix_kernel, tm=tm, tiles_per_seq=tps),
        out_shape=jax.ShapeDtypeStruct((t, d), F32),
        grid=(t // tm,),
        in_specs=[wide(0), tile(0), tile(1), tile(2), halo(1), halo(2), tile(3), halo(3), wide(2),
                  wide(3), wide(4), wide(5), wide(6), tile(0)] + [full(a) for a in consts],
        out_specs=wide(0),
        scratch_shapes=[pltpu.VMEM((tm + HALO, bw), F32), pltpu.VMEM((tm + HALO, bw), F32)],
        compiler_params=_params(1),
        name="mix_merge",
    )(x, za, za, za, za, za, za, za, za, za, za, za, za, out_b, *consts)


def _ffn_act_kernel(x_ref, xh_ref, g_ref, wg_ref, wu_ref, cw_ref, o_ref, h_ref, hh_ref, ext_ref, *,
                    tiles_per_seq, rb):
    @pl.when(pl.program_id(1) == 0)
    def _():
        h_ref[...] = _rms_bf16(x_ref[...], g_ref[...])
        hh_ref[...] = _rms_bf16(xh_ref[...], g_ref[...])

    keep = jnp.where(pl.program_id(0) % tiles_per_seq == 0, 0.0, 1.0)
    wg = wg_ref[...]
    wu = wu_ref[...]
    cw = cw_ref[...]
    ext_ref[0:HALO, :] = _dot(hh_ref[...], wg) * keep
    for r0 in range(0, h_ref.shape[0], rb):
        h = h_ref[r0:r0 + rb, :]
        ext_ref[HALO + r0:HALO + r0 + rb, :] = _dot(h, wg)
        e = ext_ref[r0:r0 + rb + HALO, :]
        conv = ((cw[0:1, :] * pltpu.roll(e, 2, 0) + cw[1:2, :] * pltpu.roll(e, 1, 0)) + cw[2:3, :] * e)[HALO:, :]
        o_ref[r0:r0 + rb, :] = ((conv * jax.nn.sigmoid(conv)) * _dot(h, wu)).astype(o_ref.dtype)


def _ffn_act(x, g, w_up, cw, l, s, tm):
    t, d = x.shape
    dff = cw.shape[2]
    hb = tm // HALO
    nj = dff // FF_CHUNK
    return pl.pallas_call(
        functools.partial(_ffn_act_kernel, tiles_per_seq=s // tm, rb=min(tm, FF_ROWS)),
        out_shape=jax.ShapeDtypeStruct((t, dff), BF16),
        grid=(t // tm, dff // FF_CHUNK),
        in_specs=[
            pl.BlockSpec((tm, d), lambda i, j: (i, 0)),
            pl.BlockSpec((HALO, d), lambda i, j: (jnp.maximum(i * hb - 1, 0), 0)),
            _layer_spec(g, l),
            _layer_spec(w_up, l, (d, FF_CHUNK), lambda i, j: (0, j)),
            _layer_spec(w_up, l, (d, FF_CHUNK), lambda i, j: (0, nj + j)),
            _layer_spec(cw, l, (cw.shape[1], FF_CHUNK), lambda i, j: (0, j)),
        ],
        out_specs=pl.BlockSpec((tm, FF_CHUNK), lambda i, j: (i, j)),
        scratch_shapes=[pltpu.VMEM((tm, d), BF16), pltpu.VMEM((HALO, d), BF16), pltpu.VMEM((tm + HALO, FF_CHUNK), F32)],
        compiler_params=_params(2),
        name="ffn_act",
    )(x, x, g, w_up, w_up, cw)


def _ffn_down_kernel(x_ref, a_ref, wd_ref, o_ref):
    o_ref[...] = x_ref[...] + _dot(a_ref[...], wd_ref[...])


def _ffn_down(x, act, wd, l, tm):
    t, d = x.shape
    dff = wd.shape[1]
    return pl.pallas_call(
        _ffn_down_kernel,
        out_shape=jax.ShapeDtypeStruct((t, d), F32),
        grid=(t // tm,),
        in_specs=[
            pl.BlockSpec((tm, d), lambda i: (i, 0)),
            pl.BlockSpec((tm, dff), lambda i: (i, 0)),
            _layer_spec(wd, l),
        ],
        out_specs=pl.BlockSpec((tm, d), lambda i: (i, 0)),
        compiler_params=_params(1),
        name="ffn_down",
    )(x, act, wd)


def _overlap_t(nc, ns):
    n_cmp = nc - 1
    cs = np.arange(n_cmp) * CMP_STRIDE
    ce = cs + CMP_BLOCK
    ss = np.arange(ns) * SLC_BLOCK
    se = ss + SLC_BLOCK
    ov = np.clip(np.minimum(ce[:, None], se[None]) - np.maximum(cs[:, None], ss[None]), 0, None)
    out = np.zeros((ns, nc), np.float32)
    out[:, :n_cmp] = (ov.astype(np.float32) / CMP_BLOCK).T
    return out


def _compress_params(pe, w1, w2):
    depth = pe.shape[0]
    half = CMP_BLOCK // 2
    pe2 = jnp.tile(pe.reshape(depth, 2, 2, half, 1, HEAD_DIM), (1, 1, 1, 1, NSA_GROUPS, 1))
    pe2 = pe2.reshape(depth, 2, 2, half * LANES)
    w1r = w1.astype(BF16).reshape(depth, 2, 2, half, HEAD_DIM, HEAD_DIM)
    zero = jnp.zeros_like(w1r)
    w1bd = jnp.stack([jnp.concatenate([w1r, zero], axis=-1), jnp.concatenate([zero, w1r], axis=-1)], axis=4)
    w1bd = w1bd.reshape(depth, 2, 2, half * LANES, LANES)
    eye = jnp.eye(NSA_GROUPS, dtype=w2.dtype)
    w2bd = jnp.einsum("zjdo,gh->zjgdho", w2, eye).reshape(depth, 2, LANES, LANES)
    return pe2, w1bd, w2bd.astype(BF16)


def kernel(x, norm1_g, w_in, conv_a_w, qk_norm_g, cmp_pe, cmp_w1, cmp_w2, pool_w, pool_scale, sgu_norm_g, sgu_w,
           sgu_b, w_branch, w_o, norm2_g, w_up, conv_ff_w, w_down):
    b, s, d = x.shape
    depth = w_in.shape[0]
    t = b * s
    bw = BRANCH_WIDTH
    kvw = 6 * NSA_GROUPS * HEAD_DIM
    ngate = 3 * NSA_HEADS
    o_q = 3 * bw
    o_kv = o_q + bw
    o_gate = o_kv + kvw
    o_pool = o_gate + ngate
    dff = w_down.shape[1]
    ns = s // SLC_BLOCK
    nc = s // CMP_STRIDE
    ovt = jnp.asarray(_overlap_t(nc, ns), BF16)
    xf = x.reshape(t, d)
    tm = min(1024, t)
    tm_wide = min(2048, t)
    w_a = jnp.concatenate([w_in[:, :, :o_q], w_in[:, :, o_pool:]], axis=2).astype(BF16)
    w_n = jnp.concatenate([w_in[:, :, o_q:o_pool], jnp.zeros((depth, d, LANES - ngate), w_in.dtype)],
                          axis=2).astype(BF16)
    row = lambda v: v.reshape(depth, 1, v.shape[-1])
    g2 = lambda k: row(jnp.tile(qk_norm_g[:, k], (1, NSA_GROUPS)))
    g_cmp, g_sel, g_win = g2(1), g2(2), g2(3)
    gq = jnp.broadcast_to(qk_norm_g[:, 0][:, :, None], (depth, HEAD_DIM, QT))
    pe2, w1bd, w2bd = _compress_params(cmp_pe, cmp_w1, cmp_w2)
    n_pool = len(POOL_WINDOWS)
    pw = jnp.einsum("zgio,gh->zgiho", pool_w, jnp.eye(n_pool, dtype=pool_w.dtype)).reshape(depth, bw, bw).astype(BF16)
    sb = jnp.repeat(jnp.swapaxes(sgu_b, 1, 2), bw // SGU_GROUPS, axis=2)
    w_br, w_out, w_dn, w_gu = w_branch.astype(BF16), w_o.astype(BF16), w_down.astype(BF16), w_up.astype(BF16)
    n1, n2, psc, sng = row(norm1_g), row(norm2_g), row(pool_scale), row(sgu_norm_g)

    for l in range(depth):
        za = _norm_matmul(xf, n1, w_a, l, BF16, tm_wide, w_a.shape[2] // IN_PROJ_COL_TILES, "in_proj_a")
        znsa = _norm_matmul(xf, n1, w_n, l, F32, tm_wide, w_n.shape[2], "in_proj_nsa")
        ksa, vst, kwn, vwt = _kv_prep(znsa, g_sel, g_win, l, b, s)
        kc, vct = _compress(znsa, pe2, w1bd, w2bd, g_cmp, l, b, s)
        out_b = _nsa_attn(znsa, gq, l, kc, vct, ovt, ksa, vst, kwn, vwt, b, s)
        xf = _mix_merge(xf, za, out_b, conv_a_w, pw, psc, sng, sgu_w, sb, w_br, w_out, l, s)
        act = _ffn_act(xf, n2, w_gu, conv_ff_w, l, s, tm)
        xf = _ffn_down(xf, act, w_dn, l, tm)
    return xf.reshape(b, s, d)
```

```python
import functools

import jax
import jax.numpy as jnp
import numpy as np
from jax import lax
from jax.experimental import pallas as pl
from jax.experimental.pallas import tpu as pltpu

F32 = jnp.float32
BF16 = jnp.bfloat16

HEAD_DIM = 64
NSA_HEADS = 8
NSA_GROUPS = 2
NSA_REP = NSA_HEADS // NSA_GROUPS
CMP_BLOCK = 32
CMP_STRIDE = 16
SLC_BLOCK = 64
SLC_TOPN = 16
WIN = 512
NEG = -1e30
FORCE = 1e6
RMS_EPS = 1e-6
LOG2E = 1.4426950408889634
POOL_WINDOWS = (2, 4, 8, 16)
SGU_CHUNK = 128
SGU_GROUPS = 4
BRANCH_WIDTH = 512
N_BRANCH = 4

LANES = 128
HALO = 16
QT = 512
SEL_CHUNK = 256
WIN_CHUNK = 128
WIN_QT = 256
IN_PROJ_COL_TILES = 4
FF_ROWS = 512
FF_CHUNK = 1408
RANK_STEP = 16
VROWS = HEAD_DIM + 16
BF16_ROWS = 16
VMEM_LIMIT = 56 * 1024 * 1024


def _params(n_axes):
    return pltpu.CompilerParams(dimension_semantics=("arbitrary",) * n_axes, vmem_limit_bytes=VMEM_LIMIT)


def _dot(a, b):
    return jnp.dot(a, b, preferred_element_type=F32)


def _layer_spec(arr, l, block=None, index=None):
    block = tuple(arr.shape[1:]) if block is None else tuple(block)
    index = index or (lambda *_: (0,) * len(block))
    return pl.BlockSpec((None,) + block, lambda *grid: (l,) + tuple(index(*grid)))


def _rms_bf16(x, g):
    ms = jnp.mean(x * x, axis=-1, keepdims=True)
    return ((x * lax.rsqrt(ms + RMS_EPS)) * g).astype(BF16)


def _erf_gelu(x):
    return 0.5 * x * (1.0 + lax.erf(x * (2.0 ** -0.5)))


def _norm_matmul_kernel(x_ref, g_ref, w_ref, o_ref, h_ref):
    @pl.when(pl.program_id(1) == 0)
    def _():
        h_ref[...] = _rms_bf16(x_ref[...], g_ref[...])

    o_ref[...] = _dot(h_ref[...], w_ref[...]).astype(o_ref.dtype)


def _norm_matmul(x, g, w, l, out_dtype, tm, tn, name):
    t, d = x.shape
    n = w.shape[2]
    return pl.pallas_call(
        _norm_matmul_kernel,
        out_shape=jax.ShapeDtypeStruct((t, n), out_dtype),
        grid=(t // tm, n // tn),
        in_specs=[
            pl.BlockSpec((tm, d), lambda i, j: (i, 0)),
            _layer_spec(g, l),
            _layer_spec(w, l, (d, tn), lambda i, j: (0, j)),
        ],
        out_specs=pl.BlockSpec((tm, tn), lambda i, j: (i, j)),
        scratch_shapes=[pltpu.VMEM((tm, d), BF16)],
        compiler_params=_params(2),
        name=name,
    )(x, g, w)


def _group_rms(x, g):
    lane = lax.broadcasted_iota(jnp.int32, x.shape, 1)
    lo = lane < HEAD_DIM
    x2 = x * x
    s_lo = jnp.sum(jnp.where(lo, x2, 0.0), axis=-1, keepdims=True)
    s_hi = jnp.sum(jnp.where(lo, 0.0, x2), axis=-1, keepdims=True)
    ms = jnp.where(lo, s_lo, s_hi) * (1.0 / HEAD_DIM)
    return (x * lax.rsqrt(ms + RMS_EPS)) * g


def _kv_prep_kernel(ks_ref, vs_ref, kw_ref, vw_ref, gs_ref, gw_ref, ksa_ref, vst_ref, kwn_ref, vwt_ref, *, ts):
    j = pl.program_id(1)
    ksn = _group_rms(ks_ref[...], gs_ref[...]).astype(BF16)
    row = lax.broadcasted_iota(jnp.int32, (ts, LANES), 0) + j * ts
    col = lax.broadcasted_iota(jnp.int32, (ts, LANES), 1)
    onehot = jnp.where(row // SLC_BLOCK == col, 1.0, 0.0).astype(BF16)
    ksa_ref[0, :, 0:LANES] = ksn
    ksa_ref[0, :, LANES:2 * LANES] = onehot
    kwn_ref[0] = _group_rms(kw_ref[...], gw_ref[...]).astype(BF16)
    vs = vs_ref[...]
    for c in range(ts // SEL_CHUNK):
        vst_ref[0, c] = _values_t(vs[c * SEL_CHUNK:(c + 1) * SEL_CHUNK, :])
    vw = vw_ref[...]
    for c in range(ts // WIN_CHUNK):
        vwt_ref[0, c] = _values_t(vw[c * WIN_CHUNK:(c + 1) * WIN_CHUNK, :])


def _values_t(v):
    vt = v.T
    keys = vt.shape[1]
    ones_row = jnp.where(lax.broadcasted_iota(jnp.int32, (VROWS - HEAD_DIM, keys), 0) == 0, 1.0, 0.0)
    parts = []
    for g in range(NSA_GROUPS):
        parts += [vt[g * HEAD_DIM:(g + 1) * HEAD_DIM, :], ones_row]
    return jnp.concatenate(parts, axis=0).astype(BF16)


def _kv_prep(znsa, gs, gw, l, b, s, ts=2048):
    ts = min(ts, s)
    nt = s // ts
    col = lambda c: pl.BlockSpec((ts, LANES), lambda bi, j, c=c: (bi * nt + j, c))
    vec = _layer_spec(gs, l)
    return pl.pallas_call(
        functools.partial(_kv_prep_kernel, ts=ts),
        out_shape=(
            jax.ShapeDtypeStruct((b, s, 2 * LANES), BF16),
            jax.ShapeDtypeStruct((b, s // SEL_CHUNK, NSA_GROUPS * VROWS, SEL_CHUNK), BF16),
            jax.ShapeDtypeStruct((b, s, LANES), BF16),
            jax.ShapeDtypeStruct((b, s // WIN_CHUNK, NSA_GROUPS * VROWS, WIN_CHUNK), BF16),
        ),
        grid=(b, nt),
        in_specs=[col(6), col(7), col(8), col(9), vec, vec],
        out_specs=(
            pl.BlockSpec((1, ts, 2 * LANES), lambda bi, j: (bi, j, 0)),
            pl.BlockSpec((1, ts // SEL_CHUNK, NSA_GROUPS * VROWS, SEL_CHUNK), lambda bi, j: (bi, j, 0, 0)),
            pl.BlockSpec((1, ts, LANES), lambda bi, j: (bi, j, 0)),
            pl.BlockSpec((1, ts // WIN_CHUNK, NSA_GROUPS * VROWS, WIN_CHUNK), lambda bi, j: (bi, j, 0, 0)),
        ),
        compiler_params=_params(2),
        name="kv_prep",
    )(znsa, znsa, znsa, znsa, gs, gw)


def _compress_kernel(kc_ref, vc_ref, pe_ref, w1_ref, w2_ref, g_ref, kco_ref, vct_ref, u_ref, *, nc):
    half = CMP_BLOCK // 2
    for j, src in enumerate((kc_ref, vc_ref)):
        for l in range(half):
            u_ref[:, l * LANES:(l + 1) * LANES] = src[pl.ds(l, nc, stride=CMP_STRIDE), :]
        u = u_ref[...]
        top = _dot((u + pe_ref[j, 0:1, :]).astype(BF16), w1_ref[j, 0])
        bot = _dot((u + pe_ref[j, 1:2, :]).astype(BF16), w1_ref[j, 1])
        pre = top + pltpu.roll(bot, nc - 1, 0)
        hid = pre * jax.nn.sigmoid(pre)
        out = _dot(hid.astype(BF16), w2_ref[j])
        if j == 0:
            kco_ref[0] = _group_rms(out, g_ref[...]).astype(BF16)
        else:
            vct_ref[0] = out.T.astype(BF16)


def _compress(znsa, pe2, w1bd, w2bd, g1, l, b, s):
    nc = s // CMP_STRIDE
    kw = (CMP_BLOCK // 2) * LANES
    return pl.pallas_call(
        functools.partial(_compress_kernel, nc=nc),
        out_shape=(
            jax.ShapeDtypeStruct((b, nc, LANES), BF16),
            jax.ShapeDtypeStruct((b, LANES, nc), BF16),
        ),
        grid=(b,),
        in_specs=[
            pl.BlockSpec((s, LANES), lambda bi: (bi, 4)),
            pl.BlockSpec((s, LANES), lambda bi: (bi, 5)),
            _layer_spec(pe2, l),
            _layer_spec(w1bd, l),
            _layer_spec(w2bd, l),
            _layer_spec(g1, l),
        ],
        out_specs=(
            pl.BlockSpec((1, nc, LANES), lambda bi: (bi, 0, 0)),
            pl.BlockSpec((1, LANES, nc), lambda bi: (bi, 0, 0)),
        ),
        scratch_shapes=[pltpu.VMEM((nc, kw), F32)],
        compiler_params=_params(1),
        name="compress",
    )(znsa, znsa, pe2, w1bd, w2bd, g1)


def _rank_desc(imp, ns):
    nq = imp.shape[1]
    sub = lax.broadcasted_iota(jnp.int32, (8, nq), 0)
    blocks = [imp[8 * a:8 * a + 8, :] for a in range(ns // 8)]
    rank = [jnp.zeros((8, nq), F32) for _ in blocks]
    for k in range(ns):
        ka, kr = divmod(k, 8)
        rk = jnp.broadcast_to(imp[k:k + 1, :], (8, nq))
        for a, blk in enumerate(blocks):
            if a > ka:
                ahead = rk >= blk
            elif a < ka:
                ahead = rk > blk
            else:
                rank[a] = rank[a] + jnp.where(sub > kr, jnp.where(rk >= blk, 1.0, 0.0), jnp.where(rk > blk, 1.0, 0.0))
                continue
            rank[a] = rank[a] + jnp.where(ahead, 1.0, 0.0)
    return jnp.concatenate(rank, axis=0)


def _nsa_attn_kernel(q_ref, gate_ref, gq_ref, kc_ref, vct_ref, ovt_ref, ksa_ref, vst_ref, kw_ref, vwt_ref,
                     o_ref, qa_ref, sa_ref, sb_ref, m_ref, acc_ref, ocmp_ref, owin_ref, imp_ref, *, ns, nc, topn):
    i = pl.program_id(1)
    start = i * QT
    nl = NSA_HEADS * QT
    gl = NSA_REP * QT

    qt = q_ref[...].T.reshape(NSA_HEADS, HEAD_DIM, QT)
    ms = jnp.mean(qt * qt, axis=1, keepdims=True)
    qn = ((qt * lax.rsqrt(ms + RMS_EPS)) * gq_ref[...][None]) * (HEAD_DIM ** -0.5 * LOG2E)
    qa_ref[...] = jnp.zeros(qa_ref.shape, BF16)
    for h in range(NSA_HEADS):
        g = h // NSA_REP
        qa_ref[g * HEAD_DIM:(g + 1) * HEAD_DIM, h * QT:(h + 1) * QT] = qn[h].astype(BF16)

    def tq(shape):
        return start + (lax.broadcasted_iota(jnp.int32, shape, 1) & (QT - 1))

    def krow(shape, base):
        return base + lax.broadcasted_iota(jnp.int32, shape, 0)

    def group_pv(vt, pb):
        rows = vt.shape[0] // NSA_GROUPS
        lanes = pb.shape[1] // NSA_GROUPS
        return jnp.concatenate(
            [_dot(vt[g * rows:(g + 1) * rows, :], pb[:, g * lanes:(g + 1) * lanes]) for g in range(NSA_GROUPS)],
            axis=1)

    def compressed(rows):
        qk = qa_ref[0:LANES, :]
        sc = _dot(kc_ref[0, 0:rows, :], qk)
        cval = krow((rows, nl), 0) * CMP_STRIDE + (CMP_BLOCK - 1) <= tq((rows, nl))
        sc = jnp.where(cval, sc, NEG)
        e = jnp.exp2(sc - jnp.max(sc, axis=0, keepdims=True))
        pc = e / jnp.sum(e, axis=0, keepdims=True)
        pc = jnp.where(tq((1, nl)) >= CMP_BLOCK - 1, pc, 0.0)
        ocmp_ref[...] = group_pv(vct_ref[0, :, 0:rows], pc.astype(BF16))
        ovt = ovt_ref[:, 0:rows]
        for g in range(NSA_GROUPS):
            psum = pc[:, g * gl:g * gl + QT]
            for r in range(1, NSA_REP):
                psum = psum + pc[:, g * gl + r * QT:g * gl + (r + 1) * QT]
            p_hi = psum.astype(BF16)
            rem = psum - p_hi.astype(F32)
            p_mid = rem.astype(BF16)
            p_lo = (rem - p_mid.astype(F32)).astype(BF16)
            imp_ref[g] = (_dot(ovt, p_hi) + _dot(ovt, p_mid)) + _dot(ovt, p_lo)

    def compressed_and_select():
        last_cmp = jnp.maximum(start + QT - CMP_BLOCK, 0) // CMP_STRIDE
        n_cv = max(nc // LANES, 1)
        for v in range(n_cv):
            @pl.when(jnp.minimum(last_cmp // LANES, n_cv - 1) == v)
            def _(v=v):
                compressed(min(LANES * (v + 1), nc))

        jb = krow((ns, QT), 0)
        jt = (start + lax.broadcasted_iota(jnp.int32, (ns, QT), 1)) // SLC_BLOCK
        forced = (jb == 0) | (jb == jt) | (jb == jt - 1)
        future = jb > jt
        last_block = (start + QT - 1) // SLC_BLOCK
        n_var = max(ns // RANK_STEP, 1)
        for g in range(NSA_GROUPS):
            imp = jnp.where(forced, FORCE, jnp.where(future, NEG, imp_ref[g]))
            for v in range(n_var):
                rows = RANK_STEP * (v + 1)

                @pl.when(jnp.minimum(last_block // RANK_STEP, n_var - 1) == v)
                def _(g=g, imp=imp, rows=rows):
                    sel = (_rank_desc(imp[0:rows, :], rows) < topn) & jnp.logical_not(future[0:rows, :])
                    bias = jnp.where(sel, 0.0, NEG)
                    if rows < ns:
                        bias = jnp.concatenate([bias, jnp.full((ns - rows, QT), NEG, F32)], axis=0)
                    for r in range(NSA_REP):
                        h = g * NSA_REP + r
                        qa_ref[LANES:LANES + ns, h * QT:(h + 1) * QT] = bias.astype(BF16)

    def col_max(s):
        part = jnp.max(s.reshape(s.shape[0] // BF16_ROWS, BF16_ROWS, s.shape[1]), axis=0)
        return jnp.max(part.astype(F32), axis=0, keepdims=True)

    def probs(s, m):
        return jnp.exp2(s - m.astype(BF16))

    def normalised(acc):
        return acc[0:HEAD_DIM, :] / acc[HEAD_DIM:HEAD_DIM + 1, :]

    def update(s, vt):
        m_old = m_ref[...]
        m_new = jnp.maximum(m_old, col_max(s))
        pv = group_pv(vt, probs(s, m_new))
        acc_ref[...] = jnp.exp2(m_old - m_new) * acc_ref[...] + pv
        m_ref[...] = m_new

    wq = min(QT, WIN_QT)
    n_back = WIN // WIN_CHUNK
    n_own = wq // WIN_CHUNK
    wshape = (WIN_CHUNK, NSA_HEADS * wq)
    local_q = lax.broadcasted_iota(jnp.int32, wshape, 1) & (wq - 1)
    local_k = lax.broadcasted_iota(jnp.int32, wshape, 0)

    def window(a):
        lanes = [slice(h * QT + a * wq, h * QT + (a + 1) * wq) for h in range(NSA_HEADS)]
        qk = jnp.concatenate([qa_ref[0:LANES, ln] for ln in lanes], axis=1)
        ss, vts = [], []
        for c in range(n_back + n_own):
            off = c * WIN_CHUNK
            ci = (i * (QT // wq) + a) * n_own - n_back + c
            cl = jnp.maximum(ci, 0)
            k = kw_ref[0, pl.ds(pl.multiple_of(cl * WIN_CHUNK, WIN_CHUNK), WIN_CHUNK), :]
            s = _dot(k, qk)
            if off < wq:
                s = jnp.where(local_k + off > local_q + jnp.where(ci >= 0, 0, wq + WIN_CHUNK), s, NEG)
            elif off + WIN_CHUNK > WIN:
                s = jnp.where(local_k + (off - WIN) <= local_q, s, NEG)
            elif c < n_back:
                s = s + jnp.where(ci >= 0, 0.0, NEG)
            ss.append(s.astype(BF16))
            vts.append(vwt_ref[0, cl])
        m = col_max(ss[0])
        for s in ss[1:]:
            m = jnp.maximum(m, col_max(s))
        pb = jnp.concatenate([probs(s, m) for s in ss], axis=0)
        out = normalised(group_pv(jnp.concatenate(vts, axis=1), pb))
        for h, ln in enumerate(lanes):
            owin_ref[:, ln] = out[:, h * wq:(h + 1) * wq]

    compressed_and_select()
    for a in range(QT // wq):
        window(a)

    def sel_scores(buf, c, own=False):
        k = ksa_ref[0, pl.ds(pl.multiple_of(c * SEL_CHUNK, SEL_CHUNK), SEL_CHUNK), :]
        s = _dot(k, qa_ref[...])
        if own:
            shape = (SEL_CHUNK, nl)
            s = jnp.where(krow(shape, c * SEL_CHUNK) <= tq(shape), s, NEG)
        buf[...] = s.astype(BF16)

    def sel_update(buf, c):
        update(buf[...], vst_ref[0, c])

    m_ref[...] = jnp.full(m_ref.shape, NEG, F32)
    acc_ref[...] = jnp.zeros(acc_ref.shape, F32)
    cd = start // SEL_CHUNK
    sel_scores(sa_ref, 0)

    def sel_pair(p, carry):
        c0 = 2 * p
        sel_scores(sb_ref, c0 + 1)
        sel_update(sa_ref, c0)
        sel_scores(sa_ref, jnp.minimum(c0 + 2, cd - 1))
        sel_update(sb_ref, c0 + 1)
        return carry

    lax.fori_loop(0, cd // 2, sel_pair, 0)

    @pl.when(cd % 2 == 1)
    def _():
        sel_update(sa_ref, cd - 1)

    for o in range(QT // SEL_CHUNK):
        buf = sa_ref if o % 2 else sb_ref
        sel_scores(buf, cd + o, own=True)
        sel_update(buf, cd + o)

    o_sel = normalised(acc_ref[...])
    o_cmp = ocmp_ref[...]
    o_win = owin_ref[...]

    sg = jax.nn.sigmoid(gate_ref[...].T[0:3 * NSA_HEADS, :])
    outs = []
    for h in range(NSA_HEADS):
        lanes = slice(h * QT, (h + 1) * QT)
        outs.append(sg[3 * h:3 * h + 1, :] * o_cmp[:, lanes] + sg[3 * h + 1:3 * h + 2, :] * o_sel[:, lanes]
                    + sg[3 * h + 2:3 * h + 3, :] * o_win[:, lanes])
    o_ref[...] = jnp.concatenate(outs, axis=0).T.astype(o_ref.dtype)


def _nsa_attn(znsa, gq, l, kc, vct, ovt, ksa, vst, kwn, vwt, b, s):
    ns = s // SLC_BLOCK
    nc = s // CMP_STRIDE
    nq = s // QT
    nl = NSA_HEADS * QT
    width = NSA_HEADS * HEAD_DIM
    kern = functools.partial(_nsa_attn_kernel, ns=ns, nc=nc, topn=min(SLC_TOPN, ns))
    return pl.pallas_call(
        kern,
        out_shape=jax.ShapeDtypeStruct((b * s, width), BF16),
        grid=(b, nq),
        in_specs=[
            pl.BlockSpec((QT, width), lambda bi, i: (bi * nq + i, 0)),
            pl.BlockSpec((QT, LANES), lambda bi, i: (bi * nq + i, 10)),
            _layer_spec(gq, l),
            pl.BlockSpec((1, nc, LANES), lambda bi, i: (bi, 0, 0)),
            pl.BlockSpec((1, LANES, nc), lambda bi, i: (bi, 0, 0)),
            pl.BlockSpec((ns, nc), lambda bi, i: (0, 0)),
            pl.BlockSpec((1, s, 2 * LANES), lambda bi, i: (bi, 0, 0)),
            pl.BlockSpec((1, s // SEL_CHUNK, NSA_GROUPS * VROWS, SEL_CHUNK), lambda bi, i: (bi, 0, 0, 0)),
            pl.BlockSpec((1, s, LANES), lambda bi, i: (bi, 0, 0)),
            pl.BlockSpec((1, s // WIN_CHUNK, NSA_GROUPS * VROWS, WIN_CHUNK), lambda bi, i: (bi, 0, 0, 0)),
        ],
        out_specs=pl.BlockSpec((QT, width), lambda bi, i: (bi * nq + i, 0)),
        scratch_shapes=[
            pltpu.VMEM((2 * LANES, nl), BF16),
            pltpu.VMEM((SEL_CHUNK, nl), BF16),
            pltpu.VMEM((SEL_CHUNK, nl), BF16),
            pltpu.VMEM((1, nl), F32),
            pltpu.VMEM((VROWS, nl), F32),
            pltpu.VMEM((HEAD_DIM, nl), F32),
            pltpu.VMEM((HEAD_DIM, nl), F32),
            pltpu.VMEM((NSA_GROUPS, ns, QT), F32),
        ],
        compiler_params=_params(2),
        name="nsa_attn",
    )(znsa, znsa, gq, kc, vct, ovt, ksa, vst, kwn, vwt)


def _mix_kernel(x_ref, ab_ref, ac_ref, ax_ref, ach_ref, axh_ref, p_ref, ph_ref, sgu_ref, mg0_ref, mg1_ref, mg2_ref,
                mg3_ref, ob_ref, cw_ref, pw_ref, psc_ref, sng_ref, sw_ref, sb_ref, wbr_ref, wo_ref, o_ref, ext_ref,
                ext2_ref, *, tm, tiles_per_seq):
    ti = pl.program_id(0) % tiles_per_seq
    keep = jnp.where(ti == 0, 0.0, 1.0)

    ext_ref[0:HALO, :] = (ach_ref[...].astype(F32) * axh_ref[...].astype(F32)) * keep
    ext_ref[HALO:, :] = ac_ref[...].astype(F32) * ax_ref[...].astype(F32)
    e = ext_ref[...]
    cw = cw_ref[...]
    conv = (cw[0:1, :] * pltpu.roll(e, 2, 0) + cw[1:2, :] * pltpu.roll(e, 1, 0)) + cw[2:3, :] * e
    out_a = ab_ref[...].astype(F32) * conv[HALO:, :]

    p = p_ref[...].astype(F32)
    ext2_ref[0:HALO, :] = ph_ref[...].astype(F32) * keep
    ext2_ref[HALO:, :] = p
    e = ext2_ref[...]
    gw = BRANCH_WIDTH // len(POOL_WINDOWS)
    cnt = (ti * tm + 1 + lax.broadcasted_iota(jnp.int32, (tm, gw), 0)).astype(F32)
    groups = []
    for gi, w in enumerate(POOL_WINDOWS):
        acc = e[:, gi * gw:(gi + 1) * gw]
        span = 1
        while span < w:
            acc = acc + pltpu.roll(acc, span, 0)
            span *= 2
        groups.append(acc[HALO:, :] / jnp.minimum(cnt, float(w)))
    pooled = jnp.concatenate(groups, axis=1) - p
    out_c = _dot(pooled.astype(BF16), pw_ref[...]) * psc_ref[...]

    z = _erf_gelu(sgu_ref[...].astype(F32))
    u = z[:, :BRANCH_WIDTH]
    v = z[:, BRANCH_WIDTH:]
    v = (v * lax.rsqrt(jnp.mean(v * v, axis=-1, keepdims=True) + RMS_EPS)) * sng_ref[...]
    vb = v.astype(BF16)
    nchunk = tm // SGU_CHUNK
    gd = BRANCH_WIDTH // SGU_GROUPS
    tri = (lax.broadcasted_iota(jnp.int32, (SGU_CHUNK, SGU_CHUNK), 0)
           >= lax.broadcasted_iota(jnp.int32, (SGU_CHUNK, SGU_CHUNK), 1))
    mixed_g = []
    for g in range(SGU_GROUPS):
        wg = jnp.where(tri, sw_ref[g], 0.0).astype(BF16)
        rhs = jnp.concatenate(
            [vb[c * SGU_CHUNK:(c + 1) * SGU_CHUNK, g * gd:(g + 1) * gd] for c in range(nchunk)], axis=1)
        res = _dot(wg, rhs)
        mixed_g.append(jnp.concatenate([res[:, c * gd:(c + 1) * gd] for c in range(nchunk)], axis=0))
    bias = jnp.concatenate([sb_ref[...]] * nchunk, axis=0)
    out_d = u * (jnp.concatenate(mixed_g, axis=1) + bias)

    merged = jax.nn.sigmoid(mg0_ref[...].astype(F32)) * _dot(out_a.astype(BF16), wbr_ref[0])
    merged = merged + jax.nn.sigmoid(mg1_ref[...].astype(F32)) * _dot(ob_ref[...], wbr_ref[1])
    merged = merged + jax.nn.sigmoid(mg2_ref[...].astype(F32)) * _dot(out_c.astype(BF16), wbr_ref[2])
    merged = merged + jax.nn.sigmoid(mg3_ref[...].astype(F32)) * _dot(out_d.astype(BF16), wbr_ref[3])
    o_ref[...] = x_ref[...] + _dot(merged.astype(BF16), wo_ref[...])


def _mix_merge(x, za, out_b, cw, pw, psc, sng, sw, sb, wbr, wo, l, s, tm=512):
    t, d = x.shape
    bw = BRANCH_WIDTH
    tps = s // tm
    hb = tm // HALO
    tile = lambda c: pl.BlockSpec((tm, bw), lambda i, c=c: (i, c))
    halo = lambda c: pl.BlockSpec((HALO, bw), lambda i, c=c: (jnp.maximum(i * hb - 1, 0), c))
    wide = lambda c: pl.BlockSpec((tm, d), lambda i, c=c: (i, c))
    full = lambda a: _layer_spec(a, l)
    consts = (cw, pw, psc, sng, sw, sb, wbr, wo)
    return pl.pallas_call(
        functools.partial(_mix_kernel, tm=tm, tiles_per_seq=tps),
        out_shape=jax.ShapeDtypeStruct((t, d), F32),
        grid=(t // tm,),
        in_specs=[wide(0), tile(0), tile(1), tile(2), halo(1), halo(2), tile(3), halo(3), wide(2),
                  wide(3), wide(4), wide(5), wide(6), tile(0)] + [full(a) for a in consts],
        out_specs=wide(0),
        scratch_shapes=[pltpu.VMEM((tm + HALO, bw), F32), pltpu.VMEM((tm + HALO, bw), F32)],
        compiler_params=_params(1),
        name="mix_merge",
    )(x, za, za, za, za, za, za, za, za, za, za, za, za, out_b, *consts)


def _ffn_act_kernel(x_ref, xh_ref, g_ref, wg_ref, wu_ref, cw_ref, o_ref, h_ref, hh_ref, ext_ref, *,
                    tiles_per_seq, rb):
    @pl.when(pl.program_id(1) == 0)
    def _():
        h_ref[...] = _rms_bf16(x_ref[...], g_ref[...])
        hh_ref[...] = _rms_bf16(xh_ref[...], g_ref[...])

    keep = jnp.where(pl.program_id(0) % tiles_per_seq == 0, 0.0, 1.0)
    wg = wg_ref[...]
    wu = wu_ref[...]
    cw = cw_ref[...]
    ext_ref[0:HALO, :] = _dot(hh_ref[...], wg) * keep
    for r0 in range(0, h_ref.shape[0], rb):
        h = h_ref[r0:r0 + rb, :]
        ext_ref[HALO + r0:HALO + r0 + rb, :] = _dot(h, wg)
        e = ext_ref[r0:r0 + rb + HALO, :]
        conv = ((cw[0:1, :] * pltpu.roll(e, 2, 0) + cw[1:2, :] * pltpu.roll(e, 1, 0)) + cw[2:3, :] * e)[HALO:, :]
        o_ref[r0:r0 + rb, :] = ((conv * jax.nn.sigmoid(conv)) * _dot(h, wu)).astype(o_ref.dtype)


def _ffn_act(x, g, w_up, cw, l, s, tm):
    t, d = x.shape
    dff = cw.shape[2]
    hb = tm // HALO
    nj = dff // FF_CHUNK
    return pl.pallas_call(
        functools.partial(_ffn_act_kernel, tiles_per_seq=s // tm, rb=min(tm, FF_ROWS)),
        out_shape=jax.ShapeDtypeStruct((t, dff), BF16),
        grid=(t // tm, dff // FF_CHUNK),
        in_specs=[
            pl.BlockSpec((tm, d), lambda i, j: (i, 0)),
            pl.BlockSpec((HALO, d), lambda i, j: (jnp.maximum(i * hb - 1, 0), 0)),
            _layer_spec(g, l),
            _layer_spec(w_up, l, (d, FF_CHUNK), lambda i, j: (0, j)),
            _layer_spec(w_up, l, (d, FF_CHUNK), lambda i, j: (0, nj + j)),
            _layer_spec(cw, l, (cw.shape[1], FF_CHUNK), lambda i, j: (0, j)),
        ],
        out_specs=pl.BlockSpec((tm, FF_CHUNK), lambda i, j: (i, j)),
        scratch_shapes=[pltpu.VMEM((tm, d), BF16), pltpu.VMEM((HALO, d), BF16), pltpu.VMEM((tm + HALO, FF_CHUNK), F32)],
        compiler_params=_params(2),
        name="ffn_act",
    )(x, x, g, w_up, w_up, cw)


def _ffn_down_kernel(x_ref, a_ref, wd_ref, o_ref):
    o_ref[...] = x_ref[...] + _dot(a_ref[...], wd_ref[...])


def _ffn_down(x, act, wd, l, tm):
    t, d = x.shape
    dff = wd.shape[1]
    return pl.pallas_call(
        _ffn_down_kernel,
        out_shape=jax.ShapeDtypeStruct((t, d), F32),
        grid=(t // tm,),
        in_specs=[
            pl.BlockSpec((tm, d), lambda i: (i, 0)),
            pl.BlockSpec((tm, dff), lambda i: (i, 0)),
            _layer_spec(wd, l),
        ],
        out_specs=pl.BlockSpec((tm, d), lambda i: (i, 0)),
        compiler_params=_params(1),
        name="ffn_down",
    )(x, act, wd)


def _overlap_t(nc, ns):
    n_cmp = nc - 1
    cs = np.arange(n_cmp) * CMP_STRIDE
    ce = cs + CMP_BLOCK
    ss = np.arange(ns) * SLC_BLOCK
    se = ss + SLC_BLOCK
    ov = np.clip(np.minimum(ce[:, None], se[None]) - np.maximum(cs[:, None], ss[None]), 0, None)
    out = np.zeros((ns, nc), np.float32)
    out[:, :n_cmp] = (ov.astype(np.float32) / CMP_BLOCK).T
    return out


def _compress_params(pe, w1, w2):
    depth = pe.shape[0]
    half = CMP_BLOCK // 2
    pe2 = jnp.tile(pe.reshape(depth, 2, 2, half, 1, HEAD_DIM), (1, 1, 1, 1, NSA_GROUPS, 1))
    pe2 = pe2.reshape(depth, 2, 2, half * LANES)
    w1r = w1.astype(BF16).reshape(depth, 2, 2, half, HEAD_DIM, HEAD_DIM)
    zero = jnp.zeros_like(w1r)
    w1bd = jnp.stack([jnp.concatenate([w1r, zero], axis=-1), jnp.concatenate([zero, w1r], axis=-1)], axis=4)
    w1bd = w1bd.reshape(depth, 2, 2, half * LANES, LANES)
    eye = jnp.eye(NSA_GROUPS, dtype=w2.dtype)
    w2bd = jnp.einsum("zjdo,gh->zjgdho", w2, eye).reshape(depth, 2, LANES, LANES)
    return pe2, w1bd, w2bd.astype(BF16)


def kernel(x, norm1_g, w_in, conv_a_w, qk_norm_g, cmp_pe, cmp_w1, cmp_w2, pool_w, pool_scale, sgu_norm_g, sgu_w,
           sgu_b, w_branch, w_o, norm2_g, w_up, conv_ff_w, w_down):
    b, s, d = x.shape
    depth = w_in.shape[0]
    t = b * s
    bw = BRANCH_WIDTH
    kvw = 6 * NSA_GROUPS * HEAD_DIM
    ngate = 3 * NSA_HEADS
    o_q = 3 * bw
    o_kv = o_q + bw
    o_gate = o_kv + kvw
    o_pool = o_gate + ngate
    dff = w_down.shape[1]
    ns = s // SLC_BLOCK
    nc = s // CMP_STRIDE
    ovt = jnp.asarray(_overlap_t(nc, ns), BF16)
    xf = x.reshape(t, d)
    tm = min(1024, t)
    tm_wide = min(2048, t)
    w_a = jnp.concatenate([w_in[:, :, :o_q], w_in[:, :, o_pool:]], axis=2).astype(BF16)
    w_n = jnp.concatenate([w_in[:, :, o_q:o_pool], jnp.zeros((depth, d, LANES - ngate), w_in.dtype)],
                          axis=2).astype(BF16)
    row = lambda v: v.reshape(depth, 1, v.shape[-1])
    g2 = lambda k: row(jnp.tile(qk_norm_g[:, k], (1, NSA_GROUPS)))
    g_cmp, g_sel, g_win = g2(1), g2(2), g2(3)
    gq = jnp.broadcast_to(qk_norm_g[:, 0][:, :, None], (depth, HEAD_DIM, QT))
    pe2, w1bd, w2bd = _compress_params(cmp_pe, cmp_w1, cmp_w2)
    n_pool = len(POOL_WINDOWS)
    pw = jnp.einsum("zgio,gh->zgiho", pool_w, jnp.eye(n_pool, dtype=pool_w.dtype)).reshape(depth, bw, bw).astype(BF16)
    sb = jnp.repeat(jnp.swapaxes(sgu_b, 1, 2), bw // SGU_GROUPS, axis=2)
    w_br, w_out, w_dn, w_gu = w_branch.astype(BF16), w_o.astype(BF16), w_down.astype(BF16), w_up.astype(BF16)
    n1, n2, psc, sng = row(norm1_g), row(norm2_g), row(pool_scale), row(sgu_norm_g)

    for l in range(depth):
        za = _norm_matmul(xf, n1, w_a, l, BF16, tm_wide, w_a.shape[2] // IN_PROJ_COL_TILES, "in_proj_a")
        znsa = _norm_matmul(xf, n1, w_n, l, F32, tm_wide, w_n.shape[2], "in_proj_nsa")
        ksa, vst, kwn, vwt = _kv_prep(znsa, g_sel, g_win, l, b, s)
        kc, vct = _compress(znsa, pe2, w1bd, w2bd, g_cmp, l, b, s)
        out_b = _nsa_attn(znsa, gq, l, kc, vct, ovt, ksa, vst, kwn, vwt, b, s)
        xf = _mix_merge(xf, za, out_b, conv_a_w, pw, psc, sng, sgu_w, sb, w_br, w_out, l, s)
        act = _ffn_act(xf, n2, w_gu, conv_ff_w, l, s, tm)
        xf = _ffn_down(xf, act, w_dn, l, tm)
    return xf.reshape(b, s, d)
```

```python
import functools

import jax
import jax.numpy as jnp
import numpy as np
from jax import lax
from jax.experimental import pallas as pl
from jax.experimental.pallas import tpu as pltpu

F32 = jnp.float32
BF16 = jnp.bfloat16

HEAD_DIM = 64
NSA_HEADS = 8
NSA_GROUPS = 2
NSA_REP = NSA_HEADS // NSA_GROUPS
CMP_BLOCK = 32
CMP_STRIDE = 16
SLC_BLOCK = 64
SLC_TOPN = 16
WIN = 512
NEG = -1e30
FORCE = 1e6
RMS_EPS = 1e-6
LOG2E = 1.4426950408889634
POOL_WINDOWS = (2, 4, 8, 16)
SGU_CHUNK = 128
SGU_GROUPS = 4
BRANCH_WIDTH = 512
N_BRANCH = 4

LANES = 128
HALO = 16
QT = 512
SEL_CHUNK = 256
WIN_CHUNK = 128
WIN_QT = 256
IN_PROJ_COL_TILES = 4
FF_ROWS = 512
FF_CHUNK = 1408
RANK_STEP = 16
VROWS = HEAD_DIM + 16
BF16_ROWS = 16
VMEM_LIMIT = 56 * 1024 * 1024


def _params(n_axes):
    return pltpu.CompilerParams(dimension_semantics=("arbitrary",) * n_axes, vmem_limit_bytes=VMEM_LIMIT)


def _dot(a, b):
    return jnp.dot(a, b, preferred_element_type=F32)


def _layer_spec(arr, l, block=None, index=None):
    block = tuple(arr.shape[1:]) if block is None else tuple(block)
    index = index or (lambda *_: (0,) * len(block))
    return pl.BlockSpec((None,) + block, lambda *grid: (l,) + tuple(index(*grid)))


def _rms_bf16(x, g):
    ms = jnp.mean(x * x, axis=-1, keepdims=True)
    return ((x * lax.rsqrt(ms + RMS_EPS)) * g).astype(BF16)


def _erf_gelu(x):
    return 0.5 * x * (1.0 + lax.erf(x * (2.0 ** -0.5)))


def _norm_matmul_kernel(x_ref, g_ref, w_ref, o_ref, h_ref):
    @pl.when(pl.program_id(1) == 0)
    def _():
        h_ref[...] = _rms_bf16(x_ref[...], g_ref[...])

    o_ref[...] = _dot(h_ref[...], w_ref[...]).astype(o_ref.dtype)


def _norm_matmul(x, g, w, l, out_dtype, tm, tn, name):
    t, d = x.shape
    n = w.shape[2]
    return pl.pallas_call(
        _norm_matmul_kernel,
        out_shape=jax.ShapeDtypeStruct((t, n), out_dtype),
        grid=(t // tm, n // tn),
        in_specs=[
            pl.BlockSpec((tm, d), lambda i, j: (i, 0)),
            _layer_spec(g, l),
            _layer_spec(w, l, (d, tn), lambda i, j: (0, j)),
        ],
        out_specs=pl.BlockSpec((tm, tn), lambda i, j: (i, j)),
        scratch_shapes=[pltpu.VMEM((tm, d), BF16)],
        compiler_params=_params(2),
        name=name,
    )(x, g, w)


def _group_rms(x, g):
    lane = lax.broadcasted_iota(jnp.int32, x.shape, 1)
    lo = lane < HEAD_DIM
    x2 = x * x
    s_lo = jnp.sum(jnp.where(lo, x2, 0.0), axis=-1, keepdims=True)
    s_hi = jnp.sum(jnp.where(lo, 0.0, x2), axis=-1, keepdims=True)
    ms = jnp.where(lo, s_lo, s_hi) * (1.0 / HEAD_DIM)
    return (x * lax.rsqrt(ms + RMS_EPS)) * g


def _kv_prep_kernel(ks_ref, vs_ref, kw_ref, vw_ref, gs_ref, gw_ref, ksa_ref, vst_ref, kwn_ref, vwt_ref, *, ts):
    j = pl.program_id(1)
    ksn = _group_rms(ks_ref[...], gs_ref[...]).astype(BF16)
    row = lax.broadcasted_iota(jnp.int32, (ts, LANES), 0) + j * ts
    col = lax.broadcasted_iota(jnp.int32, (ts, LANES), 1)
    onehot = jnp.where(row // SLC_BLOCK == col, 1.0, 0.0).astype(BF16)
    ksa_ref[0, :, 0:LANES] = ksn
    ksa_ref[0, :, LANES:2 * LANES] = onehot
    kwn_ref[0] = _group_rms(kw_ref[...], gw_ref[...]).astype(BF16)
    vs = vs_ref[...]
    for c in range(ts // SEL_CHUNK):
        vst_ref[0, c] = _values_t(vs[c * SEL_CHUNK:(c + 1) * SEL_CHUNK, :])
    vw = vw_ref[...]
    for c in range(ts // WIN_CHUNK):
        vwt_ref[0, c] = _values_t(vw[c * WIN_CHUNK:(c + 1) * WIN_CHUNK, :])


def _values_t(v):
    vt = v.T
    keys = vt.shape[1]
    ones_row = jnp.where(lax.broadcasted_iota(jnp.int32, (VROWS - HEAD_DIM, keys), 0) == 0, 1.0, 0.0)
    parts = []
    for g in range(NSA_GROUPS):
        parts += [vt[g * HEAD_DIM:(g + 1) * HEAD_DIM, :], ones_row]
    return jnp.concatenate(parts, axis=0).astype(BF16)


def _kv_prep(znsa, gs, gw, l, b, s, ts=2048):
    ts = min(ts, s)
    nt = s // ts
    col = lambda c: pl.BlockSpec((ts, LANES), lambda bi, j, c=c: (bi * nt + j, c))
    vec = _layer_spec(gs, l)
    return pl.pallas_call(
        functools.partial(_kv_prep_kernel, ts=ts),
        out_shape=(
            jax.ShapeDtypeStruct((b, s, 2 * LANES), BF16),
            jax.ShapeDtypeStruct((b, s // SEL_CHUNK, NSA_GROUPS * VROWS, SEL_CHUNK), BF16),
            jax.ShapeDtypeStruct((b, s, LANES), BF16),
            jax.ShapeDtypeStruct((b, s // WIN_CHUNK, NSA_GROUPS * VROWS, WIN_CHUNK), BF16),
        ),
        grid=(b, nt),
        in_specs=[col(6), col(7), col(8), col(9), vec, vec],
        out_specs=(
            pl.BlockSpec((1, ts, 2 * LANES), lambda bi, j: (bi, j, 0)),
            pl.BlockSpec((1, ts // SEL_CHUNK, NSA_GROUPS * VROWS, SEL_CHUNK), lambda bi, j: (bi, j, 0, 0)),
            pl.BlockSpec((1, ts, LANES), lambda bi, j: (bi, j, 0)),
            pl.BlockSpec((1, ts // WIN_CHUNK, NSA_GROUPS * VROWS, WIN_CHUNK), lambda bi, j: (bi, j, 0, 0)),
        ),
        compiler_params=_params(2),
        name="kv_prep",
    )(znsa, znsa, znsa, znsa, gs, gw)


def _compress_kernel(kc_ref, vc_ref, pe_ref, w1_ref, w2_ref, g_ref, kco_ref, vct_ref, u_ref, *, nc):
    half = CMP_BLOCK // 2
    for j, src in enumerate((kc_ref, vc_ref)):
        for l in range(half):
            u_ref[:, l * LANES:(l + 1) * LANES] = src[pl.ds(l, nc, stride=CMP_STRIDE), :]
        u = u_ref[...]
        top = _dot((u + pe_ref[j, 0:1, :]).astype(BF16), w1_ref[j, 0])
        bot = _dot((u + pe_ref[j, 1:2, :]).astype(BF16), w1_ref[j, 1])
        pre = top + pltpu.roll(bot, nc - 1, 0)
        hid = pre * jax.nn.sigmoid(pre)
        out = _dot(hid.astype(BF16), w2_ref[j])
        if j == 0:
            kco_ref[0] = _group_rms(out, g_ref[...]).astype(BF16)
        else:
            vct_ref[0] = out.T.astype(BF16)


def _compress(znsa, pe2, w1bd, w2bd, g1, l, b, s):
    nc = s // CMP_STRIDE
    kw = (CMP_BLOCK // 2) * LANES
    return pl.pallas_call(
        functools.partial(_compress_kernel, nc=nc),
        out_shape=(
            jax.ShapeDtypeStruct((b, nc, LANES), BF16),
            jax.ShapeDtypeStruct((b, LANES, nc), BF16),
        ),
        grid=(b,),
        in_specs=[
            pl.BlockSpec((s, LANES), lambda bi: (bi, 4)),
            pl.BlockSpec((s, LANES), lambda bi: (bi, 5)),
            _layer_spec(pe2, l),
            _layer_spec(w1bd, l),
            _layer_spec(w2bd, l),
            _layer_spec(g1, l),
        ],
        out_specs=(
            pl.BlockSpec((1, nc, LANES), lambda bi: (bi, 0, 0)),
            pl.BlockSpec((1, LANES, nc), lambda bi: (bi, 0, 0)),
        ),
        scratch_shapes=[pltpu.VMEM((nc, kw), F32)],
        compiler_params=_params(1),
        name="compress",
    )(znsa, znsa, pe2, w1bd, w2bd, g1)


def _rank_desc(imp, ns):
    nq = imp.shape[1]
    sub = lax.broadcasted_iota(jnp.int32, (8, nq), 0)
    blocks = [imp[8 * a:8 * a + 8, :] for a in range(ns // 8)]
    rank = [jnp.zeros((8, nq), F32) for _ in blocks]
    for k in range(ns):
        ka, kr = divmod(k, 8)
        rk = jnp.broadcast_to(imp[k:k + 1, :], (8, nq))
        for a, blk in enumerate(blocks):
            if a > ka:
                ahead = rk >= blk
            elif a < ka:
                ahead = rk > blk
            else:
                rank[a] = rank[a] + jnp.where(sub > kr, jnp.where(rk >= blk, 1.0, 0.0), jnp.where(rk > blk, 1.0, 0.0))
                continue
            rank[a] = rank[a] + jnp.where(ahead, 1.0, 0.0)
    return jnp.concatenate(rank, axis=0)


def _nsa_attn_kernel(q_ref, gate_ref, gq_ref, kc_ref, vct_ref, ovt_ref, ksa_ref, vst_ref, kw_ref, vwt_ref,
                     o_ref, qa_ref, sa_ref, sb_ref, m_ref, acc_ref, ocmp_ref, owin_ref, imp_ref, *, ns, nc, topn):
    i = pl.program_id(1)
    start = i * QT
    nl = NSA_HEADS * QT
    gl = NSA_REP * QT

    qt = q_ref[...].T.reshape(NSA_HEADS, HEAD_DIM, QT)
    ms = jnp.mean(qt * qt, axis=1, keepdims=True)
    qn = ((qt * lax.rsqrt(ms + RMS_EPS)) * gq_ref[...][None]) * (HEAD_DIM ** -0.5 * LOG2E)
    qa_ref[...] = jnp.zeros(qa_ref.shape, BF16)
    for h in range(NSA_HEADS):
        g = h // NSA_REP
        qa_ref[g * HEAD_DIM:(g + 1) * HEAD_DIM, h * QT:(h + 1) * QT] = qn[h].astype(BF16)

    def tq(shape):
        return start + (lax.broadcasted_iota(jnp.int32, shape, 1) & (QT - 1))

    def krow(shape, base):
        return base + lax.broadcasted_iota(jnp.int32, shape, 0)

    def group_pv(vt, pb):
        rows = vt.shape[0] // NSA_GROUPS
        lanes = pb.shape[1] // NSA_GROUPS
        return jnp.concatenate(
            [_dot(vt[g * rows:(g + 1) * rows, :], pb[:, g * lanes:(g + 1) * lanes]) for g in range(NSA_GROUPS)],
            axis=1)

    def compressed(rows):
        qk = qa_ref[0:LANES, :]
        sc = _dot(kc_ref[0, 0:rows, :], qk)
        cval = krow((rows, nl), 0) * CMP_STRIDE + (CMP_BLOCK - 1) <= tq((rows, nl))
        sc = jnp.where(cval, sc, NEG)
        e = jnp.exp2(sc - jnp.max(sc, axis=0, keepdims=True))
        pc = e / jnp.sum(e, axis=0, keepdims=True)
        pc = jnp.where(tq((1, nl)) >= CMP_BLOCK - 1, pc, 0.0)
        ocmp_ref[...] = group_pv(vct_ref[0, :, 0:rows], pc.astype(BF16))
        ovt = ovt_ref[:, 0:rows]
        for g in range(NSA_GROUPS):
            psum = pc[:, g * gl:g * gl + QT]
            for r in range(1, NSA_REP):
                psum = psum + pc[:, g * gl + r * QT:g * gl + (r + 1) * QT]
            p_hi = psum.astype(BF16)
            rem = psum - p_hi.astype(F32)
            p_mid = rem.astype(BF16)
            p_lo = (rem - p_mid.astype(F32)).astype(BF16)
            imp_ref[g] = (_dot(ovt, p_hi) + _dot(ovt, p_mid)) + _dot(ovt, p_lo)

    def compressed_and_select():
        last_cmp = jnp.maximum(start + QT - CMP_BLOCK, 0) // CMP_STRIDE
        n_cv = max(nc // LANES, 1)
        for v in range(n_cv):
            @pl.when(jnp.minimum(last_cmp // LANES, n_cv - 1) == v)
            def _(v=v):
                compressed(min(LANES * (v + 1), nc))

        jb = krow((ns, QT), 0)
        jt = (start + lax.broadcasted_iota(jnp.int32, (ns, QT), 1)) // SLC_BLOCK
        forced = (jb == 0) | (jb == jt) | (jb == jt - 1)
        future = jb > jt
        last_block = (start + QT - 1) // SLC_BLOCK
        n_var = max(ns // RANK_STEP, 1)
        for g in range(NSA_GROUPS):
            imp = jnp.where(forced, FORCE, jnp.where(future, NEG, imp_ref[g]))
            for v in range(n_var):
                rows = RANK_STEP * (v + 1)

                @pl.when(jnp.minimum(last_block // RANK_STEP, n_var - 1) == v)
                def _(g=g, imp=imp, rows=rows):
                    sel = (_rank_desc(imp[0:rows, :], rows) < topn) & jnp.logical_not(future[0:rows, :])
                    bias = jnp.where(sel, 0.0, NEG)
                    if rows < ns:
                        bias = jnp.concatenate([bias, jnp.full((ns - rows, QT), NEG, F32)], axis=0)
                    for r in range(NSA_REP):
                        h = g * NSA_REP + r
                        qa_ref[LANES:LANES + ns, h * QT:(h + 1) * QT] = bias.astype(BF16)

    def col_max(s):
        part = jnp.max(s.reshape(s.shape[0] // BF16_ROWS, BF16_ROWS, s.shape[1]), axis=0)
        return jnp.max(part.astype(F32), axis=0, keepdims=True)

    def probs(s, m):
        return jnp.exp2(s - m.astype(BF16))

    def normalised(acc):
        return acc[0:HEAD_DIM, :] / acc[HEAD_DIM:HEAD_DIM + 1, :]

    def update(s, vt):
        m_old = m_ref[...]
        m_new = jnp.maximum(m_old, col_max(s))
        pv = group_pv(vt, probs(s, m_new))
        acc_ref[...] = jnp.exp2(m_old - m_new) * acc_ref[...] + pv
        m_ref[...] = m_new

    wq = min(QT, WIN_QT)
    n_back = WIN // WIN_CHUNK
    n_own = wq // WIN_CHUNK
    wshape = (WIN_CHUNK, NSA_HEADS * wq)
    local_q = lax.broadcasted_iota(jnp.int32, wshape, 1) & (wq - 1)
    local_k = lax.broadcasted_iota(jnp.int32, wshape, 0)

    def window(a):
        lanes = [slice(h * QT + a * wq, h * QT + (a + 1) * wq) for h in range(NSA_HEADS)]
        qk = jnp.concatenate([qa_ref[0:LANES, ln] for ln in lanes], axis=1)
        ss, vts = [], []
        for c in range(n_back + n_own):
            off = c * WIN_CHUNK
            ci = (i * (QT // wq) + a) * n_own - n_back + c
            cl = jnp.maximum(ci, 0)
            k = kw_ref[0, pl.ds(pl.multiple_of(cl * WIN_CHUNK, WIN_CHUNK), WIN_CHUNK), :]
            s = _dot(k, qk)
            if off < wq:
                s = jnp.where(local_k + off > local_q + jnp.where(ci >= 0, 0, wq + WIN_CHUNK), s, NEG)
            elif off + WIN_CHUNK > WIN:
                s = jnp.where(local_k + (off - WIN) <= local_q, s, NEG)
            elif c < n_back:
                s = s + jnp.where(ci >= 0, 0.0, NEG)
            ss.append(s.astype(BF16))
            vts.append(vwt_ref[0, cl])
        m = col_max(ss[0])
        for s in ss[1:]:
            m = jnp.maximum(m, col_max(s))
        pb = jnp.concatenate([probs(s, m) for s in ss], axis=0)
        out = normalised(group_pv(jnp.concatenate(vts, axis=1), pb))
        for h, ln in enumerate(lanes):
            owin_ref[:, ln] = out[:, h * wq:(h + 1) * wq]

    compressed_and_select()
    for a in range(QT // wq):
        window(a)

    def sel_scores(buf, c, own=False):
        k = ksa_ref[0, pl.ds(pl.multiple_of(c * SEL_CHUNK, SEL_CHUNK), SEL_CHUNK), :]
        s = _dot(k, qa_ref[...])
        if own:
            shape = (SEL_CHUNK, nl)
            s = jnp.where(krow(shape, c * SEL_CHUNK) <= tq(shape), s, NEG)
        buf[...] = s.astype(BF16)

    def sel_update(buf, c):
        update(buf[...], vst_ref[0, c])

    m_ref[...] = jnp.full(m_ref.shape, NEG, F32)
    acc_ref[...] = jnp.zeros(acc_ref.shape, F32)
    cd = start // SEL_CHUNK
    sel_scores(sa_ref, 0)

    def sel_pair(p, carry):
        c0 = 2 * p
        sel_scores(sb_ref, c0 + 1)
        sel_update(sa_ref, c0)
        sel_scores(sa_ref, jnp.minimum(c0 + 2, cd - 1))
        sel_update(sb_ref, c0 + 1)
        return carry

    lax.fori_loop(0, cd // 2, sel_pair, 0)

    @pl.when(cd % 2 == 1)
    def _():
        sel_update(sa_ref, cd - 1)

    for o in range(QT // SEL_CHUNK):
        buf = sa_ref if o % 2 else sb_ref
        sel_scores(buf, cd + o, own=True)
        sel_update(buf, cd + o)

    o_sel = normalised(acc_ref[...])
    o_cmp = ocmp_ref[...]
    o_win = owin_ref[...]

    sg = jax.nn.sigmoid(gate_ref[...].T[0:3 * NSA_HEADS, :])
    outs = []
    for h in range(NSA_HEADS):
        lanes = slice(h * QT, (h + 1) * QT)
        outs.append(sg[3 * h:3 * h + 1, :] * o_cmp[:, lanes] + sg[3 * h + 1:3 * h + 2, :] * o_sel[:, lanes]
                    + sg[3 * h + 2:3 * h + 3, :] * o_win[:, lanes])
    o_ref[...] = jnp.concatenate(outs, axis=0).T.astype(o_ref.dtype)


def _nsa_attn(znsa, gq, l, kc, vct, ovt, ksa, vst, kwn, vwt, b, s):
    ns = s // SLC_BLOCK
    nc = s // CMP_STRIDE
    nq = s // QT
    nl = NSA_HEADS * QT
    width = NSA_HEADS * HEAD_DIM
    kern = functools.partial(_nsa_attn_kernel, ns=ns, nc=nc, topn=min(SLC_TOPN, ns))
    return pl.pallas_call(
        kern,
        out_shape=jax.ShapeDtypeStruct((b * s, width), BF16),
        grid=(b, nq),
        in_specs=[
            pl.BlockSpec((QT, width), lambda bi, i: (bi * nq + i, 0)),
            pl.BlockSpec((QT, LANES), lambda bi, i: (bi * nq + i, 10)),
            _layer_spec(gq, l),
            pl.BlockSpec((1, nc, LANES), lambda bi, i: (bi, 0, 0)),
            pl.BlockSpec((1, LANES, nc), lambda bi, i: (bi, 0, 0)),
            pl.BlockSpec((ns, nc), lambda bi, i: (0, 0)),
            pl.BlockSpec((1, s, 2 * LANES), lambda bi, i: (bi, 0, 0)),
            pl.BlockSpec((1, s // SEL_CHUNK, NSA_GROUPS * VROWS, SEL_CHUNK), lambda bi, i: (bi, 0, 0, 0)),
            pl.BlockSpec((1, s, LANES), lambda bi, i: (bi, 0, 0)),
            pl.BlockSpec((1, s // WIN_CHUNK, NSA_GROUPS * VROWS, WIN_CHUNK), lambda bi, i: (bi, 0, 0, 0)),
        ],
        out_specs=pl.BlockSpec((QT, width), lambda bi, i: (bi * nq + i, 0)),
        scratch_shapes=[
            pltpu.VMEM((2 * LANES, nl), BF16),
            pltpu.VMEM((SEL_CHUNK, nl), BF16),
            pltpu.VMEM((SEL_CHUNK, nl), BF16),
            pltpu.VMEM((1, nl), F32),
            pltpu.VMEM((VROWS, nl), F32),
            pltpu.VMEM((HEAD_DIM, nl), F32),
            pltpu.VMEM((HEAD_DIM, nl), F32),
            pltpu.VMEM((NSA_GROUPS, ns, QT), F32),
        ],
        compiler_params=_params(2),
        name="nsa_attn",
    )(znsa, znsa, gq, kc, vct, ovt, ksa, vst, kwn, vwt)


def _mix_kernel(x_ref, ab_ref, ac_ref, ax_ref, ach_ref, axh_ref, p_ref, ph_ref, sgu_ref, mg0_ref, mg1_ref, mg2_ref,
                mg3_ref, ob_ref, cw_ref, pw_ref, psc_ref, sng_ref, sw_ref, sb_ref, wbr_ref, wo_ref, o_ref, ext_ref,
                ext2_ref, *, tm, tiles_per_seq):
    ti = pl.program_id(0) % tiles_per_seq
    keep = jnp.where(ti == 0, 0.0, 1.0)

    ext_ref[0:HALO, :] = (ach_ref[...].astype(F32) * axh_ref[...].astype(F32)) * keep
    ext_ref[HALO:, :] = ac_ref[...].astype(F32) * ax_ref[...].astype(F32)
    e = ext_ref[...]
    cw = cw_ref[...]
    conv = (cw[0:1, :] * pltpu.roll(e, 2, 0) + cw[1:2, :] * pltpu.roll(e, 1, 0)) + cw[2:3, :] * e
    out_a = ab_ref[...].astype(F32) * conv[HALO:, :]

    p = p_ref[...].astype(F32)
    ext2_ref[0:HALO, :] = ph_ref[...].astype(F32) * keep
    ext2_ref[HALO:, :] = p
    e = ext2_ref[...]
    gw = BRANCH_WIDTH // len(POOL_WINDOWS)
    cnt = (ti * tm + 1 + lax.broadcasted_iota(jnp.int32, (tm, gw), 0)).astype(F32)
    groups = []
    for gi, w in enumerate(POOL_WINDOWS):
        acc = e[:, gi * gw:(gi + 1) * gw]
        span = 1
        while span < w:
            acc = acc + pltpu.roll(acc, span, 0)
            span *= 2
        groups.append(acc[HALO:, :] / jnp.minimum(cnt, float(w)))
    pooled = jnp.concatenate(groups, axis=1) - p
    out_c = _dot(pooled.astype(BF16), pw_ref[...]) * psc_ref[...]

    z = _erf_gelu(sgu_ref[...].astype(F32))
    u = z[:, :BRANCH_WIDTH]
    v = z[:, BRANCH_WIDTH:]
    v = (v * lax.rsqrt(jnp.mean(v * v, axis=-1, keepdims=True) + RMS_EPS)) * sng_ref[...]
    vb = v.astype(BF16)
    nchunk = tm // SGU_CHUNK
    gd = BRANCH_WIDTH // SGU_GROUPS
    tri = (lax.broadcasted_iota(jnp.int32, (SGU_CHUNK, SGU_CHUNK), 0)
           >= lax.broadcasted_iota(jnp.int32, (SGU_CHUNK, SGU_CHUNK), 1))
    mixed_g = []
    for g in range(SGU_GROUPS):
        wg = jnp.where(tri, sw_ref[g], 0.0).astype(BF16)
        rhs = jnp.concatenate(
            [vb[c * SGU_CHUNK:(c + 1) * SGU_CHUNK, g * gd:(g + 1) * gd] for c in range(nchunk)], axis=1)
        res = _dot(wg, rhs)
        mixed_g.append(jnp.concatenate([res[:, c * gd:(c + 1) * gd] for c in range(nchunk)], axis=0))
    bias = jnp.concatenate([sb_ref[...]] * nchunk, axis=0)
    out_d = u * (jnp.concatenate(mixed_g, axis=1) + bias)

    merged = jax.nn.sigmoid(mg0_ref[...].astype(F32)) * _dot(out_a.astype(BF16), wbr_ref[0])
    merged = merged + jax.nn.sigmoid(mg1_ref[...].astype(F32)) * _dot(ob_ref[...], wbr_ref[1])
    merged = merged + jax.nn.sigmoid(mg2_ref[...].astype(F32)) * _dot(out_c.astype(BF16), wbr_ref[2])
    merged = merged + jax.nn.sigmoid(mg3_ref[...].astype(F32)) * _dot(out_d.astype(BF16), wbr_ref[3])
    o_ref[...] = x_ref[...] + _dot(merged.astype(BF16), wo_ref[...])


def _mix_merge(x, za, out_b, cw, pw, psc, sng, sw, sb, wbr, wo, l, s, tm=512):
    t, d = x.shape
    bw = BRANCH_WIDTH
    tps = s // tm
    hb = tm // HALO
    tile = lambda c: pl.BlockSpec((tm, bw), lambda i, c=c: (i, c))
    halo = lambda c: pl.BlockSpec((HALO, bw), lambda i, c=c: (jnp.maximum(i * hb - 1, 0), c))
    wide = lambda c: pl.BlockSpec((tm, d), lambda i, c=c: (i, c))
    full = lambda a: _layer_spec(a, l)
    consts = (cw, pw, psc, sng, sw, sb, wbr, wo)
    return pl.pallas_call(
        functools.partial(_mix_kernel, tm=tm, tiles_per_seq=tps),
        out_shape=jax.ShapeDtypeStruct((t, d), F32),
        grid=(t // tm,),
        in_specs=[wide(0), tile(0), tile(1), tile(2), halo(1), halo(2), tile(3), halo(3), wide(2),
                  wide(3), wide(4), wide(5), wide(6), tile(0)] + [full(a) for a in consts],
        out_specs=wide(0),
        scratch_shapes=[pltpu.VMEM((tm + HALO, bw), F32), pltpu.VMEM((tm + HALO, bw), F32)],
        compiler_params=_params(1),
        name="mix_merge",
    )(x, za, za, za, za, za, za, za, za, za, za, za, za, out_b, *consts)


def _ffn_act_kernel(x_ref, xh_ref, g_ref, wg_ref, wu_ref, cw_ref, o_ref, h_ref, hh_ref, ext_ref, *,
                    tiles_per_seq, rb):
    @pl.when(pl.program_id(1) == 0)
    def _():
        h_ref[...] = _rms_bf16(x_ref[...], g_ref[...])
        hh_ref[...] = _rms_bf16(xh_ref[...], g_ref[...])

    keep = jnp.where(pl.program_id(0) % tiles_per_seq == 0, 0.0, 1.0)
    wg = wg_ref[...]
    wu = wu_ref[...]
    cw = cw_ref[...]
    ext_ref[0:HALO, :] = _dot(hh_ref[...], wg) * keep
    for r0 in range(0, h_ref.shape[0], rb):
        h = h_ref[r0:r0 + rb, :]
        ext_ref[HALO + r0:HALO + r0 + rb, :] = _dot(h, wg)
        e = ext_ref[r0:r0 + rb + HALO, :]
        conv = ((cw[0:1, :] * pltpu.roll(e, 2, 0) + cw[1:2, :] * pltpu.roll(e, 1, 0)) + cw[2:3, :] * e)[HALO:, :]
        o_ref[r0:r0 + rb, :] = ((conv * jax.nn.sigmoid(conv)) * _dot(h, wu)).astype(o_ref.dtype)


def _ffn_act(x, g, w_up, cw, l, s, tm):
    t, d = x.shape
    dff = cw.shape[2]
    hb = tm // HALO
    nj = dff // FF_CHUNK
    return pl.pallas_call(
        functools.partial(_ffn_act_kernel, tiles_per_seq=s // tm, rb=min(tm, FF_ROWS)),
        out_shape=jax.ShapeDtypeStruct((t, dff), BF16),
        grid=(t // tm, dff // FF_CHUNK),
        in_specs=[
            pl.BlockSpec((tm, d), lambda i, j: (i, 0)),
            pl.BlockSpec((HALO, d), lambda i, j: (jnp.maximum(i * hb - 1, 0), 0)),
            _layer_spec(g, l),
            _layer_spec(w_up, l, (d, FF_CHUNK), lambda i, j: (0, j)),
            _layer_spec(w_up, l, (d, FF_CHUNK), lambda i, j: (0, nj + j)),
            _layer_spec(cw, l, (cw.shape[1], FF_CHUNK), lambda i, j: (0, j)),
        ],
        out_specs=pl.BlockSpec((tm, FF_CHUNK), lambda i, j: (i, j)),
        scratch_shapes=[pltpu.VMEM((tm, d), BF16), pltpu.VMEM((HALO, d), BF16), pltpu.VMEM((tm + HALO, FF_CHUNK), F32)],
        compiler_params=_params(2),
        name="ffn_act",
    )(x, x, g, w_up, w_up, cw)


def _ffn_down_kernel(x_ref, a_ref, wd_ref, o_ref):
    o_ref[...] = x_ref[...] + _dot(a_ref[...], wd_ref[...])


def _ffn_down(x, act, wd, l, tm):
    t, d = x.shape
    dff = wd.shape[1]
    return pl.pallas_call(
        _ffn_down_kernel,
        out_shape=jax.ShapeDtypeStruct((t, d), F32),
        grid=(t // tm,),
        in_specs=[
            pl.BlockSpec((tm, d), lambda i: (i, 0)),
            pl.BlockSpec((tm, dff), lambda i: (i, 0)),
            _layer_spec(wd, l),
        ],
        out_specs=pl.BlockSpec((tm, d), lambda i: (i, 0)),
        compiler_params=_params(1),
        name="ffn_down",
    )(x, act, wd)


def _overlap_t(nc, ns):
    n_cmp = nc - 1
    cs = np.arange(n_cmp) * CMP_STRIDE
    ce = cs + CMP_BLOCK
    ss = np.arange(ns) * SLC_BLOCK
    se = ss + SLC_BLOCK
    ov = np.clip(np.minimum(ce[:, None], se[None]) - np.maximum(cs[:, None], ss[None]), 0, None)
    out = np.zeros((ns, nc), np.float32)
    out[:, :n_cmp] = (ov.astype(np.float32) / CMP_BLOCK).T
    return out


def _compress_params(pe, w1, w2):
    depth = pe.shape[0]
    half = CMP_BLOCK // 2
    pe2 = jnp.tile(pe.reshape(depth, 2, 2, half, 1, HEAD_DIM), (1, 1, 1, 1, NSA_GROUPS, 1))
    pe2 = pe2.reshape(depth, 2, 2, half * LANES)
    w1r = w1.astype(BF16).reshape(depth, 2, 2, half, HEAD_DIM, HEAD_DIM)
    zero = jnp.zeros_like(w1r)
    w1bd = jnp.stack([jnp.concatenate([w1r, zero], axis=-1), jnp.concatenate([zero, w1r], axis=-1)], axis=4)
    w1bd = w1bd.reshape(depth, 2, 2, half * LANES, LANES)
    eye = jnp.eye(NSA_GROUPS, dtype=w2.dtype)
    w2bd = jnp.einsum("zjdo,gh->zjgdho", w2, eye).reshape(depth, 2, LANES, LANES)
    return pe2, w1bd, w2bd.astype(BF16)


def kernel(x, norm1_g, w_in, conv_a_w, qk_norm_g, cmp_pe, cmp_w1, cmp_w2, pool_w, pool_scale, sgu_norm_g, sgu_w,
           sgu_b, w_branch, w_o, norm2_g, w_up, conv_ff_w, w_down):
    b, s, d = x.shape
    depth = w_in.shape[0]
    t = b * s
    bw = BRANCH_WIDTH
    kvw = 6 * NSA_GROUPS * HEAD_DIM
    ngate = 3 * NSA_HEADS
    o_q = 3 * bw
    o_kv = o_q + bw
    o_gate = o_kv + kvw
    o_pool = o_gate + ngate
    dff = w_down.shape[1]
    ns = s // SLC_BLOCK
    nc = s // CMP_STRIDE
    ovt = jnp.asarray(_overlap_t(nc, ns), BF16)
    xf = x.reshape(t, d)
    tm = min(1024, t)
    tm_wide = min(2048, t)
    w_bf = w_in.astype(BF16)
    w_a = jnp.concatenate([w_bf[:, :, :o_q], w_bf[:, :, o_pool:]], axis=2)
    w_n = jnp.concatenate([w_bf[:, :, o_q:o_pool], jnp.zeros((depth, d, LANES - ngate), BF16)], axis=2)
    row = lambda v: v.reshape(depth, 1, v.shape[-1])
    g2 = lambda k: row(jnp.tile(qk_norm_g[:, k], (1, NSA_GROUPS)))
    g_cmp, g_sel, g_win = g2(1), g2(2), g2(3)
    gq = jnp.broadcast_to(qk_norm_g[:, 0][:, :, None], (depth, HEAD_DIM, QT))
    pe2, w1bd, w2bd = _compress_params(cmp_pe, cmp_w1, cmp_w2)
    n_pool = len(POOL_WINDOWS)
    pw = jnp.einsum("zgio,gh->zgiho", pool_w, jnp.eye(n_pool, dtype=pool_w.dtype)).reshape(depth, bw, bw).astype(BF16)
    sb = jnp.repeat(jnp.swapaxes(sgu_b, 1, 2), bw // SGU_GROUPS, axis=2)
    w_br, w_out, w_dn, w_gu = w_branch.astype(BF16), w_o.astype(BF16), w_down.astype(BF16), w_up.astype(BF16)
    n1, n2, psc, sng = row(norm1_g), row(norm2_g), row(pool_scale), row(sgu_norm_g)

    for l in range(depth):
        za = _norm_matmul(xf, n1, w_a, l, BF16, tm_wide, w_a.shape[2] // IN_PROJ_COL_TILES, "in_proj_a")
        znsa = _norm_matmul(xf, n1, w_n, l, F32, tm_wide, w_n.shape[2], "in_proj_nsa")
        ksa, vst, kwn, vwt = _kv_prep(znsa, g_sel, g_win, l, b, s)
        kc, vct = _compress(znsa, pe2, w1bd, w2bd, g_cmp, l, b, s)
        out_b = _nsa_attn(znsa, gq, l, kc, vct, ovt, ksa, vst, kwn, vwt, b, s)
        xf = _mix_merge(xf, za, out_b, conv_a_w, pw, psc, sng, sgu_w, sb, w_br, w_out, l, s)
        act = _ffn_act(xf, n2, w_gu, conv_ff_w, l, s, tm)
        xf = _ffn_down(xf, act, w_dn, l, tm)
    return xf.reshape(b, s, d)
```
